```python
import math
import jax
import jax.numpy as jnp
from jax import lax
import numpy as np

D_MODEL = 1024
BATCH = 8
SEQ = 8192
DEPTH = 4
DEC_BATCH = 32
DEC_SEQ = 64
PAST_LEN = 1024

CHUNK = 64
CONV_W = 4
NH_M = 4
DQK_M = 128
DV_M = 256
QK_M = NH_M * DQK_M
VM = NH_M * DV_M
NH_S = 16
P_S = 64
N_S = 128
G_S = 4
HPG_S = NH_S // G_S
XS = NH_S * P_S
BC_S = G_S * N_S
CONV_S_DIM = XS + 2 * BC_S
NH_G = 4
DK_G = 128
DV_G = 256
QK_G = NH_G * DK_G
VG = NH_G * DV_G
R_G = 16
TAU_G = 16.0
N_GROUPS = 4
EXP_PER_GROUP = 8
N_EXPERTS = N_GROUPS * EXP_PER_GROUP
TOP_K = 2
D_EXP = 512
ALPHA = (2 * DEPTH) ** 0.25
BETA = (8 * DEPTH) ** -0.25
LN_EPS = 1e-5
COL_SIZES = (2 * QK_M, VM, VM, NH_M, NH_M,
             XS, CONV_S_DIM, NH_S,
             QK_G, QK_G, VG, VG, R_G,
             3 * D_MODEL)
P_IN = sum(COL_SIZES)

kernel_name = 'hybrid_mlstm_ssd_gla_hmoe_stream_step'


def _split_cols(a, sizes):
    idx = [int(i) for i in np.cumsum(sizes)[:-1]]
    return jnp.split(a, idx, axis=-1)


def _layernorm(x, g, b):
    xf = x.astype(jnp.float32)
    mu = xf.mean(-1, keepdims=True)
    var = jnp.square(xf - mu).mean(-1, keepdims=True)
    return ((xf - mu) * lax.rsqrt(var + LN_EPS) * g + b).astype(x.dtype)


def _head_norm(x, g, n_heads, center):
    shp = x.shape
    xf = x.astype(jnp.float32).reshape(shp[:-1] + (n_heads, shp[-1] // n_heads))
    if center:
        xf = xf - xf.mean(-1, keepdims=True)
    xf = xf * lax.rsqrt(jnp.square(xf).mean(-1, keepdims=True) + LN_EPS)
    return (xf.reshape(shp) * g).astype(x.dtype)


def _causal_conv(x, buf, w, b):
    T = x.shape[1]
    xp = jnp.concatenate([buf.astype(x.dtype), x], axis=1)
    y = xp[:, 0:T] * w[0]
    for j in range(1, CONV_W):
        y = y + xp[:, j:j + T] * w[j]
    return y + b, xp[:, T:]


def _chunks(a, L):
    return a.reshape((a.shape[0], a.shape[1] // L, L) + a.shape[2:]).swapaxes(0, 1)


def _unchunk(a):
    return a.swapaxes(0, 1).reshape((a.shape[1], a.shape[0] * a.shape[2]) + a.shape[3:])


def _mlstm_scan(q, k, v, li, lf, C0, n0, m0):
    L = min(CHUNK, q.shape[1])
    mask = jnp.tril(jnp.ones((L, L), dtype=bool))

    def step(carry, inp):
        C, n, m = carry
        qc, kc, vc, lic, lfc = inp
        b = jnp.cumsum(lfc, axis=1)
        mt = b + jnp.maximum(m[:, None], lax.cummax(lic - b, axis=1))
        w_inter = jnp.exp(b + m[:, None] - mt)
        bt = jnp.swapaxes(b, 1, 2)
        lit = jnp.swapaxes(lic, 1, 2)
        mtt = jnp.swapaxes(mt, 1, 2)
        logd = bt[..., :, None] - bt[..., None, :] + lit[..., None, :] - mtt[..., :, None]
        dmat = jnp.exp(jnp.where(mask, logd, -jnp.inf))
        s = jnp.einsum('blhd,bshd->bhls', qc, kc) * dmat
        num = jnp.einsum('bhls,bshv->blhv', s, vc) + w_inter[..., None] * jnp.einsum('bhvd,blhd->blhv', C, qc)
        den = jnp.swapaxes(s.sum(-1), 1, 2) + w_inter * jnp.einsum('bhd,blhd->blh', n, qc)
        h = num / jnp.maximum(jnp.abs(den), jnp.exp(-mt))[..., None]
        mL = mt[:, -1]
        wsrc = jnp.exp(b[:, -1:] - b + lic - mL[:, None])
        decay = jnp.exp(b[:, -1] + m - mL)
        C = decay[..., None, None] * C + jnp.einsum('blh,blhv,blhd->bhvd', wsrc, vc, kc)
        n = decay[..., None] * n + jnp.einsum('blh,blhd->bhd', wsrc, kc)
        return (C, n, mL), h

    carry, hs = lax.scan(step, (C0, n0, m0), tuple(_chunks(a, L) for a in (q, k, v, li, lf)))
    return _unchunk(hs), carry


def _ssd_scan(x, dt, A, Bm, Cm, h0):
    L = min(CHUNK, x.shape[1])
    mask = jnp.tril(jnp.ones((L, L), dtype=bool))

    def step(h, inp):
        xc, dtc, bc, cc = inp
        ca = jnp.cumsum(dtc * A, axis=1)
        cat = jnp.moveaxis(ca, 1, -1)
        seg = jnp.exp(jnp.where(mask, cat[..., :, None] - cat[..., None, :], -jnp.inf))
        cb = jnp.einsum('blgn,bsgn->bgls', cc, bc)
        y = jnp.einsum('bgls,bgjls,bsgjp->blgjp', cb, seg, xc * dtc[..., None])
        y = y + jnp.exp(ca)[..., None] * jnp.einsum('blgn,bgjpn->blgjp', cc, h)
        caL = ca[:, -1]
        wsrc = jnp.exp(caL[:, None] - ca) * dtc
        h = jnp.exp(caL)[..., None, None] * h + jnp.einsum('bsgj,bsgjp,bsgn->bgjpn', wsrc, xc, bc)
        return h, y

    h, ys = lax.scan(step, h0, tuple(_chunks(a, L) for a in (x, dt, Bm, Cm)))
    return _unchunk(ys), h


def _gla_scan(q, k, v, lg, S0):
    L = min(CHUNK, q.shape[1])
    mask = jnp.tril(jnp.ones((L, L), dtype=bool))

    def step(S, inp):
        qc, kc, vc, lgc = inp
        G = jnp.cumsum(lgc, axis=1)
        qg = qc * jnp.exp(G)
        att = jnp.where(mask, jnp.einsum('blhd,bshd->bhls', qg, kc * jnp.exp(-G)), 0.0)
        o = jnp.einsum('bhls,bshv->blhv', att, vc) + jnp.einsum('blhd,bhdv->blhv', qg, S)
        GL = G[:, -1]
        S = jnp.exp(GL)[..., None] * S + jnp.einsum('bshd,bshv->bhdv', kc * jnp.exp(GL[:, None] - G), vc)
        return S, o

    S, os_ = lax.scan(step, S0, tuple(_chunks(a, L) for a in (q, k, v, lg)))
    return _unchunk(os_), S


def _token_mixers(u, st, p):
    B, T, _ = u.shape
    f32 = jnp.float32
    c_m, n_m, m_m, cv_m, h_s, cv_s, s_g = st
    (qk_m, v_m, o_m, i_m, f_m, z_s, xbc_s, dt_s,
     q_g, k_g, v_g, g_g, lr_g, gate) = _split_cols(u @ p['w_in'], COL_SIZES)

    qk_m, cv_m = _causal_conv(qk_m, cv_m, p['mlstm_conv_w'], p['mlstm_conv_b'])
    q_m, k_m = jnp.split(jax.nn.silu(qk_m).astype(f32), 2, axis=-1)
    h_m, (c_m, n_m, m_m) = _mlstm_scan(
        q_m.reshape(B, T, NH_M, DQK_M), k_m.reshape(B, T, NH_M, DQK_M) * DQK_M ** -0.5,
        v_m.astype(f32).reshape(B, T, NH_M, DV_M),
        (i_m + p['mlstm_b_i']).astype(f32),
        jax.nn.log_sigmoid((f_m + p['mlstm_b_f']).astype(f32)),
        c_m.astype(f32), n_m.astype(f32), m_m.astype(f32))
    h_m = _head_norm(h_m.reshape(B, T, VM).astype(u.dtype), p['mlstm_norm_g'], NH_M, True) * jax.nn.sigmoid(o_m)

    xbc, cv_s = _causal_conv(xbc_s, cv_s, p['ssd_conv_w'], p['ssd_conv_b'])
    x_s, b_s, c_s = jnp.split(jax.nn.silu(xbc).astype(f32), [XS, XS + BC_S], axis=-1)
    x_s = x_s.reshape(B, T, G_S, HPG_S, P_S)
    dt = jax.nn.softplus((dt_s + p['ssd_dt_bias']).astype(f32)).reshape(B, T, G_S, HPG_S)
    a = -jnp.exp(p['ssd_a_log'].astype(f32)).reshape(G_S, HPG_S)
    y_s, h_s = _ssd_scan(x_s, dt, a, b_s.reshape(B, T, G_S, N_S), c_s.reshape(B, T, G_S, N_S),
                         h_s.astype(f32).reshape(B, G_S, HPG_S, P_S, N_S))
    y_s = y_s + p['ssd_d'].astype(f32).reshape(G_S, HPG_S, 1) * x_s
    y_s = _head_norm(y_s.reshape(B, T, XS).astype(u.dtype) * jax.nn.silu(z_s), p['ssd_norm_g'], G_S, False)
    h_s = h_s.reshape(B, NH_S, P_S, N_S)

    lg = jax.nn.log_sigmoid((lr_g @ p['gla_w_lr'] + p['gla_b_lr']).astype(f32)) / TAU_G
    o_g, s_g = _gla_scan(
        q_g.astype(f32).reshape(B, T, NH_G, DK_G), k_g.astype(f32).reshape(B, T, NH_G, DK_G) * DK_G ** -0.5,
        v_g.astype(f32).reshape(B, T, NH_G, DV_G), lg.reshape(B, T, NH_G, DK_G), s_g.astype(f32))
    o_g = _head_norm(o_g.reshape(B, T, VG).astype(u.dtype), p['gla_norm_g'], NH_G, False) * jax.nn.silu(g_g)

    gm, gs, gg = jnp.split(jax.nn.sigmoid(gate + p['b_branch']), 3, axis=-1)
    merged = (gm * (h_m @ p['w_branch'][0]) + gs * (y_s @ p['w_branch'][1])
              + gg * (o_g @ p['w_branch'][2]))
    return merged @ p['w_out'], (c_m, n_m, m_m, cv_m, h_s, cv_s, s_g)


def _hier_moe(u, p):
    B, T, D = u.shape
    xt = u.reshape(B * T, D)
    pg = jax.nn.softmax((xt @ p['w_grp'] + p['b_grp']).astype(jnp.float32), axis=-1)
    pg_top, g_idx = lax.top_k(pg, 1)
    el = (xt @ p['w_router'] + p['b_router']).astype(jnp.float32).reshape(-1, N_GROUPS, EXP_PER_GROUP)
    el = jnp.take_along_axis(el, g_idx[:, :, None], axis=1)[:, 0]
    ev, e_idx = lax.top_k(el, TOP_K)
    wts = (pg_top * jax.nn.softmax(ev, axis=-1)).reshape(-1)
    eid = (g_idx * EXP_PER_GROUP + e_idx).reshape(-1).astype(jnp.int32)
    order = jnp.argsort(eid)
    tok = order // TOP_K
    sizes = jnp.bincount(eid, length=N_EXPERTS).astype(jnp.int32)
    xs = xt[tok]
    hg = lax.ragged_dot(xs, p['w_e_gate'], sizes)
    hu = lax.ragged_dot(xs, p['w_e_up'], sizes)
    ye = lax.ragged_dot(jax.nn.silu(hg) * hu, p['w_e_down'], sizes)
    ye = ye * wts[order][:, None].astype(ye.dtype)
    return jnp.zeros_like(xt).at[tok].add(ye).reshape(B, T, D)


def _trunk(x, c, states, params):
    new = [[] for _ in states]
    for l in range(DEPTH):
        p = {name: arr[l] for name, arr in params.items()}
        st = tuple(s[l] for s in states)
        ada = (jax.nn.silu(c) @ p['w_ada'] + p['b_ada'])[:, None]
        sh1, sc1, g1, sh2, sc2, g2 = jnp.split(ada, 6, axis=-1)
        y, st_new = _token_mixers(x * (1 + sc1) + sh1, st, p)
        x = _layernorm(ALPHA * x + g1 * y, p['ln_g'][0], p['ln_b'][0])
        x = _layernorm(ALPHA * x + g2 * _hier_moe(x * (1 + sc2) + sh2, p), p['ln_g'][1], p['ln_b'][1])
        for lst, s in zip(new, st_new):
            lst.append(s.astype(x.dtype))
    return x, tuple(jnp.stack(lst) for lst in new)


def setup_inputs(seed: int = 0) -> dict:
    key = jax.random.key(seed)
    ks = iter(jax.random.split(key, 64))

    def nrm(shape, scale):
        return jax.random.normal(next(ks), shape, jnp.float32) * scale

    D = D_MODEL
    dt0 = jnp.exp(jax.random.uniform(next(ks), (DEPTH, NH_S)) * (math.log(0.1) - math.log(0.001)) + math.log(0.001))
    return {
        'x_prompt': nrm((BATCH, SEQ, D), 1.0),
        'x_sample': nrm((DEC_BATCH, DEC_SEQ, D), 1.0),
        'state_mlstm_c': nrm((DEPTH, DEC_BATCH, NH_M, DV_M, DQK_M), 0.5),
        'state_mlstm_n': nrm((DEPTH, DEC_BATCH, NH_M, DQK_M), 0.5),
        'state_mlstm_m': nrm((DEPTH, DEC_BATCH, NH_M), 1.0),
        'state_mlstm_conv': nrm((DEPTH, DEC_BATCH, CONV_W - 1, 2 * QK_M), 1.0),
        'state_ssd': nrm((DEPTH, DEC_BATCH, NH_S, P_S, N_S), 0.1),
        'state_ssd_conv': nrm((DEPTH, DEC_BATCH, CONV_W - 1, CONV_S_DIM), 1.0),
        'state_gla': nrm((DEPTH, DEC_BATCH, NH_G, DK_G, DV_G), 0.1),
        'c_prompt': nrm((BATCH, D), 1.0),
        'c_sample': nrm((DEC_BATCH, D), 1.0),
        'w_ada': nrm((DEPTH, D, 6 * D), D ** -0.5),
        'b_ada': nrm((DEPTH, 6 * D), 0.01),
        'w_in': nrm((DEPTH, D, P_IN), D ** -0.5),
        'mlstm_b_i': nrm((DEPTH, NH_M), 0.1),
        'mlstm_b_f': jnp.linspace(3.0, 6.0, NH_M)[None] + nrm((DEPTH, NH_M), 0.1),
        'mlstm_conv_w': nrm((DEPTH, CONV_W, 2 * QK_M), CONV_W ** -0.5),
        'mlstm_conv_b': nrm((DEPTH, 2 * QK_M), 0.01),
        'mlstm_norm_g': 1.0 + nrm((DEPTH, VM), 0.02),
        'ssd_conv_w': nrm((DEPTH, CONV_W, CONV_S_DIM), CONV_W ** -0.5),
        'ssd_conv_b': nrm((DEPTH, CONV_S_DIM), 0.01),
        'ssd_dt_bias': dt0 + jnp.log(-jnp.expm1(-dt0)),
        'ssd_a_log': jnp.log(jax.random.uniform(next(ks), (DEPTH, NH_S), minval=1.0, maxval=16.0)),
        'ssd_d': 1.0 + nrm((DEPTH, NH_S), 0.1),
        'ssd_norm_g': 1.0 + nrm((DEPTH, XS), 0.02),
        'gla_w_lr': nrm((DEPTH, R_G, QK_G), R_G ** -0.5),
        'gla_b_lr': nrm((DEPTH, QK_G), 0.1),
        'gla_norm_g': 1.0 + nrm((DEPTH, VG), 0.02),
        'b_branch': nrm((DEPTH, 3 * D), 0.01),
        'w_branch': nrm((DEPTH, 3, VM, D), BETA * VM ** -0.5),
        'w_out': nrm((DEPTH, D, D), BETA * D ** -0.5),
        'ln_g': 1.0 + nrm((DEPTH, 2, D), 0.02),
        'ln_b': nrm((DEPTH, 2, D), 0.01),
        'w_grp': nrm((DEPTH, D, N_GROUPS), D ** -0.5),
        'b_grp': nrm((DEPTH, N_GROUPS), 0.01),
        'w_router': nrm((DEPTH, D, N_EXPERTS), D ** -0.5),
        'b_router': nrm((DEPTH, N_EXPERTS), 0.01),
        'w_e_gate': nrm((DEPTH, N_EXPERTS, D, D_EXP), D ** -0.5),
        'w_e_up': nrm((DEPTH, N_EXPERTS, D, D_EXP), D ** -0.5),
        'w_e_down': nrm((DEPTH, N_EXPERTS, D_EXP, D), BETA * D_EXP ** -0.5),
    }


def reference(x_prompt, x_sample, state_mlstm_c, state_mlstm_n, state_mlstm_m, state_mlstm_conv,
              state_ssd, state_ssd_conv, state_gla, c_prompt, c_sample,
              w_ada, b_ada, w_in, mlstm_b_i, mlstm_b_f, mlstm_conv_w, mlstm_conv_b, mlstm_norm_g,
              ssd_conv_w, ssd_conv_b, ssd_dt_bias, ssd_a_log, ssd_d, ssd_norm_g,
              gla_w_lr, gla_b_lr, gla_norm_g, b_branch, w_branch, w_out, ln_g, ln_b,
              w_grp, b_grp, w_router, b_router, w_e_gate, w_e_up, w_e_down):
    params = {
        'w_ada': w_ada, 'b_ada': b_ada, 'w_in': w_in, 'mlstm_b_i': mlstm_b_i, 'mlstm_b_f': mlstm_b_f,
        'mlstm_conv_w': mlstm_conv_w, 'mlstm_conv_b': mlstm_conv_b, 'mlstm_norm_g': mlstm_norm_g,
        'ssd_conv_w': ssd_conv_w, 'ssd_conv_b': ssd_conv_b, 'ssd_dt_bias': ssd_dt_bias,
        'ssd_a_log': ssd_a_log, 'ssd_d': ssd_d, 'ssd_norm_g': ssd_norm_g,
        'gla_w_lr': gla_w_lr, 'gla_b_lr': gla_b_lr, 'gla_norm_g': gla_norm_g,
        'b_branch': b_branch, 'w_branch': w_branch, 'w_out': w_out, 'ln_g': ln_g, 'ln_b': ln_b,
        'w_grp': w_grp, 'b_grp': b_grp, 'w_router': w_router, 'b_router': b_router,
        'w_e_gate': w_e_gate, 'w_e_up': w_e_up, 'w_e_down': w_e_down,
    }
    sample_states = (state_mlstm_c, state_mlstm_n, state_mlstm_m, state_mlstm_conv,
                     state_ssd, state_ssd_conv, state_gla)
    nb = x_prompt.shape[0]
    prompt_states = tuple(jnp.zeros((s.shape[0], nb) + s.shape[2:], x_prompt.dtype) for s in sample_states)

    y_prompt, new_p = _trunk(x_prompt, c_prompt, prompt_states, params)
    y_sample, new_s = _trunk(x_sample, c_sample, sample_states, params)
    p_mc, p_mn, p_mm, p_mconv, p_ssd, p_sconv, p_gla = new_p
    s_mc, s_mn, s_mm, s_mconv, s_ssd, s_sconv, s_gla = new_s
    return (y_prompt, y_sample,
            p_mc, p_mn, p_mm, p_mconv, p_ssd, p_sconv, p_gla,
            s_mc, s_mn, s_mm, s_mconv, s_ssd, s_sconv, s_gla)
```

```python
import functools

import jax
import jax.numpy as jnp
import numpy as np
from jax import lax
from jax.experimental import pallas as pl
from jax.experimental.pallas import tpu as pltpu

F32 = jnp.float32
BF16 = jnp.bfloat16
NEG_INF = float("-inf")

CHUNK = 64
CONV_W = 4
NH_M, DQK_M, DV_M = 4, 128, 256
QK_M, VM = NH_M * DQK_M, NH_M * DV_M
NH_S, P_S, N_S, G_S = 16, 64, 128, 4
HPG_S = NH_S // G_S
XS, BC_S = NH_S * P_S, G_S * N_S
CONV_S_DIM = XS + 2 * BC_S
NH_G, DK_G, DV_G = 4, 128, 256
QK_G, VG = NH_G * DK_G, NH_G * DV_G
R_G = 16
TAU_G = 16.0
N_GROUPS, EXP_PER_GROUP, TOP_K = 4, 8, 2
N_EXPERTS = N_GROUPS * EXP_PER_GROUP
LN_EPS = 1e-5
LANES = 128

COL_SIZES = (2 * QK_M, VM, VM, NH_M, NH_M, XS, CONV_S_DIM, NH_S, QK_G, QK_G, VG, VG, R_G)
COL_NAMES = ("qk_m", "v_m", "o_m", "i_m", "f_m", "z_s", "xbc_s", "dt_s", "q_g", "k_g", "v_g", "g_g", "lr_g")

VMEM_LIMIT = 56 * 1024 * 1024
MOE_TILE = 256


def _cparams(n_axes):
    return pltpu.CompilerParams(dimension_semantics=("arbitrary",) * n_axes, vmem_limit_bytes=VMEM_LIMIT)


def _dot(a, b):
    return jnp.dot(a, b, preferred_element_type=F32)


def _dot_nt(a, b):
    return lax.dot_general(a, b, (((1,), (1,)), ((), ())), preferred_element_type=F32)


def _split3(x):
    hi = x.astype(BF16)
    r = x - hi.astype(F32)
    mid = r.astype(BF16)
    lo = (r - mid.astype(F32)).astype(BF16)
    return hi, mid, lo


def _sel_left(m, x):
    hi, mid, lo = _split3(x)
    return _dot(m, lo) + _dot(m, mid) + _dot(m, hi)


def _sel_right(x, m):
    hi, mid, lo = _split3(x)
    return _dot(lo, m) + _dot(mid, m) + _dot(hi, m)


def _eye(n, m):
    r = lax.broadcasted_iota(jnp.int32, (n, m), 0)
    c = lax.broadcasted_iota(jnp.int32, (n, m), 1)
    return jnp.where(r == c, 1.0, 0.0).astype(BF16)


def _transpose_rows(x, n):
    e = _eye(n, x.shape[1])
    hi, mid, lo = _split3(x)
    return _dot_nt(e, lo) + _dot_nt(e, mid) + _dot_nt(e, hi)


def _transpose_bf16(x):
    return _dot_nt(_eye(x.shape[1], x.shape[1]), x).astype(BF16)


def _tri_masks(L):
    r = lax.broadcasted_iota(jnp.int32, (L, L), 0)
    c = lax.broadcasted_iota(jnp.int32, (L, L), 1)
    causal = c <= r
    tril = jnp.where(causal, 1.0, 0.0).astype(BF16)
    triu = jnp.where(r <= c, 1.0, 0.0).astype(BF16)
    return causal, tril, triu


def _log_sigmoid(x):
    return jnp.minimum(x, 0.0) - jnp.log1p(jnp.exp(-jnp.abs(x)))


def _softplus(x):
    return jnp.maximum(x, 0.0) + jnp.log1p(jnp.exp(-jnp.abs(x)))


def _silu(x):
    return x * jax.nn.sigmoid(x)


def _modulate(x, ada, shift_row, scale_row):
    return x * (1.0 + ada[scale_row:scale_row + 1]) + ada[shift_row:shift_row + 1]


def _causal_conv(buf, x, w_ref, b_ref, Tt):
    buf[8:8 + Tt, :] = x
    y = buf[5:5 + Tt, :] * w_ref[0, 0:1, :]
    y = y + buf[6:6 + Tt, :] * w_ref[0, 1:2, :]
    y = y + buf[7:7 + Tt, :] * w_ref[0, 2:3, :]
    y = y + x * w_ref[0, 3:4, :]
    y = y + b_ref[0]
    buf[5:8, :] = buf[5 + Tt:8 + Tt, :]
    return y


def _group_norm(x, n_groups, center):
    w = x.shape[1] // n_groups
    outs = []
    for g in range(n_groups):
        xg = x[:, g * w:(g + 1) * w]
        if center:
            xg = xg - jnp.mean(xg, axis=1, keepdims=True)
        outs.append(xg * lax.rsqrt(jnp.mean(xg * xg, axis=1, keepdims=True) + LN_EPS))
    return jnp.concatenate(outs, axis=1)


def _ada_kernel(c_ref, w_ref, b_ref, o_ref):
    c = _silu(c_ref[...]).astype(BF16)
    o_ref[0] = _dot(c, w_ref[0].astype(BF16)) + b_ref[0]


def _ada_call(c_all, w_ada, b_ada):
    depth, d, n6 = w_ada.shape
    nb = c_all.shape[0]
    tn = 1536
    return pl.pallas_call(
        _ada_kernel,
        grid=(depth, n6 // tn),
        in_specs=[pl.BlockSpec((nb, d), lambda l, j: (0, 0)),
                  pl.BlockSpec((1, d, tn), lambda l, j: (l, 0, j)),
                  pl.BlockSpec((1, 1, tn), lambda l, j: (l, 0, j))],
        out_specs=pl.BlockSpec((1, nb, tn), lambda l, j: (l, 0, j)),
        out_shape=jax.ShapeDtypeStruct((depth, nb, n6), F32),
        compiler_params=_cparams(2),
        name="ada",
    )(c_all, w_ada, b_ada.reshape(depth, 1, n6))


def _mlstm_kernel(x_ref, ada_ref, w_ref, cw_ref, cb_ref, bif_ref, ng_ref, c0_ref, n0_ref, m0_ref, cv0_ref,
                  out_ref, c_out, n_out, m_out, cv_out,
                  conv_s, q_s, k_s, v_s, o_s, gate_s, ct_s, n_s, m_s, *, Tt, L):
    t = pl.program_id(1)

    @pl.when(t == 0)
    def _():
        ct_s[...] = c0_ref[0]
        n_s[...] = n0_ref[0]
        m_s[...] = m0_ref[0]
        conv_s[5:8, :] = cv0_ref[0]

    u = _modulate(x_ref[0], ada_ref[0], 0, 1).astype(BF16)
    qk = _silu(_causal_conv(conv_s, _dot(u, w_ref[0, :, 0:2 * QK_M]), cw_ref, cb_ref, Tt))
    q_s[...] = qk[:, :QK_M]
    k_s[...] = qk[:, QK_M:] * DQK_M ** -0.5
    v_s[...] = _dot(u, w_ref[0, :, 2 * QK_M:2 * QK_M + VM]).astype(BF16)
    o_s[...] = _dot(u, w_ref[0, :, 2 * QK_M + VM:2 * QK_M + 2 * VM])
    g = _dot(u, w_ref[0, :, 2 * QK_M + 2 * VM:]) + bif_ref[0]
    lane = lax.broadcasted_iota(jnp.int32, g.shape, 1)
    gate_s[...] = jnp.where(lane < NH_M, g, _log_sigmoid(g))

    causal, tril, triu = _tri_masks(L)

    def chunk(c, carry):
        r0 = pl.multiple_of(c * L, L)
        rows = pl.ds(r0, L)
        gc = gate_s[rows, :]
        cs = _sel_left(tril, gc)
        g_rows = _transpose_rows(gc, 2 * NH_M)
        cs_rows = _sel_right(g_rows, triu)
        for h in range(NH_M):
            li_col, li_row = gc[:, h:h + 1], g_rows[h:h + 1, :]
            b_col, b_row = cs[:, NH_M + h:NH_M + h + 1], cs_rows[NH_M + h:NH_M + h + 1, :]
            m_h = m_s[0:1, h:h + 1]
            r_row = li_row - b_row
            a_col = jnp.max(jnp.where(causal, r_row, NEG_INF), axis=1, keepdims=True)
            mt = b_col + jnp.maximum(m_h, a_col)
            w_inter = jnp.exp(b_col + m_h - mt)
            dmat = jnp.exp(jnp.where(causal, (b_col - mt) + r_row, NEG_INF))
            qh = q_s[rows, h * DQK_M:(h + 1) * DQK_M]
            kh = k_s[rows, h * DQK_M:(h + 1) * DQK_M]
            vh = v_s[rows, h * DV_M:(h + 1) * DV_M]
            qb = qh.astype(BF16)
            s = _dot_nt(qb, kh.astype(BF16)) * dmat
            num = _dot(s.astype(BF16), vh) + w_inter * _dot(qb, ct_s[h].astype(BF16))
            den = jnp.sum(s, axis=1, keepdims=True) + w_inter * jnp.sum(qh * n_s[h:h + 1, :], axis=1, keepdims=True)
            hv = num / jnp.maximum(jnp.abs(den), jnp.exp(-mt))
            m_last = mt[L - 1:L, :]
            b_last = b_col[L - 1:L, :]
            wsrc = jnp.exp(b_last - b_col + li_col - m_last)
            decay = jnp.exp(b_last + m_h - m_last)
            kw = wsrc * kh
            ct_s[h] = decay * ct_s[h] + _dot(_transpose_bf16(kw.astype(BF16)), vh)
            n_s[h:h + 1, :] = decay * n_s[h:h + 1, :] + jnp.sum(kw, axis=0, keepdims=True)
            m_s[0:1, h:h + 1] = m_last
            hs = slice(h * DV_M, (h + 1) * DV_M)
            hn = _group_norm(hv, 1, True) * ng_ref[0, :, hs] * jax.nn.sigmoid(o_s[rows, hs])
            out_ref[0, rows, hs] = hn.astype(out_ref.dtype)
        return carry

    lax.fori_loop(0, Tt // L, chunk, 0)

    @pl.when(t == pl.num_programs(1) - 1)
    def _():
        c_out[0] = ct_s[...]
        n_out[0] = n_s[...]
        m_out[0] = m_s[...]
        cv_out[0] = conv_s[5:8, :]


def _mlstm_call(l, x, ada, w, cw, cb, bif, ng, c0t, n0, m0, cv0, Tt):
    B, T, D = x.shape
    L = min(CHUNK, T)
    wcols = w.shape[2]
    bmap = lambda b, t: (b, 0, 0)
    lmap = lambda b, t: (l, 0, 0)
    return pl.pallas_call(
        functools.partial(_mlstm_kernel, Tt=Tt, L=L),
        grid=(B, T // Tt),
        in_specs=[pl.BlockSpec((1, Tt, D), lambda b, t: (b, t, 0)),
                  pl.BlockSpec((1, 6, D), bmap),
                  pl.BlockSpec((1, D, wcols), lmap),
                  pl.BlockSpec((1, CONV_W, 2 * QK_M), lmap),
                  pl.BlockSpec((1, 1, 2 * QK_M), lmap),
                  pl.BlockSpec((1, 1, LANES), lmap),
                  pl.BlockSpec((1, 1, VM), lmap),
                  pl.BlockSpec((1, NH_M, DQK_M, DV_M), lambda b, t: (b, 0, 0, 0)),
                  pl.BlockSpec((1, NH_M, DQK_M), bmap),
                  pl.BlockSpec((1, 1, LANES), bmap),
                  pl.BlockSpec((1, CONV_W - 1, 2 * QK_M), bmap)],
        out_specs=[pl.BlockSpec((1, Tt, VM), lambda b, t: (b, t, 0)),
                   pl.BlockSpec((1, NH_M, DQK_M, DV_M), lambda b, t: (b, 0, 0, 0)),
                   pl.BlockSpec((1, NH_M, DQK_M), bmap),
                   pl.BlockSpec((1, 1, LANES), bmap),
                   pl.BlockSpec((1, CONV_W - 1, 2 * QK_M), bmap)],
        out_shape=[jax.ShapeDtypeStruct((B, T, VM), BF16),
                   jax.ShapeDtypeStruct((B, NH_M, DQK_M, DV_M), F32),
                   jax.ShapeDtypeStruct((B, NH_M, DQK_M), F32),
                   jax.ShapeDtypeStruct((B, 1, LANES), F32),
                   jax.ShapeDtypeStruct((B, CONV_W - 1, 2 * QK_M), F32)],
        scratch_shapes=[pltpu.VMEM((8 + Tt, 2 * QK_M), F32),
                        pltpu.VMEM((Tt, QK_M), F32),
                        pltpu.VMEM((Tt, QK_M), F32),
                        pltpu.VMEM((Tt, VM), BF16),
                        pltpu.VMEM((Tt, VM), F32),
                        pltpu.VMEM((Tt, LANES), F32),
                        pltpu.VMEM((NH_M, DQK_M, DV_M), F32),
                        pltpu.VMEM((NH_M, DQK_M), F32),
                        pltpu.VMEM((1, LANES), F32)],
        compiler_params=_cparams(2),
        name="mlstm",
    )(x, ada, w, cw, cb, bif, ng, c0t, n0, m0, cv0)


def _ssd_kernel(x_ref, ada_ref, w_ref, cw_ref, cb_ref, dtb_ref, alog_ref, d_ref, ng_ref, h0_ref, cv0_ref,
                out_ref, h_out, cv_out,
                conv_s, u_s, x_s, b_s, c_s, dt_s, y_s, ht_s, *, Tt, L):
    t = pl.program_id(1)

    @pl.when(t == 0)
    def _():
        ht_s[...] = h0_ref[0]
        conv_s[5:8, :] = cv0_ref[0]

    u = _modulate(x_ref[0], ada_ref[0], 0, 1).astype(BF16)
    u_s[...] = u
    xbc = _silu(_causal_conv(conv_s, _dot(u, w_ref[0, :, XS:XS + CONV_S_DIM]), cw_ref, cb_ref, Tt))
    x_s[...] = xbc[:, :XS]
    b_s[...] = xbc[:, XS:XS + BC_S].astype(BF16)
    c_s[...] = xbc[:, XS + BC_S:].astype(BF16)
    dt_s[...] = _softplus(_dot(u, w_ref[0, :, XS + CONV_S_DIM:]) + dtb_ref[0])
    a_row = -jnp.exp(alog_ref[0])

    causal, tril, _ = _tri_masks(L)

    def chunk(c, carry):
        r0 = pl.multiple_of(c * L, L)
        rows = pl.ds(r0, L)
        dtc = dt_s[rows, :]
        cs = _sel_left(tril, dtc * a_row)
        cs_rows = _transpose_rows(cs, NH_S)
        ecs = jnp.exp(cs)
        cs_last = cs[L - 1:L, :]
        wsrc = jnp.exp(cs_last - cs) * dtc
        decay = jnp.exp(cs_last)
        for g in range(G_S):
            bg = b_s[rows, g * N_S:(g + 1) * N_S]
            cg = c_s[rows, g * N_S:(g + 1) * N_S]
            cb = _dot_nt(cg, bg)
            y_inter = _dot(cg, ht_s[g].astype(BF16))
            bgt = _transpose_bf16(bg)
            for j in range(HPG_S):
                hd = g * HPG_S + j
                cols = slice(hd * P_S, (hd + 1) * P_S)
                jc = slice(j * P_S, (j + 1) * P_S)
                xj = x_s[rows, cols]
                seg = jnp.exp(jnp.where(causal, cs[:, hd:hd + 1] - cs_rows[hd:hd + 1, :], NEG_INF))
                mj = (cb * seg).astype(BF16)
                xdt = (xj * dtc[:, hd:hd + 1]).astype(BF16)
                y_s[rows, cols] = _dot(mj, xdt) + ecs[:, hd:hd + 1] * y_inter[:, jc]
                xw = (xj * wsrc[:, hd:hd + 1]).astype(BF16)
                ht_s[g, :, jc] = decay[:, hd:hd + 1] * ht_s[g, :, jc] + _dot(bgt, xw)
        return carry

    lax.fori_loop(0, Tt // L, chunk, 0)

    y = y_s[...] + d_ref[0] * x_s[...]
    y = y * _silu(_dot(u_s[...], w_ref[0, :, 0:XS]))
    out_ref[0] = (_group_norm(y, G_S, False) * ng_ref[0]).astype(out_ref.dtype)

    @pl.when(t == pl.num_programs(1) - 1)
    def _():
        h_out[0] = ht_s[...]
        cv_out[0] = conv_s[5:8, :]


def _ssd_call(l, x, ada, w, cw, cb, dtb, alog, dfull, ng, h0t, cv0, Tt):
    B, T, D = x.shape
    L = min(CHUNK, T)
    wcols = w.shape[2]
    bmap = lambda b, t: (b, 0, 0)
    lmap = lambda b, t: (l, 0, 0)
    hshape = (G_S, N_S, HPG_S * P_S)
    return pl.pallas_call(
        functools.partial(_ssd_kernel, Tt=Tt, L=L),
        grid=(B, T // Tt),
        in_specs=[pl.BlockSpec((1, Tt, D), lambda b, t: (b, t, 0)),
                  pl.BlockSpec((1, 6, D), bmap),
                  pl.BlockSpec((1, D, wcols), lmap),
                  pl.BlockSpec((1, CONV_W, CONV_S_DIM), lmap),
                  pl.BlockSpec((1, 1, CONV_S_DIM), lmap),
                  pl.BlockSpec((1, 1, LANES), lmap),
                  pl.BlockSpec((1, 1, LANES), lmap),
                  pl.BlockSpec((1, 1, XS), lmap),
                  pl.BlockSpec((1, 1, XS), lmap),
                  pl.BlockSpec((1,) + hshape, lambda b, t: (b, 0, 0, 0)),
                  pl.BlockSpec((1, CONV_W - 1, CONV_S_DIM), bmap)],
        out_specs=[pl.BlockSpec((1, Tt, XS), lambda b, t: (b, t, 0)),
                   pl.BlockSpec((1,) + hshape, lambda b, t: (b, 0, 0, 0)),
                   pl.BlockSpec((1, CONV_W - 1, CONV_S_DIM), bmap)],
        out_shape=[jax.ShapeDtypeStruct((B, T, XS), BF16),
                   jax.ShapeDtypeStruct((B,) + hshape, F32),
                   jax.ShapeDtypeStruct((B, CONV_W - 1, CONV_S_DIM), F32)],
        scratch_shapes=[pltpu.VMEM((8 + Tt, CONV_S_DIM), F32),
                        pltpu.VMEM((Tt, D), BF16),
                        pltpu.VMEM((Tt, XS), F32),
                        pltpu.VMEM((Tt, BC_S), BF16),
                        pltpu.VMEM((Tt, BC_S), BF16),
                        pltpu.VMEM((Tt, LANES), F32),
                        pltpu.VMEM((Tt, XS), F32),
                        pltpu.VMEM(hshape, F32)],
        compiler_params=_cparams(2),
        name="ssd",
    )(x, ada, w, cw, cb, dtb, alog, dfull, ng, h0t, cv0)


def _gla_kernel(x_ref, ada_ref, w_ref, wlr_ref, blr_ref, ng_ref, s0_ref,
                out_ref, s_out,
                u_s, q_s, k_s, v_s, lg_s, o_s, st_s, *, Tt, L):
    t = pl.program_id(1)

    @pl.when(t == 0)
    def _():
        st_s[...] = s0_ref[0]

    u = _modulate(x_ref[0], ada_ref[0], 0, 1).astype(BF16)
    u_s[...] = u
    q_s[...] = _dot(u, w_ref[0, :, 0:QK_G])
    k_s[...] = _dot(u, w_ref[0, :, QK_G:2 * QK_G]) * DK_G ** -0.5
    v_s[...] = _dot(u, w_ref[0, :, 2 * QK_G:2 * QK_G + VG]).astype(BF16)
    lr = _dot(u, w_ref[0, :, 2 * QK_G + 2 * VG:]).astype(BF16)
    lg_s[...] = _log_sigmoid(_dot(lr, wlr_ref[0]) + blr_ref[0]) / TAU_G

    causal, tril, _ = _tri_masks(L)

    def chunk(c, carry):
        r0 = pl.multiple_of(c * L, L)
        rows = pl.ds(r0, L)
        G = _sel_left(tril, lg_s[rows, :])
        kc = k_s[rows, :]
        qg = (q_s[rows, :] * jnp.exp(G)).astype(BF16)
        kg = (kc * jnp.exp(-G)).astype(BF16)
        g_last = G[L - 1:L, :]
        kdec = (kc * jnp.exp(g_last - G)).astype(BF16)
        eg_last = jnp.exp(g_last)
        for h in range(NH_G):
            ks = slice(h * DK_G, (h + 1) * DK_G)
            vs = slice(h * DV_G, (h + 1) * DV_G)
            vh = v_s[rows, vs]
            att = jnp.where(causal, _dot_nt(qg[:, ks], kg[:, ks]), 0.0)
            o_s[rows, vs] = _dot(att.astype(BF16), vh) + _dot(qg[:, ks], st_s[h].astype(BF16))
            dcol = _transpose_rows(jnp.broadcast_to(eg_last[:, ks], (8, DK_G)), DK_G)[:, 0:1]
            st_s[h] = dcol * st_s[h] + _dot(_transpose_bf16(kdec[:, ks]), vh)
        return carry

    lax.fori_loop(0, Tt // L, chunk, 0)

    gg = _dot(u_s[...], w_ref[0, :, 2 * QK_G + VG:2 * QK_G + 2 * VG])
    out_ref[0] = (_group_norm(o_s[...], NH_G, False) * ng_ref[0] * _silu(gg)).astype(out_ref.dtype)

    @pl.when(t == pl.num_programs(1) - 1)
    def _():
        s_out[0] = st_s[...]


def _gla_call(l, x, ada, w, wlr, blr, ng, s0, Tt):
    B, T, D = x.shape
    L = min(CHUNK, T)
    wcols = w.shape[2]
    bmap = lambda b, t: (b, 0, 0)
    lmap = lambda b, t: (l, 0, 0)
    sshape = (NH_G, DK_G, DV_G)
    return pl.pallas_call(
        functools.partial(_gla_kernel, Tt=Tt, L=L),
        grid=(B, T // Tt),
        in_specs=[pl.BlockSpec((1, Tt, D), lambda b, t: (b, t, 0)),
                  pl.BlockSpec((1, 6, D), bmap),
                  pl.BlockSpec((1, D, wcols), lmap),
                  pl.BlockSpec((1, LANES, QK_G), lmap),
                  pl.BlockSpec((1, 1, QK_G), lmap),
                  pl.BlockSpec((1, 1, VG), lmap),
                  pl.BlockSpec((1,) + sshape, lambda b, t: (b, 0, 0, 0))],
        out_specs=[pl.BlockSpec((1, Tt, VG), lambda b, t: (b, t, 0)),
                   pl.BlockSpec((1,) + sshape, lambda b, t: (b, 0, 0, 0))],
        out_shape=[jax.ShapeDtypeStruct((B, T, VG), BF16),
                   jax.ShapeDtypeStruct((B,) + sshape, F32)],
        scratch_shapes=[pltpu.VMEM((Tt, D), BF16),
                        pltpu.VMEM((Tt, QK_G), F32),
                        pltpu.VMEM((Tt, QK_G), F32),
                        pltpu.VMEM((Tt, VG), BF16),
                        pltpu.VMEM((Tt, QK_G), F32),
                        pltpu.VMEM((Tt, VG), F32),
                        pltpu.VMEM(sshape, F32)],
        compiler_params=_cparams(2),
        name="gla",
    )(x, ada, w, wlr, blr, ng, s0)


def _layer_norm(x, g, b):
    mu = jnp.mean(x, axis=1, keepdims=True)
    xc = x - mu
    var = jnp.mean(xc * xc, axis=1, keepdims=True)
    return xc * lax.rsqrt(var + LN_EPS) * g + b


def _merge_kernel(x_ref, ada_ref, hm_ref, ys_ref, og_ref, wg_ref, bb_ref, wb_ref, wo_ref, lng_ref, lnb_ref,
                  wrt_ref, brt_ref,
                  x1_ref, u2_ref, eid_ref, wts_ref, *, alpha):
    x = x_ref[0]
    ada = ada_ref[0]
    D = x.shape[1]
    u = _modulate(x, ada, 0, 1).astype(BF16)
    gate = jax.nn.sigmoid(_dot(u, wg_ref[0]) + bb_ref[0])
    merged = (gate[:, 0:D] * _dot(hm_ref[0], wb_ref[0, 0])
              + gate[:, D:2 * D] * _dot(ys_ref[0], wb_ref[0, 1])
              + gate[:, 2 * D:3 * D] * _dot(og_ref[0], wb_ref[0, 2]))
    y = _dot(merged.astype(BF16), wo_ref[0])
    x1 = _layer_norm(alpha * x + ada[2:3] * y, lng_ref[0, 0:1], lnb_ref[0, 0:1])
    x1_ref[0] = x1
    u2 = _modulate(x1, ada, 3, 4).astype(BF16)
    u2_ref[0] = u2

    logits = _dot(u2, wrt_ref[0]) + brt_ref[0]
    lane = lax.broadcasted_iota(jnp.int32, logits.shape, 1)
    lane_f = lane.astype(F32)
    big = float(LANES)
    is_g = (lane >= N_EXPERTS) & (lane < N_EXPERTS + N_GROUPS)
    gmax = jnp.max(jnp.where(is_g, logits, NEG_INF), axis=1, keepdims=True)
    gsum = jnp.sum(jnp.where(is_g, jnp.exp(logits - gmax), 0.0), axis=1, keepdims=True)
    pg_top = 1.0 / gsum
    g_lane = jnp.min(jnp.where(is_g & (logits == gmax), lane_f, big), axis=1, keepdims=True)
    g_idx = g_lane.astype(jnp.int32) - N_EXPERTS
    in_grp = (lane < N_EXPERTS) & ((lane // EXP_PER_GROUP) == g_idx)
    el = jnp.where(in_grp, logits, NEG_INF)
    v1 = jnp.max(el, axis=1, keepdims=True)
    i1 = jnp.min(jnp.where(in_grp & (el == v1), lane_f, big), axis=1, keepdims=True)
    rest = in_grp & (lane_f != i1)
    el2 = jnp.where(rest, logits, NEG_INF)
    v2 = jnp.max(el2, axis=1, keepdims=True)
    i2 = jnp.min(jnp.where(rest & (el2 == v2), lane_f, big), axis=1, keepdims=True)
    e = jnp.exp(v2 - v1)
    w1 = pg_top / (1.0 + e)
    w2 = pg_top * e / (1.0 + e)
    eid_ref[0] = jnp.where(lane == 0, i1, jnp.where(lane == 1, i2, 0.0)).astype(jnp.int32)
    wts_ref[0] = jnp.where(lane == 0, w1, jnp.where(lane == 1, w2, 0.0))


def _merge_call(l, x, ada, hm, ys, og, wg, bb, wb, wo, lng, lnb, wrt, brt, tm, alpha):
    B, T, D = x.shape
    bmap = lambda b, t: (b, 0, 0)
    lmap = lambda b, t: (l, 0, 0)
    tmap = lambda b, t: (b, t, 0)
    return pl.pallas_call(
        functools.partial(_merge_kernel, alpha=alpha),
        grid=(B, T // tm),
        in_specs=[pl.BlockSpec((1, tm, D), tmap),
                  pl.BlockSpec((1, 6, D), bmap),
                  pl.BlockSpec((1, tm, D), tmap),
                  pl.BlockSpec((1, tm, D), tmap),
                  pl.BlockSpec((1, tm, D), tmap),
                  pl.BlockSpec((1, D, 3 * D), lmap),
                  pl.BlockSpec((1, 1, 3 * D), lmap),
                  pl.BlockSpec((1, 3, D, D), lambda b, t: (l, 0, 0, 0)),
                  pl.BlockSpec((1, D, D), lmap),
                  pl.BlockSpec((1, 2, D), lmap),
                  pl.BlockSpec((1, 2, D), lmap),
                  pl.BlockSpec((1, D, LANES), lmap),
                  pl.BlockSpec((1, 1, LANES), lmap)],
        out_specs=[pl.BlockSpec((1, tm, D), tmap),
                   pl.BlockSpec((1, tm, D), tmap),
                   pl.BlockSpec((1, tm, LANES), tmap),
                   pl.BlockSpec((1, tm, LANES), tmap)],
        out_shape=[jax.ShapeDtypeStruct((B, T, D), F32),
                   jax.ShapeDtypeStruct((B, T, D), BF16),
                   jax.ShapeDtypeStruct((B, T, LANES), jnp.int32),
                   jax.ShapeDtypeStruct((B, T, LANES), F32)],
        compiler_params=_cparams(2),
        name="merge",
    )(x, ada, hm, ys, og, wg, bb, wb, wo, lng, lnb, wrt, brt)


def _moe_kernel(te_ref, xs_ref, wg_ref, wu_ref, wd_ref, o_ref):
    del te_ref
    xs = xs_ref[...]
    hg = _dot(xs, wg_ref[0, 0])
    hu = _dot(xs, wu_ref[0, 0])
    o_ref[...] = _dot((_silu(hg) * hu).astype(BF16), wd_ref[0, 0])


def _moe_call(l, tile_expert, xs, weg, weu, wed):
    R, D = xs.shape
    dexp = weg.shape[3]
    tm = MOE_TILE
    return pl.pallas_call(
        _moe_kernel,
        grid_spec=pltpu.PrefetchScalarGridSpec(
            num_scalar_prefetch=1,
            grid=(R // tm,),
            in_specs=[pl.BlockSpec((tm, D), lambda i, te: (i, 0)),
                      pl.BlockSpec((1, 1, D, dexp), lambda i, te: (l, te[i], 0, 0)),
                      pl.BlockSpec((1, 1, D, dexp), lambda i, te: (l, te[i], 0, 0)),
                      pl.BlockSpec((1, 1, dexp, D), lambda i, te: (l, te[i], 0, 0))],
            out_specs=pl.BlockSpec((tm, D), lambda i, te: (i, 0))),
        out_shape=jax.ShapeDtypeStruct((R, D), F32),
        compiler_params=_cparams(1),
        name="moe",
    )(tile_expert, xs, weg, weu, wed)


def _ln2_kernel(x1_ref, ada_ref, y0_ref, y1_ref, wts_ref, lng_ref, lnb_ref, o_ref, *, alpha):
    ada = ada_ref[0]
    wts = wts_ref[0]
    moe = y0_ref[0] * wts[:, 0:1] + y1_ref[0] * wts[:, 1:2]
    o_ref[0] = _layer_norm(alpha * x1_ref[0] + ada[5:6] * moe, lng_ref[0, 1:2], lnb_ref[0, 1:2])


def _ln2_call(l, x1, ada, y0, y1, wts, lng, lnb, tm, alpha):
    B, T, D = x1.shape
    bmap = lambda b, t: (b, 0, 0)
    lmap = lambda b, t: (l, 0, 0)
    tmap = lambda b, t: (b, t, 0)
    return pl.pallas_call(
        functools.partial(_ln2_kernel, alpha=alpha),
        grid=(B, T // tm),
        in_specs=[pl.BlockSpec((1, tm, D), tmap),
                  pl.BlockSpec((1, 6, D), bmap),
                  pl.BlockSpec((1, tm, D), tmap),
                  pl.BlockSpec((1, tm, D), tmap),
                  pl.BlockSpec((1, tm, LANES), tmap),
                  pl.BlockSpec((1, 2, D), lmap),
                  pl.BlockSpec((1, 2, D), lmap)],
        out_specs=pl.BlockSpec((1, tm, D), tmap),
        out_shape=jax.ShapeDtypeStruct((B, T, D), F32),
        compiler_params=_cparams(2),
        name="ln2",
    )(x1, ada, y0, y1, wts, lng, lnb)


def _route(eid, n_tok):
    tm = MOE_TILE
    flat = eid.reshape(-1)
    n_asg = flat.shape[0]
    n_rows = n_asg + N_EXPERTS * tm
    order = jnp.argsort(flat)
    sizes = jnp.bincount(flat, length=N_EXPERTS).astype(jnp.int32)
    off = jnp.cumsum(sizes) - sizes
    psz = ((sizes + tm - 1) // tm) * tm
    pend = jnp.cumsum(psz)
    poff = pend - psz
    sorted_e = flat[order]
    dest_sorted = poff[sorted_e] + (jnp.arange(n_asg, dtype=jnp.int32) - off[sorted_e])
    row_tok = jnp.zeros((n_rows,), jnp.int32).at[dest_sorted].set((order // TOP_K).astype(jnp.int32))
    dest = jnp.zeros((n_asg,), jnp.int32).at[order].set(dest_sorted)
    tile_start = jnp.arange(n_rows // tm, dtype=jnp.int32) * tm
    tile_expert = jnp.minimum(jnp.searchsorted(pend, tile_start, side="right"), N_EXPERTS - 1).astype(jnp.int32)
    return row_tok, dest.reshape(n_tok, TOP_K), tile_expert


def _pad_lanes(a, width=LANES):
    return jnp.pad(a, [(0, 0)] * (a.ndim - 1) + [(0, width - a.shape[-1])])


def _time_block(T):
    return min(T, 256)


def _trunk(x, ada_all, states, P, depth, alpha):
    B, T, D = x.shape
    Tt = _time_block(T)
    c_m, n_m, m_m, cv_m, h_s, cv_s, s_g = states
    new = [[] for _ in range(7)]
    for l in range(depth):
        ada = ada_all[l].reshape(B, 6, D)
        c0t = jnp.swapaxes(c_m[l], -1, -2)
        m0 = _pad_lanes(m_m[l])[:, None, :]
        h0t = (h_s[l].reshape(B, G_S, HPG_S, P_S, N_S).transpose(0, 1, 4, 2, 3)
               .reshape(B, G_S, N_S, HPG_S * P_S))
        hm, c_t, n_n, m_n, cvm_n = _mlstm_call(l, x, ada, P["w_mlstm"], P["mlstm_conv_w"], P["mlstm_conv_b"],
                                               P["mlstm_bif"], P["mlstm_norm_g"], c0t, n_m[l], m0, cv_m[l], Tt)
        ys, h_t, cvs_n = _ssd_call(l, x, ada, P["w_ssd"], P["ssd_conv_w"], P["ssd_conv_b"], P["ssd_dtb"],
                                   P["ssd_alog"], P["ssd_dfull"], P["ssd_norm_g"], h0t, cv_s[l], Tt)
        og, s_n = _gla_call(l, x, ada, P["w_gla"], P["gla_w_lr"], P["gla_b_lr"], P["gla_norm_g"], s_g[l], Tt)
        x1, u2, eid, wts = _merge_call(l, x, ada, hm, ys, og, P["w_gate"], P["b_branch"], P["w_branch"],
                                       P["w_out"], P["ln_g"], P["ln_b"], P["w_rt"], P["b_rt"], Tt, alpha)
        row_tok, dest, tile_expert = _route(eid[:, :, :TOP_K], B * T)
        xs = jnp.take(u2.reshape(B * T, D), row_tok, axis=0)
        ye = _moe_call(l, tile_expert, xs, P["w_e_gate"], P["w_e_up"], P["w_e_down"])
        y0 = jnp.take(ye, dest[:, 0], axis=0).reshape(B, T, D)
        y1 = jnp.take(ye, dest[:, 1], axis=0).reshape(B, T, D)
        x = _ln2_call(l, x1, ada, y0, y1, wts, P["ln_g"], P["ln_b"], Tt, alpha)

        new[0].append(jnp.swapaxes(c_t, -1, -2))
        new[1].append(n_n)
        new[2].append(m_n[:, 0, :NH_M])
        new[3].append(cvm_n)
        new[4].append(h_t.reshape(B, G_S, N_S, HPG_S, P_S).transpose(0, 1, 3, 4, 2).reshape(B, NH_S, P_S, N_S))
        new[5].append(cvs_n)
        new[6].append(s_n)
    return x, tuple(jnp.stack(lst) for lst in new)


def kernel(x_prompt, x_sample, state_mlstm_c, state_mlstm_n, state_mlstm_m, state_mlstm_conv, state_ssd, state_ssd_conv, state_gla, c_prompt, c_sample, w_ada, b_ada, w_in, mlstm_b_i, mlstm_b_f, mlstm_conv_w, mlstm_conv_b, mlstm_norm_g, ssd_conv_w, ssd_conv_b, ssd_dt_bias, ssd_a_log, ssd_d, ssd_norm_g, gla_w_lr, gla_b_lr, gla_norm_g, b_branch, w_branch, w_out, ln_g, ln_b, w_grp, b_grp, w_router, b_router, w_e_gate, w_e_up, w_e_down):
    depth, D, _ = w_in.shape
    alpha = (2 * depth) ** 0.25
    nbp = x_prompt.shape[0]

    edges = np.concatenate([[0], np.cumsum(COL_SIZES)])
    col = {n: w_in[:, :, int(edges[i]):int(edges[i + 1])] for i, n in enumerate(COL_NAMES)}
    w_gate = w_in[:, :, int(edges[-1]):].astype(BF16)
    cat = lambda parts: jnp.concatenate(parts, axis=-1).astype(BF16)
    row = lambda a: a[:, None, :]
    P = {
        "w_mlstm": cat([col["qk_m"], col["v_m"], col["o_m"], _pad_lanes(jnp.concatenate([col["i_m"], col["f_m"]], -1))]),
        "w_ssd": cat([col["z_s"], col["xbc_s"], _pad_lanes(col["dt_s"])]),
        "w_gla": cat([col["q_g"], col["k_g"], col["v_g"], col["g_g"], _pad_lanes(col["lr_g"])]),
        "w_gate": w_gate,
        "mlstm_conv_w": mlstm_conv_w, "mlstm_conv_b": row(mlstm_conv_b),
        "mlstm_bif": row(_pad_lanes(jnp.concatenate([mlstm_b_i, mlstm_b_f], -1))),
        "mlstm_norm_g": row(mlstm_norm_g),
        "ssd_conv_w": ssd_conv_w, "ssd_conv_b": row(ssd_conv_b),
        "ssd_dtb": row(_pad_lanes(ssd_dt_bias)), "ssd_alog": row(_pad_lanes(ssd_a_log)),
        "ssd_dfull": row(jnp.repeat(ssd_d, P_S, axis=-1)), "ssd_norm_g": row(ssd_norm_g),
        "gla_w_lr": jnp.pad(gla_w_lr, ((0, 0), (0, LANES - R_G), (0, 0))).astype(BF16),
        "gla_b_lr": row(gla_b_lr), "gla_norm_g": row(gla_norm_g),
        "b_branch": row(b_branch), "w_branch": w_branch.astype(BF16), "w_out": w_out.astype(BF16),
        "ln_g": ln_g, "ln_b": ln_b,
        "w_rt": _pad_lanes(jnp.concatenate([w_router, w_grp], -1)).astype(BF16),
        "b_rt": row(_pad_lanes(jnp.concatenate([b_router, b_grp], -1))),
        "w_e_gate": w_e_gate.astype(BF16), "w_e_up": w_e_up.astype(BF16), "w_e_down": w_e_down.astype(BF16),
    }

    ada_all = _ada_call(jnp.concatenate([c_prompt, c_sample], axis=0), w_ada, b_ada)

    sample_states = (state_mlstm_c, state_mlstm_n, state_mlstm_m, state_mlstm_conv,
                     state_ssd, state_ssd_conv, state_gla)
    prompt_states = tuple(jnp.zeros((s.shape[0], nbp) + s.shape[2:], x_prompt.dtype) for s in sample_states)

    y_prompt, new_p = _trunk(x_prompt, ada_all[:, :nbp], prompt_states, P, depth, alpha)
    y_sample, new_s = _trunk(x_sample, ada_all[:, nbp:], sample_states, P, depth, alpha)
    return (y_prompt, y_sample) + new_p + new_s
```

```python
import functools

import jax
import jax.numpy as jnp
import numpy as np
from jax import lax
from jax.experimental import pallas as pl
from jax.experimental.pallas import tpu as pltpu

F32 = jnp.float32
BF16 = jnp.bfloat16
NEG_INF = float("-inf")

CHUNK = 64
CONV_W = 4
NH_M, DQK_M, DV_M = 4, 128, 256
QK_M, VM = NH_M * DQK_M, NH_M * DV_M
NH_S, P_S, N_S, G_S = 16, 64, 128, 4
HPG_S = NH_S // G_S
XS, BC_S = NH_S * P_S, G_S * N_S
CONV_S_DIM = XS + 2 * BC_S
NH_G, DK_G, DV_G = 4, 128, 256
QK_G, VG = NH_G * DK_G, NH_G * DV_G
R_G = 16
TAU_G = 16.0
N_GROUPS, EXP_PER_GROUP, TOP_K = 4, 8, 2
N_EXPERTS = N_GROUPS * EXP_PER_GROUP
LN_EPS = 1e-5
LANES = 128

COL_SIZES = (2 * QK_M, VM, VM, NH_M, NH_M, XS, CONV_S_DIM, NH_S, QK_G, QK_G, VG, VG, R_G)
COL_NAMES = ("qk_m", "v_m", "o_m", "i_m", "f_m", "z_s", "xbc_s", "dt_s", "q_g", "k_g", "v_g", "g_g", "lr_g")

VMEM_LIMIT = 56 * 1024 * 1024
MOE_TILE = 256


def _cparams(n_axes):
    return pltpu.CompilerParams(dimension_semantics=("arbitrary",) * n_axes, vmem_limit_bytes=VMEM_LIMIT)


def _dot(a, b):
    return jnp.dot(a, b, preferred_element_type=F32)


def _dot_nt(a, b):
    return lax.dot_general(a, b, (((1,), (1,)), ((), ())), preferred_element_type=F32)


def _split3(x):
    hi = x.astype(BF16)
    r = x - hi.astype(F32)
    mid = r.astype(BF16)
    lo = (r - mid.astype(F32)).astype(BF16)
    return hi, mid, lo


def _sel_left(m, x):
    hi, mid, lo = _split3(x)
    return _dot(m, lo) + _dot(m, mid) + _dot(m, hi)


def _sel_right(x, m):
    hi, mid, lo = _split3(x)
    return _dot(lo, m) + _dot(mid, m) + _dot(hi, m)


def _eye(n, m):
    r = lax.broadcasted_iota(jnp.int32, (n, m), 0)
    c = lax.broadcasted_iota(jnp.int32, (n, m), 1)
    return jnp.where(r == c, 1.0, 0.0).astype(BF16)


def _transpose_rows(x, n):
    e = _eye(n, x.shape[1])
    hi, mid, lo = _split3(x)
    return _dot_nt(e, lo) + _dot_nt(e, mid) + _dot_nt(e, hi)


def _transpose_bf16(x):
    return _dot_nt(_eye(x.shape[1], x.shape[1]), x).astype(BF16)


def _tri_masks(L):
    r = lax.broadcasted_iota(jnp.int32, (L, L), 0)
    c = lax.broadcasted_iota(jnp.int32, (L, L), 1)
    causal = c <= r
    tril = jnp.where(causal, 1.0, 0.0).astype(BF16)
    triu = jnp.where(r <= c, 1.0, 0.0).astype(BF16)
    return causal, tril, triu


def _log_sigmoid(x):
    return jnp.minimum(x, 0.0) - jnp.log1p(jnp.exp(-jnp.abs(x)))


def _softplus(x):
    return jnp.maximum(x, 0.0) + jnp.log1p(jnp.exp(-jnp.abs(x)))


def _silu(x):
    return x * jax.nn.sigmoid(x)


def _modulate(x, ada, shift_row, scale_row):
    return x * (1.0 + ada[scale_row:scale_row + 1]) + ada[shift_row:shift_row + 1]


def _causal_conv(buf, x, w_ref, b_ref, Tt):
    buf[8:8 + Tt, :] = x
    y = buf[5:5 + Tt, :] * w_ref[0, 0:1, :]
    y = y + buf[6:6 + Tt, :] * w_ref[0, 1:2, :]
    y = y + buf[7:7 + Tt, :] * w_ref[0, 2:3, :]
    y = y + x * w_ref[0, 3:4, :]
    y = y + b_ref[0]
    buf[5:8, :] = buf[5 + Tt:8 + Tt, :]
    return y


def _group_norm(x, n_groups, center):
    w = x.shape[1] // n_groups
    outs = []
    for g in range(n_groups):
        xg = x[:, g * w:(g + 1) * w]
        if center:
            xg = xg - jnp.mean(xg, axis=1, keepdims=True)
        outs.append(xg * lax.rsqrt(jnp.mean(xg * xg, axis=1, keepdims=True) + LN_EPS))
    return jnp.concatenate(outs, axis=1)


def _ada_kernel(c_ref, w_ref, b_ref, o_ref):
    c = _silu(c_ref[...]).astype(BF16)
    o_ref[0] = _dot(c, w_ref[0].astype(BF16)) + b_ref[0]


def _ada_call(c_all, w_ada, b_ada):
    depth, d, n6 = w_ada.shape
    nb = c_all.shape[0]
    tn = 1536
    return pl.pallas_call(
        _ada_kernel,
        grid=(depth, n6 // tn),
        in_specs=[pl.BlockSpec((nb, d), lambda l, j: (0, 0)),
                  pl.BlockSpec((1, d, tn), lambda l, j: (l, 0, j)),
                  pl.BlockSpec((1, 1, tn), lambda l, j: (l, 0, j))],
        out_specs=pl.BlockSpec((1, nb, tn), lambda l, j: (l, 0, j)),
        out_shape=jax.ShapeDtypeStruct((depth, nb, n6), F32),
        compiler_params=_cparams(2),
        name="ada",
    )(c_all, w_ada, b_ada.reshape(depth, 1, n6))


def _mlstm_kernel(x_ref, ada_ref, w_ref, cw_ref, cb_ref, bif_ref, ng_ref, c0_ref, n0_ref, m0_ref, cv0_ref,
                  out_ref, c_out, n_out, m_out, cv_out,
                  conv_s, q_s, k_s, v_s, o_s, gate_s, ct_s, n_s, m_s, *, Tt, L):
    t = pl.program_id(1)

    @pl.when(t == 0)
    def _():
        ct_s[...] = c0_ref[0]
        n_s[...] = n0_ref[0]
        m_s[...] = m0_ref[0]
        conv_s[5:8, :] = cv0_ref[0]

    u = _modulate(x_ref[0], ada_ref[0], 0, 1).astype(BF16)
    qk = _silu(_causal_conv(conv_s, _dot(u, w_ref[0, :, 0:2 * QK_M]), cw_ref, cb_ref, Tt))
    q_s[...] = qk[:, :QK_M]
    k_s[...] = qk[:, QK_M:] * DQK_M ** -0.5
    v_s[...] = _dot(u, w_ref[0, :, 2 * QK_M:2 * QK_M + VM]).astype(BF16)
    o_s[...] = _dot(u, w_ref[0, :, 2 * QK_M + VM:2 * QK_M + 2 * VM])
    g = _dot(u, w_ref[0, :, 2 * QK_M + 2 * VM:]) + bif_ref[0]
    lane = lax.broadcasted_iota(jnp.int32, g.shape, 1)
    gate_s[...] = jnp.where(lane < NH_M, g, _log_sigmoid(g))

    causal, tril, triu = _tri_masks(L)

    def chunk(c, carry):
        r0 = pl.multiple_of(c * L, L)
        rows = pl.ds(r0, L)
        gc = gate_s[rows, :]
        cs = _sel_left(tril, gc)
        g_rows = _transpose_rows(gc, 2 * NH_M)
        cs_rows = _sel_right(g_rows, triu)
        for h in range(NH_M):
            li_col, li_row = gc[:, h:h + 1], g_rows[h:h + 1, :]
            b_col, b_row = cs[:, NH_M + h:NH_M + h + 1], cs_rows[NH_M + h:NH_M + h + 1, :]
            m_h = m_s[0:1, h:h + 1]
            r_row = li_row - b_row
            a_col = jnp.max(jnp.where(causal, r_row, NEG_INF), axis=1, keepdims=True)
            mt = b_col + jnp.maximum(m_h, a_col)
            w_inter = jnp.exp(b_col + m_h - mt)
            dmat = jnp.exp(jnp.where(causal, (b_col - mt) + r_row, NEG_INF))
            qh = q_s[rows, h * DQK_M:(h + 1) * DQK_M]
            kh = k_s[rows, h * DQK_M:(h + 1) * DQK_M]
            vh = v_s[rows, h * DV_M:(h + 1) * DV_M]
            qb = qh.astype(BF16)
            s = _dot_nt(qb, kh.astype(BF16)) * dmat
            num = _dot(s.astype(BF16), vh) + w_inter * _dot(qb, ct_s[h].astype(BF16))
            den = jnp.sum(s, axis=1, keepdims=True) + w_inter * jnp.sum(qh * n_s[h:h + 1, :], axis=1, keepdims=True)
            hv = num / jnp.maximum(jnp.abs(den), jnp.exp(-mt))
            m_last = mt[L - 1:L, :]
            b_last = b_col[L - 1:L, :]
            wsrc = jnp.exp(b_last - b_col + li_col - m_last)
            decay = jnp.exp(b_last + m_h - m_last)
            kw = wsrc * kh
            ct_s[h] = decay * ct_s[h] + _dot(_transpose_bf16(kw.astype(BF16)), vh)
            n_s[h:h + 1, :] = decay * n_s[h:h + 1, :] + jnp.sum(kw, axis=0, keepdims=True)
            m_s[0:1, h:h + 1] = m_last
            hs = slice(h * DV_M, (h + 1) * DV_M)
            hn = _group_norm(hv, 1, True) * ng_ref[0, :, hs] * jax.nn.sigmoid(o_s[rows, hs])
            out_ref[0, rows, hs] = hn.astype(out_ref.dtype)
        return carry

    lax.fori_loop(0, Tt // L, chunk, 0)

    @pl.when(t == pl.num_programs(1) - 1)
    def _():
        c_out[0] = ct_s[...]
        n_out[0] = n_s[...]
        m_out[0] = m_s[...]
        cv_out[0] = conv_s[5:8, :]


def _mlstm_call(l, x, ada, w, cw, cb, bif, ng, c0t, n0, m0, cv0, Tt):
    B, T, D = x.shape
    L = min(CHUNK, T)
    wcols = w.shape[2]
    bmap = lambda b, t: (b, 0, 0)
    lmap = lambda b, t: (l, 0, 0)
    return pl.pallas_call(
        functools.partial(_mlstm_kernel, Tt=Tt, L=L),
        grid=(B, T // Tt),
        in_specs=[pl.BlockSpec((1, Tt, D), lambda b, t: (b, t, 0)),
                  pl.BlockSpec((1, 6, D), bmap),
                  pl.BlockSpec((1, D, wcols), lmap),
                  pl.BlockSpec((1, CONV_W, 2 * QK_M), lmap),
                  pl.BlockSpec((1, 1, 2 * QK_M), lmap),
                  pl.BlockSpec((1, 1, LANES), lmap),
                  pl.BlockSpec((1, 1, VM), lmap),
                  pl.BlockSpec((1, NH_M, DQK_M, DV_M), lambda b, t: (b, 0, 0, 0)),
                  pl.BlockSpec((1, NH_M, DQK_M), bmap),
                  pl.BlockSpec((1, 1, LANES), bmap),
                  pl.BlockSpec((1, CONV_W - 1, 2 * QK_M), bmap)],
        out_specs=[pl.BlockSpec((1, Tt, VM), lambda b, t: (b, t, 0)),
                   pl.BlockSpec((1, NH_M, DQK_M, DV_M), lambda b, t: (b, 0, 0, 0)),
                   pl.BlockSpec((1, NH_M, DQK_M), bmap),
                   pl.BlockSpec((1, 1, LANES), bmap),
                   pl.BlockSpec((1, CONV_W - 1, 2 * QK_M), bmap)],
        out_shape=[jax.ShapeDtypeStruct((B, T, VM), BF16),
                   jax.ShapeDtypeStruct((B, NH_M, DQK_M, DV_M), F32),
                   jax.ShapeDtypeStruct((B, NH_M, DQK_M), F32),
                   jax.ShapeDtypeStruct((B, 1, LANES), F32),
                   jax.ShapeDtypeStruct((B, CONV_W - 1, 2 * QK_M), F32)],
        scratch_shapes=[pltpu.VMEM((8 + Tt, 2 * QK_M), F32),
                        pltpu.VMEM((Tt, QK_M), F32),
                        pltpu.VMEM((Tt, QK_M), F32),
                        pltpu.VMEM((Tt, VM), BF16),
                        pltpu.VMEM((Tt, VM), F32),
                        pltpu.VMEM((Tt, LANES), F32),
                        pltpu.VMEM((NH_M, DQK_M, DV_M), F32),
                        pltpu.VMEM((NH_M, DQK_M), F32),
                        pltpu.VMEM((1, LANES), F32)],
        compiler_params=_cparams(2),
        name="mlstm",
    )(x, ada, w, cw, cb, bif, ng, c0t, n0, m0, cv0)


def _ssd_kernel(x_ref, ada_ref, w_ref, cw_ref, cb_ref, dtb_ref, alog_ref, d_ref, ng_ref, h0_ref, cv0_ref,
                out_ref, h_out, cv_out,
                conv_s, u_s, x_s, b_s, c_s, dt_s, y_s, ht_s, *, Tt, L):
    t = pl.program_id(1)

    @pl.when(t == 0)
    def _():
        ht_s[...] = h0_ref[0]
        conv_s[5:8, :] = cv0_ref[0]

    u = _modulate(x_ref[0], ada_ref[0], 0, 1).astype(BF16)
    u_s[...] = u
    xbc = _silu(_causal_conv(conv_s, _dot(u, w_ref[0, :, XS:XS + CONV_S_DIM]), cw_ref, cb_ref, Tt))
    x_s[...] = xbc[:, :XS]
    b_s[...] = xbc[:, XS:XS + BC_S].astype(BF16)
    c_s[...] = xbc[:, XS + BC_S:].astype(BF16)
    dt_s[...] = _softplus(_dot(u, w_ref[0, :, XS + CONV_S_DIM:]) + dtb_ref[0])
    a_row = -jnp.exp(alog_ref[0])

    causal, tril, _ = _tri_masks(L)

    def chunk(c, carry):
        r0 = pl.multiple_of(c * L, L)
        rows = pl.ds(r0, L)
        dtc = dt_s[rows, :]
        cs = _sel_left(tril, dtc * a_row)
        cs_rows = _transpose_rows(cs, NH_S)
        ecs = jnp.exp(cs)
        cs_last = cs[L - 1:L, :]
        wsrc = jnp.exp(cs_last - cs) * dtc
        decay = jnp.exp(cs_last)
        for g in range(G_S):
            bg = b_s[rows, g * N_S:(g + 1) * N_S]
            cg = c_s[rows, g * N_S:(g + 1) * N_S]
            cb = _dot_nt(cg, bg)
            y_inter = _dot(cg, ht_s[g].astype(BF16))
            bgt = _transpose_bf16(bg)
            for j in range(HPG_S):
                hd = g * HPG_S + j
                cols = slice(hd * P_S, (hd + 1) * P_S)
                jc = slice(j * P_S, (j + 1) * P_S)
                xj = x_s[rows, cols]
                seg = jnp.exp(jnp.where(causal, cs[:, hd:hd + 1] - cs_rows[hd:hd + 1, :], NEG_INF))
                mj = (cb * seg).astype(BF16)
                xdt = (xj * dtc[:, hd:hd + 1]).astype(BF16)
                y_s[rows, cols] = _dot(mj, xdt) + ecs[:, hd:hd + 1] * y_inter[:, jc]
                xw = (xj * wsrc[:, hd:hd + 1]).astype(BF16)
                ht_s[g, :, jc] = decay[:, hd:hd + 1] * ht_s[g, :, jc] + _dot(bgt, xw)
        return carry

    lax.fori_loop(0, Tt // L, chunk, 0)

    y = y_s[...] + d_ref[0] * x_s[...]
    y = y * _silu(_dot(u_s[...], w_ref[0, :, 0:XS]))
    out_ref[0] = (_group_norm(y, G_S, False) * ng_ref[0]).astype(out_ref.dtype)

    @pl.when(t == pl.num_programs(1) - 1)
    def _():
        h_out[0] = ht_s[...]
        cv_out[0] = conv_s[5:8, :]


def _ssd_call(l, x, ada, w, cw, cb, dtb, alog, dfull, ng, h0t, cv0, Tt):
    B, T, D = x.shape
    L = min(CHUNK, T)
    wcols = w.shape[2]
    bmap = lambda b, t: (b, 0, 0)
    lmap = lambda b, t: (l, 0, 0)
    hshape = (G_S, N_S, HPG_S * P_S)
    return pl.pallas_call(
        functools.partial(_ssd_kernel, Tt=Tt, L=L),
        grid=(B, T // Tt),
        in_specs=[pl.BlockSpec((1, Tt, D), lambda b, t: (b, t, 0)),
                  pl.BlockSpec((1, 6, D), bmap),
                  pl.BlockSpec((1, D, wcols), lmap),
                  pl.BlockSpec((1, CONV_W, CONV_S_DIM), lmap),
                  pl.BlockSpec((1, 1, CONV_S_DIM), lmap),
                  pl.BlockSpec((1, 1, LANES), lmap),
                  pl.BlockSpec((1, 1, LANES), lmap),
                  pl.BlockSpec((1, 1, XS), lmap),
                  pl.BlockSpec((1, 1, XS), lmap),
                  pl.BlockSpec((1,) + hshape, lambda b, t: (b, 0, 0, 0)),
                  pl.BlockSpec((1, CONV_W - 1, CONV_S_DIM), bmap)],
        out_specs=[pl.BlockSpec((1, Tt, XS), lambda b, t: (b, t, 0)),
                   pl.BlockSpec((1,) + hshape, lambda b, t: (b, 0, 0, 0)),
                   pl.BlockSpec((1, CONV_W - 1, CONV_S_DIM), bmap)],
        out_shape=[jax.ShapeDtypeStruct((B, T, XS), BF16),
                   jax.ShapeDtypeStruct((B,) + hshape, F32),
                   jax.ShapeDtypeStruct((B, CONV_W - 1, CONV_S_DIM), F32)],
        scratch_shapes=[pltpu.VMEM((8 + Tt, CONV_S_DIM), F32),
                        pltpu.VMEM((Tt, D), BF16),
                        pltpu.VMEM((Tt, XS), F32),
                        pltpu.VMEM((Tt, BC_S), BF16),
                        pltpu.VMEM((Tt, BC_S), BF16),
                        pltpu.VMEM((Tt, LANES), F32),
                        pltpu.VMEM((Tt, XS), F32),
                        pltpu.VMEM(hshape, F32)],
        compiler_params=_cparams(2),
        name="ssd",
    )(x, ada, w, cw, cb, dtb, alog, dfull, ng, h0t, cv0)


def _gla_kernel(x_ref, ada_ref, w_ref, wlr_ref, blr_ref, ng_ref, s0_ref,
                out_ref, s_out,
                u_s, q_s, k_s, v_s, lg_s, o_s, st_s, *, Tt, L):
    t = pl.program_id(1)

    @pl.when(t == 0)
    def _():
        st_s[...] = s0_ref[0]

    u = _modulate(x_ref[0], ada_ref[0], 0, 1).astype(BF16)
    u_s[...] = u
    q_s[...] = _dot(u, w_ref[0, :, 0:QK_G])
    k_s[...] = _dot(u, w_ref[0, :, QK_G:2 * QK_G]) * DK_G ** -0.5
    v_s[...] = _dot(u, w_ref[0, :, 2 * QK_G:2 * QK_G + VG]).astype(BF16)
    lr = _dot(u, w_ref[0, :, 2 * QK_G + 2 * VG:]).astype(BF16)
    lg_s[...] = _log_sigmoid(_dot(lr, wlr_ref[0]) + blr_ref[0]) / TAU_G

    causal, tril, _ = _tri_masks(L)

    def chunk(c, carry):
        r0 = pl.multiple_of(c * L, L)
        rows = pl.ds(r0, L)
        G = _sel_left(tril, lg_s[rows, :])
        kc = k_s[rows, :]
        qg = (q_s[rows, :] * jnp.exp(G)).astype(BF16)
        kg = (kc * jnp.exp(-G)).astype(BF16)
        g_last = G[L - 1:L, :]
        kdec = (kc * jnp.exp(g_last - G)).astype(BF16)
        eg_last = jnp.exp(g_last)
        for h in range(NH_G):
            ks = slice(h * DK_G, (h + 1) * DK_G)
            vs = slice(h * DV_G, (h + 1) * DV_G)
            vh = v_s[rows, vs]
            att = jnp.where(causal, _dot_nt(qg[:, ks], kg[:, ks]), 0.0)
            o_s[rows, vs] = _dot(att.astype(BF16), vh) + _dot(qg[:, ks], st_s[h].astype(BF16))
            dcol = _transpose_rows(jnp.broadcast_to(eg_last[:, ks], (8, DK_G)), DK_G)[:, 0:1]
            st_s[h] = dcol * st_s[h] + _dot(_transpose_bf16(kdec[:, ks]), vh)
        return carry

    lax.fori_loop(0, Tt // L, chunk, 0)

    gg = _dot(u_s[...], w_ref[0, :, 2 * QK_G + VG:2 * QK_G + 2 * VG])
    out_ref[0] = (_group_norm(o_s[...], NH_G, False) * ng_ref[0] * _silu(gg)).astype(out_ref.dtype)

    @pl.when(t == pl.num_programs(1) - 1)
    def _():
        s_out[0] = st_s[...]


def _gla_call(l, x, ada, w, wlr, blr, ng, s0, Tt):
    B, T, D = x.shape
    L = min(CHUNK, T)
    wcols = w.shape[2]
    bmap = lambda b, t: (b, 0, 0)
    lmap = lambda b, t: (l, 0, 0)
    sshape = (NH_G, DK_G, DV_G)
    return pl.pallas_call(
        functools.partial(_gla_kernel, Tt=Tt, L=L),
        grid=(B, T // Tt),
        in_specs=[pl.BlockSpec((1, Tt, D), lambda b, t: (b, t, 0)),
                  pl.BlockSpec((1, 6, D), bmap),
                  pl.BlockSpec((1, D, wcols), lmap),
                  pl.BlockSpec((1, LANES, QK_G), lmap),
                  pl.BlockSpec((1, 1, QK_G), lmap),
                  pl.BlockSpec((1, 1, VG), lmap),
                  pl.BlockSpec((1,) + sshape, lambda b, t: (b, 0, 0, 0))],
        out_specs=[pl.BlockSpec((1, Tt, VG), lambda b, t: (b, t, 0)),
                   pl.BlockSpec((1,) + sshape, lambda b, t: (b, 0, 0, 0))],
        out_shape=[jax.ShapeDtypeStruct((B, T, VG), BF16),
                   jax.ShapeDtypeStruct((B,) + sshape, F32)],
        scratch_shapes=[pltpu.VMEM((Tt, D), BF16),
                        pltpu.VMEM((Tt, QK_G), F32),
                        pltpu.VMEM((Tt, QK_G), F32),
                        pltpu.VMEM((Tt, VG), BF16),
                        pltpu.VMEM((Tt, QK_G), F32),
                        pltpu.VMEM((Tt, VG), F32),
                        pltpu.VMEM(sshape, F32)],
        compiler_params=_cparams(2),
        name="gla",
    )(x, ada, w, wlr, blr, ng, s0)


def _layer_norm(x, g, b):
    mu = jnp.mean(x, axis=1, keepdims=True)
    xc = x - mu
    var = jnp.mean(xc * xc, axis=1, keepdims=True)
    return xc * lax.rsqrt(var + LN_EPS) * g + b


def _merge_kernel(x_ref, ada_ref, hm_ref, ys_ref, og_ref, wg_ref, bb_ref, wb_ref, wo_ref, lng_ref, lnb_ref,
                  wrt_ref, brt_ref,
                  x1_ref, u2_ref, eid_ref, wts_ref, *, alpha):
    x = x_ref[0]
    ada = ada_ref[0]
    D = x.shape[1]
    u = _modulate(x, ada, 0, 1).astype(BF16)
    gate = jax.nn.sigmoid(_dot(u, wg_ref[0]) + bb_ref[0])
    merged = (gate[:, 0:D] * _dot(hm_ref[0], wb_ref[0, 0])
              + gate[:, D:2 * D] * _dot(ys_ref[0], wb_ref[0, 1])
              + gate[:, 2 * D:3 * D] * _dot(og_ref[0], wb_ref[0, 2]))
    y = _dot(merged.astype(BF16), wo_ref[0])
    x1 = _layer_norm(alpha * x + ada[2:3] * y, lng_ref[0, 0:1], lnb_ref[0, 0:1])
    x1_ref[0] = x1
    u2 = _modulate(x1, ada, 3, 4).astype(BF16)
    u2_ref[0] = u2

    logits = _dot(u2, wrt_ref[0]) + brt_ref[0]
    lane = lax.broadcasted_iota(jnp.int32, logits.shape, 1)
    lane_f = lane.astype(F32)
    big = float(LANES)
    is_g = (lane >= N_EXPERTS) & (lane < N_EXPERTS + N_GROUPS)
    gmax = jnp.max(jnp.where(is_g, logits, NEG_INF), axis=1, keepdims=True)
    gsum = jnp.sum(jnp.where(is_g, jnp.exp(logits - gmax), 0.0), axis=1, keepdims=True)
    pg_top = 1.0 / gsum
    g_lane = jnp.min(jnp.where(is_g & (logits == gmax), lane_f, big), axis=1, keepdims=True)
    g_idx = g_lane.astype(jnp.int32) - N_EXPERTS
    in_grp = (lane < N_EXPERTS) & ((lane // EXP_PER_GROUP) == g_idx)
    el = jnp.where(in_grp, logits, NEG_INF)
    v1 = jnp.max(el, axis=1, keepdims=True)
    i1 = jnp.min(jnp.where(in_grp & (el == v1), lane_f, big), axis=1, keepdims=True)
    rest = in_grp & (lane_f != i1)
    el2 = jnp.where(rest, logits, NEG_INF)
    v2 = jnp.max(el2, axis=1, keepdims=True)
    i2 = jnp.min(jnp.where(rest & (el2 == v2), lane_f, big), axis=1, keepdims=True)
    e = jnp.exp(v2 - v1)
    w1 = pg_top / (1.0 + e)
    w2 = pg_top * e / (1.0 + e)
    eid_ref[0] = jnp.where(lane == 0, i1, jnp.where(lane == 1, i2, 0.0)).astype(jnp.int32)
    wts_ref[0] = jnp.where(lane == 0, w1, jnp.where(lane == 1, w2, 0.0))


def _merge_call(l, x, ada, hm, ys, og, wg, bb, wb, wo, lng, lnb, wrt, brt, tm, alpha):
    B, T, D = x.shape
    bmap = lambda b, t: (b, 0, 0)
    lmap = lambda b, t: (l, 0, 0)
    tmap = lambda b, t: (b, t, 0)
    return pl.pallas_call(
        functools.partial(_merge_kernel, alpha=alpha),
        grid=(B, T // tm),
        in_specs=[pl.BlockSpec((1, tm, D), tmap),
                  pl.BlockSpec((1, 6, D), bmap),
                  pl.BlockSpec((1, tm, D), tmap),
                  pl.BlockSpec((1, tm, D), tmap),
                  pl.BlockSpec((1, tm, D), tmap),
                  pl.BlockSpec((1, D, 3 * D), lmap),
                  pl.BlockSpec((1, 1, 3 * D), lmap),
                  pl.BlockSpec((1, 3, D, D), lambda b, t: (l, 0, 0, 0)),
                  pl.BlockSpec((1, D, D), lmap),
                  pl.BlockSpec((1, 2, D), lmap),
                  pl.BlockSpec((1, 2, D), lmap),
                  pl.BlockSpec((1, D, LANES), lmap),
                  pl.BlockSpec((1, 1, LANES), lmap)],
        out_specs=[pl.BlockSpec((1, tm, D), tmap),
                   pl.BlockSpec((1, tm, D), tmap),
                   pl.BlockSpec((1, tm, LANES), tmap),
                   pl.BlockSpec((1, tm, LANES), tmap)],
        out_shape=[jax.ShapeDtypeStruct((B, T, D), F32),
                   jax.ShapeDtypeStruct((B, T, D), BF16),
                   jax.ShapeDtypeStruct((B, T, LANES), jnp.int32),
                   jax.ShapeDtypeStruct((B, T, LANES), F32)],
        compiler_params=_cparams(2),
        name="merge",
    )(x, ada, hm, ys, og, wg, bb, wb, wo, lng, lnb, wrt, brt)


def _moe_kernel(te_ref, xs_ref, wg_ref, wu_ref, wd_ref, o_ref):
    del te_ref
    xs = xs_ref[...]
    hg = _dot(xs, wg_ref[0, 0])
    hu = _dot(xs, wu_ref[0, 0])
    o_ref[...] = _dot((_silu(hg) * hu).astype(BF16), wd_ref[0, 0])


def _moe_call(l, tile_expert, xs, weg, weu, wed):
    R, D = xs.shape
    dexp = weg.shape[3]
    tm = MOE_TILE
    return pl.pallas_call(
        _moe_kernel,
        grid_spec=pltpu.PrefetchScalarGridSpec(
            num_scalar_prefetch=1,
            grid=(R // tm,),
            in_specs=[pl.BlockSpec((tm, D), lambda i, te: (i, 0)),
                      pl.BlockSpec((1, 1, D, dexp), lambda i, te: (l, te[i], 0, 0)),
                      pl.BlockSpec((1, 1, D, dexp), lambda i, te: (l, te[i], 0, 0)),
                      pl.BlockSpec((1, 1, dexp, D), lambda i, te: (l, te[i], 0, 0))],
            out_specs=pl.BlockSpec((tm, D), lambda i, te: (i, 0))),
        out_shape=jax.ShapeDtypeStruct((R, D), F32),
        compiler_params=_cparams(1),
        name="moe",
    )(tile_expert, xs, weg, weu, wed)


def _ln2_kernel(x1_ref, ada_ref, y0_ref, y1_ref, wts_ref, lng_ref, lnb_ref, o_ref, *, alpha):
    ada = ada_ref[0]
    wts = wts_ref[0]
    moe = y0_ref[0] * wts[:, 0:1] + y1_ref[0] * wts[:, 1:2]
    o_ref[0] = _layer_norm(alpha * x1_ref[0] + ada[5:6] * moe, lng_ref[0, 1:2], lnb_ref[0, 1:2])


def _ln2_call(l, x1, ada, y0, y1, wts, lng, lnb, tm, alpha):
    B, T, D = x1.shape
    bmap = lambda b, t: (b, 0, 0)
    lmap = lambda b, t: (l, 0, 0)
    tmap = lambda b, t: (b, t, 0)
    return pl.pallas_call(
        functools.partial(_ln2_kernel, alpha=alpha),
        grid=(B, T // tm),
        in_specs=[pl.BlockSpec((1, tm, D), tmap),
                  pl.BlockSpec((1, 6, D), bmap),
                  pl.BlockSpec((1, tm, D), tmap),
                  pl.BlockSpec((1, tm, D), tmap),
                  pl.BlockSpec((1, tm, LANES), tmap),
                  pl.BlockSpec((1, 2, D), lmap),
                  pl.BlockSpec((1, 2, D), lmap)],
        out_specs=pl.BlockSpec((1, tm, D), tmap),
        out_shape=jax.ShapeDtypeStruct((B, T, D), F32),
        compiler_params=_cparams(2),
        name="ln2",
    )(x1, ada, y0, y1, wts, lng, lnb)


def _take_rows(a, idx):
    return a.at[idx].get(mode="promise_in_bounds")


def _lookup(table, idx):
    sel = idx[:, None] == jnp.arange(table.shape[0], dtype=idx.dtype)[None, :]
    return jnp.sum(jnp.where(sel, table[None, :], 0), axis=1)


def _route(eid, n_tok):
    tm = MOE_TILE
    flat = eid.reshape(-1)
    n_asg = flat.shape[0]
    n_rows = (-(-n_asg // tm) + N_EXPERTS) * tm
    order = jnp.argsort(flat).astype(jnp.int32)
    inv = jnp.argsort(order).astype(jnp.int32)
    experts = jnp.arange(N_EXPERTS, dtype=jnp.int32)
    sizes = jnp.sum((flat[:, None] == experts[None, :]).astype(jnp.int32), axis=0)
    off = jnp.cumsum(sizes) - sizes
    psz = ((sizes + tm - 1) // tm) * tm
    pend = jnp.cumsum(psz)
    poff = pend - psz
    dest = _lookup(poff - off, flat) + inv
    tile_start = jnp.arange(n_rows // tm, dtype=jnp.int32) * tm
    tile_expert = jnp.minimum(jnp.sum((pend[None, :] <= tile_start[:, None]).astype(jnp.int32), axis=1),
                              N_EXPERTS - 1)
    within = jnp.arange(n_rows, dtype=jnp.int32) - jnp.repeat(poff[tile_expert], tm)
    valid = within < jnp.repeat(sizes[tile_expert], tm)
    pos = jnp.clip(jnp.repeat(off[tile_expert], tm) + within, 0, n_asg - 1)
    row_tok = jnp.where(valid, _take_rows(order, pos) // TOP_K, 0)
    return row_tok, dest.reshape(n_tok, TOP_K), tile_expert


def _pad_lanes(a, width=LANES):
    return jnp.pad(a, [(0, 0)] * (a.ndim - 1) + [(0, width - a.shape[-1])])


def _time_block(T):
    return min(T, 256)


def _trunk(x, ada_all, states, P, depth, alpha):
    B, T, D = x.shape
    Tt = _time_block(T)
    c_m, n_m, m_m, cv_m, h_s, cv_s, s_g = states
    new = [[] for _ in range(7)]
    for l in range(depth):
        ada = ada_all[l].reshape(B, 6, D)
        c0t = jnp.swapaxes(c_m[l], -1, -2)
        m0 = _pad_lanes(m_m[l])[:, None, :]
        h0t = (h_s[l].reshape(B, G_S, HPG_S, P_S, N_S).transpose(0, 1, 4, 2, 3)
               .reshape(B, G_S, N_S, HPG_S * P_S))
        hm, c_t, n_n, m_n, cvm_n = _mlstm_call(l, x, ada, P["w_mlstm"], P["mlstm_conv_w"], P["mlstm_conv_b"],
                                               P["mlstm_bif"], P["mlstm_norm_g"], c0t, n_m[l], m0, cv_m[l], Tt)
        ys, h_t, cvs_n = _ssd_call(l, x, ada, P["w_ssd"], P["ssd_conv_w"], P["ssd_conv_b"], P["ssd_dtb"],
                                   P["ssd_alog"], P["ssd_dfull"], P["ssd_norm_g"], h0t, cv_s[l], Tt)
        og, s_n = _gla_call(l, x, ada, P["w_gla"], P["gla_w_lr"], P["gla_b_lr"], P["gla_norm_g"], s_g[l], Tt)
        x1, u2, eid, wts = _merge_call(l, x, ada, hm, ys, og, P["w_gate"], P["b_branch"], P["w_branch"],
                                       P["w_out"], P["ln_g"], P["ln_b"], P["w_rt"], P["b_rt"], Tt, alpha)
        row_tok, dest, tile_expert = _route(eid[:, :, :TOP_K], B * T)
        xs = _take_rows(u2.reshape(B * T, D), row_tok)
        ye = _moe_call(l, tile_expert, xs, P["w_e_gate"], P["w_e_up"], P["w_e_down"])
        y0 = _take_rows(ye, dest[:, 0]).reshape(B, T, D)
        y1 = _take_rows(ye, dest[:, 1]).reshape(B, T, D)
        x = _ln2_call(l, x1, ada, y0, y1, wts, P["ln_g"], P["ln_b"], Tt, alpha)

        new[0].append(jnp.swapaxes(c_t, -1, -2))
        new[1].append(n_n)
        new[2].append(m_n[:, 0, :NH_M])
        new[3].append(cvm_n)
        new[4].append(h_t.reshape(B, G_S, N_S, HPG_S, P_S).transpose(0, 1, 3, 4, 2).reshape(B, NH_S, P_S, N_S))
        new[5].append(cvs_n)
        new[6].append(s_n)
    return x, tuple(jnp.stack(lst) for lst in new)


def kernel(x_prompt, x_sample, state_mlstm_c, state_mlstm_n, state_mlstm_m, state_mlstm_conv, state_ssd, state_ssd_conv, state_gla, c_prompt, c_sample, w_ada, b_ada, w_in, mlstm_b_i, mlstm_b_f, mlstm_conv_w, mlstm_conv_b, mlstm_norm_g, ssd_conv_w, ssd_conv_b, ssd_dt_bias, ssd_a_log, ssd_d, ssd_norm_g, gla_w_lr, gla_b_lr, gla_norm_g, b_branch, w_branch, w_out, ln_g, ln_b, w_grp, b_grp, w_router, b_router, w_e_gate, w_e_up, w_e_down):
    depth, D, _ = w_in.shape
    alpha = (2 * depth) ** 0.25
    nbp = x_prompt.shape[0]

    edges = np.concatenate([[0], np.cumsum(COL_SIZES)])
    col = {n: w_in[:, :, int(edges[i]):int(edges[i + 1])] for i, n in enumerate(COL_NAMES)}
    w_gate = w_in[:, :, int(edges[-1]):].astype(BF16)
    cat = lambda parts: jnp.concatenate(parts, axis=-1).astype(BF16)
    row = lambda a: a[:, None, :]
    P = {
        "w_mlstm": cat([col["qk_m"], col["v_m"], col["o_m"], _pad_lanes(jnp.concatenate([col["i_m"], col["f_m"]], -1))]),
        "w_ssd": cat([col["z_s"], col["xbc_s"], _pad_lanes(col["dt_s"])]),
        "w_gla": cat([col["q_g"], col["k_g"], col["v_g"], col["g_g"], _pad_lanes(col["lr_g"])]),
        "w_gate": w_gate,
        "mlstm_conv_w": mlstm_conv_w, "mlstm_conv_b": row(mlstm_conv_b),
        "mlstm_bif": row(_pad_lanes(jnp.concatenate([mlstm_b_i, mlstm_b_f], -1))),
        "mlstm_norm_g": row(mlstm_norm_g),
        "ssd_conv_w": ssd_conv_w, "ssd_conv_b": row(ssd_conv_b),
        "ssd_dtb": row(_pad_lanes(ssd_dt_bias)), "ssd_alog": row(_pad_lanes(ssd_a_log)),
        "ssd_dfull": row(jnp.repeat(ssd_d, P_S, axis=-1)), "ssd_norm_g": row(ssd_norm_g),
        "gla_w_lr": jnp.pad(gla_w_lr, ((0, 0), (0, LANES - R_G), (0, 0))).astype(BF16),
        "gla_b_lr": row(gla_b_lr), "gla_norm_g": row(gla_norm_g),
        "b_branch": row(b_branch), "w_branch": w_branch.astype(BF16), "w_out": w_out.astype(BF16),
        "ln_g": ln_g, "ln_b": ln_b,
        "w_rt": _pad_lanes(jnp.concatenate([w_router, w_grp], -1)).astype(BF16),
        "b_rt": row(_pad_lanes(jnp.concatenate([b_router, b_grp], -1))),
        "w_e_gate": w_e_gate.astype(BF16), "w_e_up": w_e_up.astype(BF16), "w_e_down": w_e_down.astype(BF16),
    }

    ada_all = _ada_call(jnp.concatenate([c_prompt, c_sample], axis=0), w_ada, b_ada)

    sample_states = (state_mlstm_c, state_mlstm_n, state_mlstm_m, state_mlstm_conv,
                     state_ssd, state_ssd_conv, state_gla)
    prompt_states = tuple(jnp.zeros((s.shape[0], nbp) + s.shape[2:], x_prompt.dtype) for s in sample_states)

    y_prompt, new_p = _trunk(x_prompt, ada_all[:, :nbp], prompt_states, P, depth, alpha)
    y_sample, new_s = _trunk(x_sample, ada_all[:, nbp:], sample_states, P, depth, alpha)
    return (y_prompt, y_sample) + new_p + new_s
```

```python
import functools

import jax
import jax.numpy as jnp
import numpy as np
from jax import lax
from jax.experimental import pallas as pl
from jax.experimental.pallas import tpu as pltpu

F32 = jnp.float32
BF16 = jnp.bfloat16
NEG_INF = float("-inf")

CHUNK = 64
CONV_W = 4
NH_M, DQK_M, DV_M = 4, 128, 256
QK_M, VM = NH_M * DQK_M, NH_M * DV_M
NH_S, P_S, N_S, G_S = 16, 64, 128, 4
HPG_S = NH_S // G_S
XS, BC_S = NH_S * P_S, G_S * N_S
CONV_S_DIM = XS + 2 * BC_S
NH_G, DK_G, DV_G = 4, 128, 256
QK_G, VG = NH_G * DK_G, NH_G * DV_G
R_G = 16
TAU_G = 16.0
N_GROUPS, EXP_PER_GROUP, TOP_K = 4, 8, 2
N_EXPERTS = N_GROUPS * EXP_PER_GROUP
LN_EPS = 1e-5
LANES = 128

COL_SIZES = (2 * QK_M, VM, VM, NH_M, NH_M, XS, CONV_S_DIM, NH_S, QK_G, QK_G, VG, VG, R_G)
COL_NAMES = ("qk_m", "v_m", "o_m", "i_m", "f_m", "z_s", "xbc_s", "dt_s", "q_g", "k_g", "v_g", "g_g", "lr_g")

VMEM_LIMIT = 56 * 1024 * 1024
MOE_TILE = 256


def _cparams(n_axes):
    return pltpu.CompilerParams(dimension_semantics=("arbitrary",) * n_axes, vmem_limit_bytes=VMEM_LIMIT)


def _dot(a, b):
    return jnp.dot(a, b, preferred_element_type=F32)


def _dot_nt(a, b):
    return lax.dot_general(a, b, (((1,), (1,)), ((), ())), preferred_element_type=F32)


def _split3(x):
    hi = x.astype(BF16)
    r = x - hi.astype(F32)
    mid = r.astype(BF16)
    lo = (r - mid.astype(F32)).astype(BF16)
    return hi, mid, lo


def _eye(n, m):
    r = lax.broadcasted_iota(jnp.int32, (n, m), 0)
    c = lax.broadcasted_iota(jnp.int32, (n, m), 1)
    return jnp.where(r == c, 1.0, 0.0).astype(BF16)


def _transpose_rows(x, n):
    e = _eye(n, x.shape[1])
    hi, mid, lo = _split3(x)
    return _dot_nt(e, lo) + _dot_nt(e, mid) + _dot_nt(e, hi)


def _block_masks(Tt, L):
    r = lax.broadcasted_iota(jnp.int32, (Tt, Tt), 0)
    c = lax.broadcasted_iota(jnp.int32, (Tt, Tt), 1)
    same = (r // L) == (c // L)
    tril = jnp.where(same & (c <= r), 1.0, 0.0).astype(BF16)
    full = jnp.where(same, 1.0, 0.0).astype(BF16)
    return tril, full


def _transpose_f32(x):
    if x.shape[0] % LANES == 0:
        return x.T
    return _transpose_rows(x, x.shape[1])


def _log_sigmoid(x):
    return jnp.minimum(x, 0.0) - jnp.log1p(jnp.exp(-jnp.abs(x)))


def _softplus(x):
    return jnp.maximum(x, 0.0) + jnp.log1p(jnp.exp(-jnp.abs(x)))


def _silu(x):
    return x * jax.nn.sigmoid(x)


def _modulate(x, ada, shift_row, scale_row):
    return x * (1.0 + ada[scale_row:scale_row + 1]) + ada[shift_row:shift_row + 1]


def _causal_conv(buf, x, w_ref, b_ref, Tt):
    buf[8:8 + Tt, :] = x
    y = buf[5:5 + Tt, :] * w_ref[0, 0:1, :]
    y = y + buf[6:6 + Tt, :] * w_ref[0, 1:2, :]
    y = y + buf[7:7 + Tt, :] * w_ref[0, 2:3, :]
    y = y + x * w_ref[0, 3:4, :]
    y = y + b_ref[0]
    buf[5:8, :] = buf[5 + Tt:8 + Tt, :]
    return y


def _group_norm(x, n_groups, center):
    w = x.shape[1] // n_groups
    outs = []
    for g in range(n_groups):
        xg = x[:, g * w:(g + 1) * w]
        if center:
            xg = xg - jnp.mean(xg, axis=1, keepdims=True)
        outs.append(xg * lax.rsqrt(jnp.mean(xg * xg, axis=1, keepdims=True) + LN_EPS))
    return jnp.concatenate(outs, axis=1)


def _ada_kernel(c_ref, w_ref, b_ref, o_ref):
    c = _silu(c_ref[...]).astype(BF16)
    o_ref[0] = _dot(c, w_ref[0].astype(BF16)) + b_ref[0]


def _ada_call(c_all, w_ada, b_ada):
    depth, d, n6 = w_ada.shape
    nb = c_all.shape[0]
    tn = 1536
    return pl.pallas_call(
        _ada_kernel,
        grid=(depth, n6 // tn),
        in_specs=[pl.BlockSpec((nb, d), lambda l, j: (0, 0)),
                  pl.BlockSpec((1, d, tn), lambda l, j: (l, 0, j)),
                  pl.BlockSpec((1, 1, tn), lambda l, j: (l, 0, j))],
        out_specs=pl.BlockSpec((1, nb, tn), lambda l, j: (l, 0, j)),
        out_shape=jax.ShapeDtypeStruct((depth, nb, n6), F32),
        compiler_params=_cparams(2),
        name="ada",
    )(c_all, w_ada, b_ada.reshape(depth, 1, n6))


def _mlstm_kernel(x_ref, ada_ref, w_ref, cw_ref, cb_ref, bif_ref, ng_ref, c0_ref, n0_ref, m0_ref, cv0_ref,
                  out_ref, c_out, n_out, m_out, cv_out,
                  conv_s, q_s, qb_s, kb_s, v_s, o_s, kwt_s, un_s, nu_s, p_s, rs_s, b_s, ml_s, n_s, m_s, *ct_s, Tt, L):
    t = pl.program_id(1)
    nc = Tt // L

    @pl.when(t == 0)
    def _():
        for h in range(NH_M):
            ct_s[h][...] = c0_ref[0, h]
        n_s[...] = n0_ref[0]
        m_s[...] = m0_ref[0]
        conv_s[5:8, :] = cv0_ref[0]

    u = _modulate(x_ref[0], ada_ref[0], 0, 1).astype(BF16)
    qk = _silu(_causal_conv(conv_s, _dot(u, w_ref[0, :, 0:2 * QK_M]), cw_ref, cb_ref, Tt))
    q = qk[:, :QK_M]
    k = qk[:, QK_M:] * DQK_M ** -0.5
    q_s[...] = q
    qb_s[...] = q.astype(BF16)
    kb_s[...] = k.astype(BF16)
    v_s[...] = _dot(u, w_ref[0, :, 2 * QK_M:2 * QK_M + VM]).astype(BF16)
    o_s[...] = _dot(u, w_ref[0, :, 2 * QK_M + VM:2 * QK_M + 2 * VM])
    g = _dot(u, w_ref[0, :, 2 * QK_M + 2 * VM:]) + bif_ref[0]
    lane = lax.broadcasted_iota(jnp.int32, g.shape, 1)
    g = jnp.where(lane < NH_M, g, _log_sigmoid(g))

    tril, _ = _block_masks(Tt, L)
    hi, mid, lo = _split3(g)
    cs = _dot(tril, lo) + _dot(tril, mid) + _dot(tril, hi)
    b0 = pltpu.roll(cs, LANES - NH_M, axis=1)
    r = g - b0
    row = lax.broadcasted_iota(jnp.int32, g.shape, 0) % L
    a = r
    sh = 1
    while sh < L:
        a = jnp.maximum(a, jnp.where(row >= sh, pltpu.roll(a, sh, axis=0), NEG_INF))
        sh *= 2
    al = a
    sh = 1
    while sh < L:
        al = jnp.maximum(al, jnp.where(row + sh < L, pltpu.roll(al, Tt - sh, axis=0), NEG_INF))
        sh *= 2
    b_s[...] = b0
    ml_s[...] = b0 + a
    wsrc = jnp.exp(r - al)
    r_rows = _transpose_f32(r)[0:8, :]
    col = lax.broadcasted_iota(jnp.int32, (DQK_M, Tt), 1)
    rr = lax.broadcasted_iota(jnp.int32, (L, L), 0)
    cc = lax.broadcasted_iota(jnp.int32, (L, L), 1)
    causal = cc <= rr
    for h in range(NH_M):
        ks = slice(h * DQK_M, (h + 1) * DQK_M)
        kw = wsrc[:, h:h + 1] * k[:, ks]
        kwt_s[h] = _transpose_f32(kw).astype(BF16)
        for c in range(nc):
            nu_s[c * NH_M + h:c * NH_M + h + 1, :] = jnp.sum(kw[c * L:(c + 1) * L], axis=0, keepdims=True)
    for c in range(nc):
        rows = slice(c * L, (c + 1) * L)
        for h in range(NH_M):
            ks = slice(h * DQK_M, (h + 1) * DQK_M)
            vs = slice(h * DV_M, (h + 1) * DV_M)
            kwm = jnp.where((col // L) == c, kwt_s[h], jnp.zeros((), BF16)) if nc > 1 else kwt_s[h]
            un_s[c * NH_M + h] = _dot(kwm, v_s[:, vs])
            dloc = jnp.exp(jnp.where(causal, r_rows[h:h + 1, rows] - a[rows, h:h + 1], NEG_INF))
            sl = _dot_nt(qb_s[rows, ks], kb_s[rows, ks]) * dloc
            p_s[rows, vs] = _dot(sl.astype(BF16), v_s[rows, vs])
            rs_s[rows, h:h + 1] = jnp.sum(sl, axis=1, keepdims=True)

    lane_l = lax.broadcasted_iota(jnp.int32, (L, LANES), 1)
    for c in range(nc):
        rows = slice(c * L, (c + 1) * L)
        m_prev = m_s[...]
        bc = b_s[rows, :]
        mlc = ml_s[rows, :]
        mt = jnp.maximum(bc + m_prev, mlc)
        corr = jnp.exp(mlc - mt)
        w_inter = jnp.exp(bc + m_prev - mt)
        inter = []
        qn = jnp.zeros((L, LANES), F32)
        for h in range(NH_M):
            ks = slice(h * DQK_M, (h + 1) * DQK_M)
            inter.append(_dot(qb_s[rows, ks], ct_s[h][...].astype(BF16)))
            qn_h = jnp.sum(q_s[rows, ks] * n_s[h:h + 1, :], axis=1, keepdims=True)
            qn = jnp.where(lane_l == h, qn_h, qn)
        den = corr * rs_s[rows, :] + w_inter * qn
        inv = 1.0 / jnp.maximum(jnp.abs(den), jnp.exp(-mt))
        ca = corr * inv
        cb2 = w_inter * inv
        for h in range(NH_M):
            vs = slice(h * DV_M, (h + 1) * DV_M)
            p_s[rows, vs] = ca[:, h:h + 1] * p_s[rows, vs] + cb2[:, h:h + 1] * inter[h]
        m_last = mt[L - 1:L, :]
        c_l = jnp.exp(mlc[L - 1:L, :] - m_last)
        decay = jnp.exp(bc[L - 1:L, :] + m_prev - m_last)
        for h in range(NH_M):
            ct_s[h][...] = decay[:, h:h + 1] * ct_s[h][...] + c_l[:, h:h + 1] * un_s[c * NH_M + h]
            n_s[h:h + 1, :] = decay[:, h:h + 1] * n_s[h:h + 1, :] + c_l[:, h:h + 1] * nu_s[c * NH_M + h:c * NH_M + h + 1, :]
        m_s[...] = m_last

    hn = _group_norm(p_s[...], NH_M, True) * ng_ref[0] * jax.nn.sigmoid(o_s[...])
    out_ref[0] = hn.astype(out_ref.dtype)

    @pl.when(t == pl.num_programs(1) - 1)
    def _():
        for h in range(NH_M):
            c_out[0, h] = ct_s[h][...]
        n_out[0] = n_s[...]
        m_out[0] = m_s[...]
        cv_out[0] = conv_s[5:8, :]


def _mlstm_call(l, x, ada, w, cw, cb, bif, ng, c0t, n0, m0, cv0, Tt):
    B, T, D = x.shape
    L = min(CHUNK, T)
    wcols = w.shape[2]
    bmap = lambda b, t: (b, 0, 0)
    lmap = lambda b, t: (l, 0, 0)
    return pl.pallas_call(
        functools.partial(_mlstm_kernel, Tt=Tt, L=L),
        grid=(B, T // Tt),
        in_specs=[pl.BlockSpec((1, Tt, D), lambda b, t: (b, t, 0)),
                  pl.BlockSpec((1, 6, D), bmap),
                  pl.BlockSpec((1, D, wcols), lmap),
                  pl.BlockSpec((1, CONV_W, 2 * QK_M), lmap),
                  pl.BlockSpec((1, 1, 2 * QK_M), lmap),
                  pl.BlockSpec((1, 1, LANES), lmap),
                  pl.BlockSpec((1, 1, VM), lmap),
                  pl.BlockSpec((1, NH_M, DQK_M, DV_M), lambda b, t: (b, 0, 0, 0)),
                  pl.BlockSpec((1, NH_M, DQK_M), bmap),
                  pl.BlockSpec((1, 1, LANES), bmap),
                  pl.BlockSpec((1, CONV_W - 1, 2 * QK_M), bmap)],
        out_specs=[pl.BlockSpec((1, Tt, VM), lambda b, t: (b, t, 0)),
                   pl.BlockSpec((1, NH_M, DQK_M, DV_M), lambda b, t: (b, 0, 0, 0)),
                   pl.BlockSpec((1, NH_M, DQK_M), bmap),
                   pl.BlockSpec((1, 1, LANES), bmap),
                   pl.BlockSpec((1, CONV_W - 1, 2 * QK_M), bmap)],
        out_shape=[jax.ShapeDtypeStruct((B, T, VM), BF16),
                   jax.ShapeDtypeStruct((B, NH_M, DQK_M, DV_M), F32),
                   jax.ShapeDtypeStruct((B, NH_M, DQK_M), F32),
                   jax.ShapeDtypeStruct((B, 1, LANES), F32),
                   jax.ShapeDtypeStruct((B, CONV_W - 1, 2 * QK_M), F32)],
        scratch_shapes=[pltpu.VMEM((8 + Tt, 2 * QK_M), F32),
                        pltpu.VMEM((Tt, QK_M), F32),
                        pltpu.VMEM((Tt, QK_M), BF16),
                        pltpu.VMEM((Tt, QK_M), BF16),
                        pltpu.VMEM((Tt, VM), BF16),
                        pltpu.VMEM((Tt, VM), F32),
                        pltpu.VMEM((NH_M, DQK_M, Tt), BF16),
                        pltpu.VMEM((Tt // L * NH_M, DQK_M, DV_M), F32),
                        pltpu.VMEM((Tt // L * NH_M, DQK_M), F32),
                        pltpu.VMEM((Tt, VM), F32),
                        pltpu.VMEM((Tt, LANES), F32),
                        pltpu.VMEM((Tt, LANES), F32),
                        pltpu.VMEM((Tt, LANES), F32),
                        pltpu.VMEM((NH_M, DQK_M), F32),
                        pltpu.VMEM((1, LANES), F32)] + [pltpu.VMEM((DQK_M, DV_M), F32)] * NH_M,
        compiler_params=_cparams(2),
        name="mlstm",
    )(x, ada, w, cw, cb, bif, ng, c0t, n0, m0, cv0)


def _pair_cols(x, hd, lane):
    return jnp.where(lane < P_S, x[:, hd:hd + 1], x[:, hd + 1:hd + 2])


def _ssd_kernel(x_ref, ada_ref, w_ref, cw_ref, cb_ref, dtb_ref, alog_ref, d_ref, ng_ref, h0_ref, cv0_ref,
                out_ref, h_out, cv_out,
                conv_s, u_s, x_s, xw_s, b_s, c_s, bt_s, e2_s, un_s, dec_s, y_s, *ht_s, Tt, L):
    t = pl.program_id(1)
    nc = Tt // L
    n_pairs = NH_S // 2

    @pl.when(t == 0)
    def _():
        for g in range(G_S):
            ht_s[g][...] = h0_ref[0, g]
        conv_s[5:8, :] = cv0_ref[0]

    u = _modulate(x_ref[0], ada_ref[0], 0, 1).astype(BF16)
    u_s[...] = u
    xbc = _silu(_causal_conv(conv_s, _dot(u, w_ref[0, :, XS:XS + CONV_S_DIM]), cw_ref, cb_ref, Tt))
    x = xbc[:, :XS]
    bm = xbc[:, XS:XS + BC_S]
    x_s[...] = x
    b_s[...] = bm.astype(BF16)
    c_s[...] = xbc[:, XS + BC_S:].astype(BF16)
    dt = _softplus(_dot(u, w_ref[0, :, XS + CONV_S_DIM:]) + dtb_ref[0])
    a_row = -jnp.exp(alog_ref[0])

    tril, full = _block_masks(Tt, L)
    hi, mid, lo = _split3(dt * a_row)
    cs = _dot(tril, lo) + _dot(tril, mid) + _dot(tril, hi)
    csl = _dot(full, lo) + _dot(full, mid) + _dot(full, hi)
    wsrc = jnp.exp(csl - cs) * dt
    dec = jnp.exp(csl)
    cs_rows = _transpose_f32(cs)[0:NH_S, :]
    dt_rows = _transpose_f32(dt)[0:NH_S, :]
    for g in range(G_S):
        bt_s[g] = _transpose_f32(bm[:, g * N_S:(g + 1) * N_S]).astype(BF16)
    lane_t = lax.broadcasted_iota(jnp.int32, (Tt, LANES), 1)
    for pp in range(n_pairs):
        pc = slice(pp * LANES, (pp + 1) * LANES)
        e2_s[:, pc] = _pair_cols(cs, 2 * pp, lane_t)
        xw_s[:, pc] = (x[:, pc] * _pair_cols(wsrc, 2 * pp, lane_t)).astype(BF16)
    lane1 = lax.broadcasted_iota(jnp.int32, (1, LANES), 1)
    for c in range(nc):
        for pp in range(n_pairs):
            r0 = c * L
            dec_s[c * n_pairs + pp:c * n_pairs + pp + 1, :] = _pair_cols(dec[r0:r0 + 1, :], 2 * pp, lane1)

    rr = lax.broadcasted_iota(jnp.int32, (L, LANES), 0)
    lane_l = lax.broadcasted_iota(jnp.int32, (L, LANES), 1)
    causal2 = (lane_l % P_S) <= rr
    col = lax.broadcasted_iota(jnp.int32, (N_S, Tt), 1)
    zero_b = jnp.zeros((), BF16)
    for c in range(nc):
        rows = slice(c * L, (c + 1) * L)
        for g in range(G_S):
            gs = slice(g * N_S, (g + 1) * N_S)
            bg = b_s[rows, gs]
            cb2 = _dot_nt(c_s[rows, gs], jnp.concatenate([bg, bg], axis=0))
            btm = jnp.where((col // L) == c, bt_s[g], zero_b) if nc > 1 else bt_s[g]
            un_s[c * G_S + g] = _dot(btm, xw_s[:, g * 2 * LANES:(g + 1) * 2 * LANES])
            for p2 in range(2):
                pp = g * 2 + p2
                hd = 2 * pp
                pc = slice(pp * LANES, (pp + 1) * LANES)
                csr2 = jnp.concatenate([cs_rows[hd:hd + 1, rows], cs_rows[hd + 1:hd + 2, rows]], axis=1)
                dtr2 = jnp.concatenate([dt_rows[hd:hd + 1, rows], dt_rows[hd + 1:hd + 2, rows]], axis=1)
                seg2 = jnp.exp(jnp.where(causal2, e2_s[rows, pc] - csr2, NEG_INF))
                m2 = (cb2 * seg2 * dtr2).astype(BF16)
                xp = x_s[rows, pc].astype(BF16)
                xbd = jnp.concatenate([jnp.where(lane_l < P_S, xp, zero_b), jnp.where(lane_l >= P_S, xp, zero_b)], axis=0)
                y_s[rows, pc] = _dot(m2, xbd)

    for c in range(nc):
        rows = slice(c * L, (c + 1) * L)
        for g in range(G_S):
            gc = slice(g * 2 * LANES, (g + 1) * 2 * LANES)
            yi = _dot(c_s[rows, g * N_S:(g + 1) * N_S], ht_s[g][...].astype(BF16))
            y_s[rows, gc] += jnp.exp(e2_s[rows, gc]) * yi
        for g in range(G_S):
            for p2 in range(2):
                pp = g * 2 + p2
                hc = slice(p2 * LANES, (p2 + 1) * LANES)
                ht_s[g][:, hc] = (dec_s[c * n_pairs + pp:c * n_pairs + pp + 1, :] * ht_s[g][:, hc]
                                  + un_s[c * G_S + g][:, hc])

    y = y_s[...] + d_ref[0] * x_s[...]
    y = y * _silu(_dot(u_s[...], w_ref[0, :, 0:XS]))
    out_ref[0] = (_group_norm(y, G_S, False) * ng_ref[0]).astype(out_ref.dtype)

    @pl.when(t == pl.num_programs(1) - 1)
    def _():
        for g in range(G_S):
            h_out[0, g] = ht_s[g][...]
        cv_out[0] = conv_s[5:8, :]


def _ssd_call(l, x, ada, w, cw, cb, dtb, alog, dfull, ng, h0t, cv0, Tt):
    B, T, D = x.shape
    L = min(CHUNK, T)
    wcols = w.shape[2]
    bmap = lambda b, t: (b, 0, 0)
    lmap = lambda b, t: (l, 0, 0)
    hshape = (G_S, N_S, HPG_S * P_S)
    assert L == P_S and 2 * L == LANES and T % Tt == 0 and Tt % L == 0
    return pl.pallas_call(
        functools.partial(_ssd_kernel, Tt=Tt, L=L),
        grid=(B, T // Tt),
        in_specs=[pl.BlockSpec((1, Tt, D), lambda b, t: (b, t, 0)),
                  pl.BlockSpec((1, 6, D), bmap),
                  pl.BlockSpec((1, D, wcols), lmap),
                  pl.BlockSpec((1, CONV_W, CONV_S_DIM), lmap),
                  pl.BlockSpec((1, 1, CONV_S_DIM), lmap),
                  pl.BlockSpec((1, 1, LANES), lmap),
                  pl.BlockSpec((1, 1, LANES), lmap),
                  pl.BlockSpec((1, 1, XS), lmap),
                  pl.BlockSpec((1, 1, XS), lmap),
                  pl.BlockSpec((1,) + hshape, lambda b, t: (b, 0, 0, 0)),
                  pl.BlockSpec((1, CONV_W - 1, CONV_S_DIM), bmap)],
        out_specs=[pl.BlockSpec((1, Tt, XS), lambda b, t: (b, t, 0)),
                   pl.BlockSpec((1,) + hshape, lambda b, t: (b, 0, 0, 0)),
                   pl.BlockSpec((1, CONV_W - 1, CONV_S_DIM), bmap)],
        out_shape=[jax.ShapeDtypeStruct((B, T, XS), BF16),
                   jax.ShapeDtypeStruct((B,) + hshape, F32),
                   jax.ShapeDtypeStruct((B, CONV_W - 1, CONV_S_DIM), F32)],
        scratch_shapes=[pltpu.VMEM((8 + Tt, CONV_S_DIM), F32),
                        pltpu.VMEM((Tt, D), BF16),
                        pltpu.VMEM((Tt, XS), F32),
                        pltpu.VMEM((Tt, XS), BF16),
                        pltpu.VMEM((Tt, BC_S), BF16),
                        pltpu.VMEM((Tt, BC_S), BF16),
                        pltpu.VMEM((G_S, N_S, Tt), BF16),
                        pltpu.VMEM((Tt, XS), F32),
                        pltpu.VMEM((Tt // L * G_S, N_S, HPG_S * P_S), F32),
                        pltpu.VMEM((Tt // L * (NH_S // 2), LANES), F32),
                        pltpu.VMEM((Tt, XS), F32)] + [pltpu.VMEM(hshape[1:], F32)] * G_S,
        compiler_params=_cparams(2),
        name="ssd",
    )(x, ada, w, cw, cb, dtb, alog, dfull, ng, h0t, cv0)


def _gla_kernel(x_ref, ada_ref, w_ref, wlr_ref, blr_ref, ng_ref, s0_ref,
                out_ref, s_out,
                u_s, qg_s, kg_s, v_s, kdt_s, egt_s, un_s, o_s, *st_s, Tt, L):
    t = pl.program_id(1)
    nc = Tt // L

    @pl.when(t == 0)
    def _():
        for h in range(NH_G):
            st_s[h][...] = s0_ref[0, h]

    u = _modulate(x_ref[0], ada_ref[0], 0, 1).astype(BF16)
    u_s[...] = u
    q = _dot(u, w_ref[0, :, 0:QK_G])
    k = _dot(u, w_ref[0, :, QK_G:2 * QK_G]) * DK_G ** -0.5
    v_s[...] = _dot(u, w_ref[0, :, 2 * QK_G:2 * QK_G + VG]).astype(BF16)
    lr = _dot(u, w_ref[0, :, 2 * QK_G + 2 * VG:]).astype(BF16)
    lg = _log_sigmoid(_dot(lr, wlr_ref[0]) + blr_ref[0]) / TAU_G

    tril, full = _block_masks(Tt, L)
    hi, mid, lo = _split3(lg)
    G = _dot(tril, lo) + _dot(tril, mid) + _dot(tril, hi)
    Gl = _dot(full, lo) + _dot(full, mid) + _dot(full, hi)
    qg_s[...] = (q * jnp.exp(G)).astype(BF16)
    kg_s[...] = (k * jnp.exp(-G)).astype(BF16)
    kd = k * jnp.exp(Gl - G)
    eg = jnp.exp(Gl)
    for h in range(NH_G):
        ks = slice(h * DK_G, (h + 1) * DK_G)
        kdt_s[h] = _transpose_f32(kd[:, ks]).astype(BF16)
        egt_s[h] = _transpose_f32(eg[:, ks])
    col = lax.broadcasted_iota(jnp.int32, (DK_G, Tt), 1)
    for c in range(nc):
        for h in range(NH_G):
            vs = slice(h * DV_G, (h + 1) * DV_G)
            kdm = jnp.where((col // L) == c, kdt_s[h], jnp.zeros((), BF16)) if nc > 1 else kdt_s[h]
            un_s[c * NH_G + h] = _dot(kdm, v_s[:, vs])

    rr = lax.broadcasted_iota(jnp.int32, (L, L), 0)
    cc = lax.broadcasted_iota(jnp.int32, (L, L), 1)
    causal = cc <= rr
    for c in range(nc):
        rows = slice(c * L, (c + 1) * L)
        att = []
        for h in range(NH_G):
            ks = slice(h * DK_G, (h + 1) * DK_G)
            att.append(_dot_nt(qg_s[rows, ks], kg_s[rows, ks]))
        for h in range(NH_G):
            ks = slice(h * DK_G, (h + 1) * DK_G)
            vs = slice(h * DV_G, (h + 1) * DV_G)
            o_s[rows, vs] = _dot(qg_s[rows, ks], st_s[h][...].astype(BF16))
        for h in range(NH_G):
            vs = slice(h * DV_G, (h + 1) * DV_G)
            o_s[rows, vs] += _dot(jnp.where(causal, att[h], 0.0).astype(BF16), v_s[rows, vs])
        for h in range(NH_G):
            st_s[h][...] = egt_s[h][:, c * L:c * L + 1] * st_s[h][...] + un_s[c * NH_G + h]

    gg = _dot(u_s[...], w_ref[0, :, 2 * QK_G + VG:2 * QK_G + 2 * VG])
    out_ref[0] = (_group_norm(o_s[...], NH_G, False) * ng_ref[0] * _silu(gg)).astype(out_ref.dtype)

    @pl.when(t == pl.num_programs(1) - 1)
    def _():
        for h in range(NH_G):
            s_out[0, h] = st_s[h][...]


def _gla_call(l, x, ada, w, wlr, blr, ng, s0, Tt):
    B, T, D = x.shape
    L = min(CHUNK, T)
    wcols = w.shape[2]
    bmap = lambda b, t: (b, 0, 0)
    lmap = lambda b, t: (l, 0, 0)
    sshape = (NH_G, DK_G, DV_G)
    return pl.pallas_call(
        functools.partial(_gla_kernel, Tt=Tt, L=L),
        grid=(B, T // Tt),
        in_specs=[pl.BlockSpec((1, Tt, D), lambda b, t: (b, t, 0)),
                  pl.BlockSpec((1, 6, D), bmap),
                  pl.BlockSpec((1, D, wcols), lmap),
                  pl.BlockSpec((1, LANES, QK_G), lmap),
                  pl.BlockSpec((1, 1, QK_G), lmap),
                  pl.BlockSpec((1, 1, VG), lmap),
                  pl.BlockSpec((1,) + sshape, lambda b, t: (b, 0, 0, 0))],
        out_specs=[pl.BlockSpec((1, Tt, VG), lambda b, t: (b, t, 0)),
                   pl.BlockSpec((1,) + sshape, lambda b, t: (b, 0, 0, 0))],
        out_shape=[jax.ShapeDtypeStruct((B, T, VG), BF16),
                   jax.ShapeDtypeStruct((B,) + sshape, F32)],
        scratch_shapes=[pltpu.VMEM((Tt, D), BF16),
                        pltpu.VMEM((Tt, QK_G), BF16),
                        pltpu.VMEM((Tt, QK_G), BF16),
                        pltpu.VMEM((Tt, VG), BF16),
                        pltpu.VMEM((NH_G, DK_G, Tt), BF16),
                        pltpu.VMEM((NH_G, DK_G, Tt), F32),
                        pltpu.VMEM((Tt // L * NH_G, DK_G, DV_G), F32),
                        pltpu.VMEM((Tt, VG), F32)] + [pltpu.VMEM(sshape[1:], F32)] * NH_G,
        compiler_params=_cparams(2),
        name="gla",
    )(x, ada, w, wlr, blr, ng, s0)


def _layer_norm(x, g, b):
    mu = jnp.mean(x, axis=1, keepdims=True)
    xc = x - mu
    var = jnp.mean(xc * xc, axis=1, keepdims=True)
    return xc * lax.rsqrt(var + LN_EPS) * g + b


def _merge_kernel(x_ref, ada_ref, hm_ref, ys_ref, og_ref, wg_ref, bb_ref, wb_ref, wo_ref, lng_ref, lnb_ref,
                  wrt_ref, brt_ref,
                  x1_ref, u2_ref, eid_ref, wts_ref, *, alpha):
    x = x_ref[0]
    ada = ada_ref[0]
    D = x.shape[1]
    u = _modulate(x, ada, 0, 1).astype(BF16)
    gate = jax.nn.sigmoid(_dot(u, wg_ref[0]) + bb_ref[0])
    merged = (gate[:, 0:D] * _dot(hm_ref[0], wb_ref[0, 0])
              + gate[:, D:2 * D] * _dot(ys_ref[0], wb_ref[0, 1])
              + gate[:, 2 * D:3 * D] * _dot(og_ref[0], wb_ref[0, 2]))
    y = _dot(merged.astype(BF16), wo_ref[0])
    x1 = _layer_norm(alpha * x + ada[2:3] * y, lng_ref[0, 0:1], lnb_ref[0, 0:1])
    x1_ref[0] = x1
    u2 = _modulate(x1, ada, 3, 4).astype(BF16)
    u2_ref[0] = u2

    logits = _dot(u2, wrt_ref[0]) + brt_ref[0]
    lane = lax.broadcasted_iota(jnp.int32, logits.shape, 1)
    lane_f = lane.astype(F32)
    big = float(LANES)
    is_g = (lane >= N_EXPERTS) & (lane < N_EXPERTS + N_GROUPS)
    gmax = jnp.max(jnp.where(is_g, logits, NEG_INF), axis=1, keepdims=True)
    gsum = jnp.sum(jnp.where(is_g, jnp.exp(logits - gmax), 0.0), axis=1, keepdims=True)
    pg_top = 1.0 / gsum
    g_lane = jnp.min(jnp.where(is_g & (logits == gmax), lane_f, big), axis=1, keepdims=True)
    g_idx = g_lane.astype(jnp.int32) - N_EXPERTS
    in_grp = (lane < N_EXPERTS) & ((lane // EXP_PER_GROUP) == g_idx)
    el = jnp.where(in_grp, logits, NEG_INF)
    v1 = jnp.max(el, axis=1, keepdims=True)
    i1 = jnp.min(jnp.where(in_grp & (el == v1), lane_f, big), axis=1, keepdims=True)
    rest = in_grp & (lane_f != i1)
    el2 = jnp.where(rest, logits, NEG_INF)
    v2 = jnp.max(el2, axis=1, keepdims=True)
    i2 = jnp.min(jnp.where(rest & (el2 == v2), lane_f, big), axis=1, keepdims=True)
    e = jnp.exp(v2 - v1)
    w1 = pg_top / (1.0 + e)
    w2 = pg_top * e / (1.0 + e)
    eid_ref[0] = jnp.where(lane == 0, i1, jnp.where(lane == 1, i2, 0.0)).astype(jnp.int32)
    wts_ref[0] = jnp.where(lane == 0, w1, jnp.where(lane == 1, w2, 0.0))


def _merge_call(l, x, ada, hm, ys, og, wg, bb, wb, wo, lng, lnb, wrt, brt, tm, alpha):
    B, T, D = x.shape
    bmap = lambda b, t: (b, 0, 0)
    lmap = lambda b, t: (l, 0, 0)
    tmap = lambda b, t: (b, t, 0)
    return pl.pallas_call(
        functools.partial(_merge_kernel, alpha=alpha),
        grid=(B, T // tm),
        in_specs=[pl.BlockSpec((1, tm, D), tmap),
                  pl.BlockSpec((1, 6, D), bmap),
                  pl.BlockSpec((1, tm, D), tmap),
                  pl.BlockSpec((1, tm, D), tmap),
                  pl.BlockSpec((1, tm, D), tmap),
                  pl.BlockSpec((1, D, 3 * D), lmap),
                  pl.BlockSpec((1, 1, 3 * D), lmap),
                  pl.BlockSpec((1, 3, D, D), lambda b, t: (l, 0, 0, 0)),
                  pl.BlockSpec((1, D, D), lmap),
                  pl.BlockSpec((1, 2, D), lmap),
                  pl.BlockSpec((1, 2, D), lmap),
                  pl.BlockSpec((1, D, LANES), lmap),
                  pl.BlockSpec((1, 1, LANES), lmap)],
        out_specs=[pl.BlockSpec((1, tm, D), tmap),
                   pl.BlockSpec((1, tm, D), tmap),
                   pl.BlockSpec((1, tm, LANES), tmap),
                   pl.BlockSpec((1, tm, LANES), tmap)],
        out_shape=[jax.ShapeDtypeStruct((B, T, D), F32),
                   jax.ShapeDtypeStruct((B, T, D), BF16),
                   jax.ShapeDtypeStruct((B, T, LANES), jnp.int32),
                   jax.ShapeDtypeStruct((B, T, LANES), F32)],
        compiler_params=_cparams(2),
        name="merge",
    )(x, ada, hm, ys, og, wg, bb, wb, wo, lng, lnb, wrt, brt)


def _moe_kernel(te_ref, xs_ref, wg_ref, wu_ref, wd_ref, o_ref):
    del te_ref
    xs = xs_ref[...]
    hg = _dot(xs, wg_ref[0, 0])
    hu = _dot(xs, wu_ref[0, 0])
    o_ref[...] = _dot((_silu(hg) * hu).astype(BF16), wd_ref[0, 0])


def _moe_call(l, tile_expert, xs, weg, weu, wed):
    R, D = xs.shape
    dexp = weg.shape[3]
    tm = MOE_TILE
    return pl.pallas_call(
        _moe_kernel,
        grid_spec=pltpu.PrefetchScalarGridSpec(
            num_scalar_prefetch=1,
            grid=(R // tm,),
            in_specs=[pl.BlockSpec((tm, D), lambda i, te: (i, 0)),
                      pl.BlockSpec((1, 1, D, dexp), lambda i, te: (l, te[i], 0, 0)),
                      pl.BlockSpec((1, 1, D, dexp), lambda i, te: (l, te[i], 0, 0)),
                      pl.BlockSpec((1, 1, dexp, D), lambda i, te: (l, te[i], 0, 0))],
            out_specs=pl.BlockSpec((tm, D), lambda i, te: (i, 0))),
        out_shape=jax.ShapeDtypeStruct((R, D), F32),
        compiler_params=_cparams(1),
        name="moe",
    )(tile_expert, xs, weg, weu, wed)


def _ln2_kernel(x1_ref, ada_ref, y0_ref, y1_ref, wts_ref, lng_ref, lnb_ref, o_ref, *, alpha):
    ada = ada_ref[0]
    wts = wts_ref[0]
    moe = y0_ref[0] * wts[:, 0:1] + y1_ref[0] * wts[:, 1:2]
    o_ref[0] = _layer_norm(alpha * x1_ref[0] + ada[5:6] * moe, lng_ref[0, 1:2], lnb_ref[0, 1:2])


def _ln2_call(l, x1, ada, y0, y1, wts, lng, lnb, tm, alpha):
    B, T, D = x1.shape
    bmap = lambda b, t: (b, 0, 0)
    lmap = lambda b, t: (l, 0, 0)
    tmap = lambda b, t: (b, t, 0)
    return pl.pallas_call(
        functools.partial(_ln2_kernel, alpha=alpha),
        grid=(B, T // tm),
        in_specs=[pl.BlockSpec((1, tm, D), tmap),
                  pl.BlockSpec((1, 6, D), bmap),
                  pl.BlockSpec((1, tm, D), tmap),
                  pl.BlockSpec((1, tm, D), tmap),
                  pl.BlockSpec((1, tm, LANES), tmap),
                  pl.BlockSpec((1, 2, D), lmap),
                  pl.BlockSpec((1, 2, D), lmap)],
        out_specs=pl.BlockSpec((1, tm, D), tmap),
        out_shape=jax.ShapeDtypeStruct((B, T, D), F32),
        compiler_params=_cparams(2),
        name="ln2",
    )(x1, ada, y0, y1, wts, lng, lnb)


def _take_rows(a, idx):
    return a.at[idx].get(mode="promise_in_bounds")


def _lookup(table, idx):
    sel = idx[:, None] == jnp.arange(table.shape[0], dtype=idx.dtype)[None, :]
    return jnp.sum(jnp.where(sel, table[None, :], 0), axis=1)


def _route(eid, n_tok):
    tm = MOE_TILE
    flat = eid.reshape(-1)
    n_asg = flat.shape[0]
    n_rows = (-(-n_asg // tm) + N_EXPERTS) * tm
    order = jnp.argsort(flat).astype(jnp.int32)
    inv = jnp.argsort(order).astype(jnp.int32)
    experts = jnp.arange(N_EXPERTS, dtype=jnp.int32)
    sizes = jnp.sum((flat[:, None] == experts[None, :]).astype(jnp.int32), axis=0)
    off = jnp.cumsum(sizes) - sizes
    psz = ((sizes + tm - 1) // tm) * tm
    pend = jnp.cumsum(psz)
    poff = pend - psz
    dest = _lookup(poff - off, flat) + inv
    tile_start = jnp.arange(n_rows // tm, dtype=jnp.int32) * tm
    tile_expert = jnp.minimum(jnp.sum((pend[None, :] <= tile_start[:, None]).astype(jnp.int32), axis=1),
                              N_EXPERTS - 1)
    within = jnp.arange(n_rows, dtype=jnp.int32) - jnp.repeat(poff[tile_expert], tm)
    valid = within < jnp.repeat(sizes[tile_expert], tm)
    pos = jnp.clip(jnp.repeat(off[tile_expert], tm) + within, 0, n_asg - 1)
    row_tok = jnp.where(valid, _take_rows(order, pos) // TOP_K, 0)
    return row_tok, dest.reshape(n_tok, TOP_K), tile_expert


def _pad_lanes(a, width=LANES):
    return jnp.pad(a, [(0, 0)] * (a.ndim - 1) + [(0, width - a.shape[-1])])


def _time_block(T):
    return min(T, 256)


def _trunk(x, ada_all, states, P, depth, alpha):
    B, T, D = x.shape
    Tt = _time_block(T)
    c_m, n_m, m_m, cv_m, h_s, cv_s, s_g = states
    new = [[] for _ in range(7)]
    for l in range(depth):
        ada = ada_all[l].reshape(B, 6, D)
        c0t = jnp.swapaxes(c_m[l], -1, -2)
        m0 = _pad_lanes(m_m[l])[:, None, :]
        h0t = (h_s[l].reshape(B, G_S, HPG_S, P_S, N_S).transpose(0, 1, 4, 2, 3)
               .reshape(B, G_S, N_S, HPG_S * P_S))
        hm, c_t, n_n, m_n, cvm_n = _mlstm_call(l, x, ada, P["w_mlstm"], P["mlstm_conv_w"], P["mlstm_conv_b"],
                                               P["mlstm_bif"], P["mlstm_norm_g"], c0t, n_m[l], m0, cv_m[l], Tt)
        ys, h_t, cvs_n = _ssd_call(l, x, ada, P["w_ssd"], P["ssd_conv_w"], P["ssd_conv_b"], P["ssd_dtb"],
                                   P["ssd_alog"], P["ssd_dfull"], P["ssd_norm_g"], h0t, cv_s[l], Tt)
        og, s_n = _gla_call(l, x, ada, P["w_gla"], P["gla_w_lr"], P["gla_b_lr"], P["gla_norm_g"], s_g[l], Tt)
        x1, u2, eid, wts = _merge_call(l, x, ada, hm, ys, og, P["w_gate"], P["b_branch"], P["w_branch"],
                                       P["w_out"], P["ln_g"], P["ln_b"], P["w_rt"], P["b_rt"], Tt, alpha)
        row_tok, dest, tile_expert = _route(eid[:, :, :TOP_K], B * T)
        xs = _take_rows(u2.reshape(B * T, D), row_tok)
        ye = _moe_call(l, tile_expert, xs, P["w_e_gate"], P["w_e_up"], P["w_e_down"])
        y0 = _take_rows(ye, dest[:, 0]).reshape(B, T, D)
        y1 = _take_rows(ye, dest[:, 1]).reshape(B, T, D)
        x = _ln2_call(l, x1, ada, y0, y1, wts, P["ln_g"], P["ln_b"], Tt, alpha)

        new[0].append(jnp.swapaxes(c_t, -1, -2))
        new[1].append(n_n)
        new[2].append(m_n[:, 0, :NH_M])
        new[3].append(cvm_n)
        new[4].append(h_t.reshape(B, G_S, N_S, HPG_S, P_S).transpose(0, 1, 3, 4, 2).reshape(B, NH_S, P_S, N_S))
        new[5].append(cvs_n)
        new[6].append(s_n)
    return x, tuple(jnp.stack(lst) for lst in new)


def kernel(x_prompt, x_sample, state_mlstm_c, state_mlstm_n, state_mlstm_m, state_mlstm_conv, state_ssd, state_ssd_conv, state_gla, c_prompt, c_sample, w_ada, b_ada, w_in, mlstm_b_i, mlstm_b_f, mlstm_conv_w, mlstm_conv_b, mlstm_norm_g, ssd_conv_w, ssd_conv_b, ssd_dt_bias, ssd_a_log, ssd_d, ssd_norm_g, gla_w_lr, gla_b_lr, gla_norm_g, b_branch, w_branch, w_out, ln_g, ln_b, w_grp, b_grp, w_router, b_router, w_e_gate, w_e_up, w_e_down):
    depth, D, _ = w_in.shape
    alpha = (2 * depth) ** 0.25
    nbp = x_prompt.shape[0]

    edges = np.concatenate([[0], np.cumsum(COL_SIZES)])
    col = {n: w_in[:, :, int(edges[i]):int(edges[i + 1])] for i, n in enumerate(COL_NAMES)}
    w_gate = w_in[:, :, int(edges[-1]):].astype(BF16)
    cat = lambda parts: jnp.concatenate(parts, axis=-1).astype(BF16)
    row = lambda a: a[:, None, :]
    P = {
        "w_mlstm": cat([col["qk_m"], col["v_m"], col["o_m"], _pad_lanes(jnp.concatenate([col["i_m"], col["f_m"]], -1))]),
        "w_ssd": cat([col["z_s"], col["xbc_s"], _pad_lanes(col["dt_s"])]),
        "w_gla": cat([col["q_g"], col["k_g"], col["v_g"], col["g_g"], _pad_lanes(col["lr_g"])]),
        "w_gate": w_gate,
        "mlstm_conv_w": mlstm_conv_w, "mlstm_conv_b": row(mlstm_conv_b),
        "mlstm_bif": row(_pad_lanes(jnp.concatenate([mlstm_b_i, mlstm_b_f], -1))),
        "mlstm_norm_g": row(mlstm_norm_g),
        "ssd_conv_w": ssd_conv_w, "ssd_conv_b": row(ssd_conv_b),
        "ssd_dtb": row(_pad_lanes(ssd_dt_bias)), "ssd_alog": row(_pad_lanes(ssd_a_log)),
        "ssd_dfull": row(jnp.repeat(ssd_d, P_S, axis=-1)), "ssd_norm_g": row(ssd_norm_g),
        "gla_w_lr": jnp.pad(gla_w_lr, ((0, 0), (0, LANES - R_G), (0, 0))).astype(BF16),
        "gla_b_lr": row(gla_b_lr), "gla_norm_g": row(gla_norm_g),
        "b_branch": row(b_branch), "w_branch": w_branch.astype(BF16), "w_out": w_out.astype(BF16),
        "ln_g": ln_g, "ln_b": ln_b,
        "w_rt": _pad_lanes(jnp.concatenate([w_router, w_grp], -1)).astype(BF16),
        "b_rt": row(_pad_lanes(jnp.concatenate([b_router, b_grp], -1))),
        "w_e_gate": w_e_gate.astype(BF16), "w_e_up": w_e_up.astype(BF16), "w_e_down": w_e_down.astype(BF16),
    }

    ada_all = _ada_call(jnp.concatenate([c_prompt, c_sample], axis=0), w_ada, b_ada)

    sample_states = (state_mlstm_c, state_mlstm_n, state_mlstm_m, state_mlstm_conv,
                     state_ssd, state_ssd_conv, state_gla)
    prompt_states = tuple(jnp.zeros((s.shape[0], nbp) + s.shape[2:], x_prompt.dtype) for s in sample_states)

    y_prompt, new_p = _trunk(x_prompt, ada_all[:, :nbp], prompt_states, P, depth, alpha)
    y_sample, new_s = _trunk(x_sample, ada_all[:, nbp:], sample_states, P, depth, alpha)
    return (y_prompt, y_sample) + new_p + new_s
```

```python
import functools

import jax
import jax.numpy as jnp
import numpy as np
from jax import lax
from jax.experimental import pallas as pl
from jax.experimental.pallas import tpu as pltpu

F32 = jnp.float32
BF16 = jnp.bfloat16
NEG_INF = float("-inf")

CHUNK = 64
CONV_W = 4
NH_M, DQK_M, DV_M = 4, 128, 256
QK_M, VM = NH_M * DQK_M, NH_M * DV_M
NH_S, P_S, N_S, G_S = 16, 64, 128, 4
HPG_S = NH_S // G_S
XS, BC_S = NH_S * P_S, G_S * N_S
CONV_S_DIM = XS + 2 * BC_S
NH_G, DK_G, DV_G = 4, 128, 256
QK_G, VG = NH_G * DK_G, NH_G * DV_G
R_G = 16
TAU_G = 16.0
N_GROUPS, EXP_PER_GROUP, TOP_K = 4, 8, 2
N_EXPERTS = N_GROUPS * EXP_PER_GROUP
LN_EPS = 1e-5
LANES = 128

COL_SIZES = (2 * QK_M, VM, VM, NH_M, NH_M, XS, CONV_S_DIM, NH_S, QK_G, QK_G, VG, VG, R_G)
COL_NAMES = ("qk_m", "v_m", "o_m", "i_m", "f_m", "z_s", "xbc_s", "dt_s", "q_g", "k_g", "v_g", "g_g", "lr_g")

VMEM_LIMIT = 56 * 1024 * 1024
MOE_TILE = 256


def _cparams(n_axes):
    return pltpu.CompilerParams(dimension_semantics=("arbitrary",) * n_axes, vmem_limit_bytes=VMEM_LIMIT)


def _dot(a, b):
    return jnp.dot(a, b, preferred_element_type=F32)


def _dot_nt(a, b):
    return lax.dot_general(a, b, (((1,), (1,)), ((), ())), preferred_element_type=F32)


def _split3(x):
    hi = x.astype(BF16)
    r = x - hi.astype(F32)
    mid = r.astype(BF16)
    lo = (r - mid.astype(F32)).astype(BF16)
    return hi, mid, lo


def _eye(n, m):
    r = lax.broadcasted_iota(jnp.int32, (n, m), 0)
    c = lax.broadcasted_iota(jnp.int32, (n, m), 1)
    return jnp.where(r == c, 1.0, 0.0).astype(BF16)


def _transpose_rows(x, n):
    e = _eye(n, x.shape[1])
    hi, mid, lo = _split3(x)
    return _dot_nt(e, lo) + _dot_nt(e, mid) + _dot_nt(e, hi)


def _block_tril(Tt, L):
    r = lax.broadcasted_iota(jnp.int32, (Tt, Tt), 0)
    c = lax.broadcasted_iota(jnp.int32, (Tt, Tt), 1)
    return jnp.where(((r // L) == (c // L)) & (c <= r), 1.0, 0.0).astype(BF16)


def _block_full(Tt, L):
    r = lax.broadcasted_iota(jnp.int32, (Tt, Tt), 0)
    c = lax.broadcasted_iota(jnp.int32, (Tt, Tt), 1)
    return jnp.where((r // L) == (c // L), 1.0, 0.0).astype(BF16)


def _chunk_last(x, L):
    n = x.shape[0] // L
    return jnp.concatenate([jnp.broadcast_to(x[(c + 1) * L - 1:(c + 1) * L, :], (L, x.shape[1])) for c in range(n)], axis=0)


def _transpose_f32(x):
    if x.shape[0] % LANES == 0:
        return x.T
    return _transpose_rows(x, x.shape[1])


def _log_sigmoid(x):
    return jnp.minimum(x, 0.0) - jnp.log1p(jnp.exp(-jnp.abs(x)))


def _softplus(x):
    return jnp.maximum(x, 0.0) + jnp.log1p(jnp.exp(-jnp.abs(x)))


def _silu(x):
    return x * jax.nn.sigmoid(x)


def _modulate(x, ada, shift_row, scale_row):
    return x * (1.0 + ada[scale_row:scale_row + 1]) + ada[shift_row:shift_row + 1]


def _causal_conv(buf, x, w_ref, b_ref, Tt):
    buf[8:8 + Tt, :] = x
    y = buf[5:5 + Tt, :] * w_ref[0, 0:1, :]
    y = y + buf[6:6 + Tt, :] * w_ref[0, 1:2, :]
    y = y + buf[7:7 + Tt, :] * w_ref[0, 2:3, :]
    y = y + x * w_ref[0, 3:4, :]
    y = y + b_ref[0]
    buf[5:8, :] = buf[5 + Tt:8 + Tt, :]
    return y


def _group_norm(x, n_groups, center):
    w = x.shape[1] // n_groups
    outs = []
    for g in range(n_groups):
        xg = x[:, g * w:(g + 1) * w]
        if center:
            xg = xg - jnp.mean(xg, axis=1, keepdims=True)
        outs.append(xg * lax.rsqrt(jnp.mean(xg * xg, axis=1, keepdims=True) + LN_EPS))
    return jnp.concatenate(outs, axis=1)


def _ada_kernel(c_ref, w_ref, b_ref, o_ref):
    c = _silu(c_ref[...]).astype(BF16)
    o_ref[0] = _dot(c, w_ref[0].astype(BF16)) + b_ref[0]


def _ada_call(c_all, w_ada, b_ada):
    depth, d, n6 = w_ada.shape
    nb = c_all.shape[0]
    tn = 1536
    return pl.pallas_call(
        _ada_kernel,
        grid=(depth, n6 // tn),
        in_specs=[pl.BlockSpec((nb, d), lambda l, j: (0, 0)),
                  pl.BlockSpec((1, d, tn), lambda l, j: (l, 0, j)),
                  pl.BlockSpec((1, 1, tn), lambda l, j: (l, 0, j))],
        out_specs=pl.BlockSpec((1, nb, tn), lambda l, j: (l, 0, j)),
        out_shape=jax.ShapeDtypeStruct((depth, nb, n6), F32),
        compiler_params=_cparams(2),
        name="ada",
    )(c_all, w_ada, b_ada.reshape(depth, 1, n6))


def _mlstm_kernel(x_ref, ada_ref, w_ref, cw_ref, cb_ref, bif_ref, ng_ref, c0_ref, n0_ref, m0_ref, cv0_ref,
                  out_ref, c_out, n_out, m_out, cv_out,
                  conv_s, q_s, qb_s, kb_s, v_s, o_s, kwt_s, un_s, nu_s, p_s, rs_s, b_s, ml_s, n_s, m_s, *ct_s, Tt, L):
    t = pl.program_id(1)
    nc = Tt // L

    @pl.when(t == 0)
    def _():
        for h in range(NH_M):
            ct_s[h][...] = c0_ref[0, h]
        n_s[...] = n0_ref[0]
        m_s[...] = m0_ref[0]
        conv_s[5:8, :] = cv0_ref[0]

    u = _modulate(x_ref[0], ada_ref[0], 0, 1).astype(BF16)
    g = _dot(u, w_ref[0, :, 2 * QK_M + 2 * VM:]) + bif_ref[0]
    qk = _dot(u, w_ref[0, :, 0:2 * QK_M])
    lane = lax.broadcasted_iota(jnp.int32, g.shape, 1)
    g = jnp.where(lane < NH_M, g, _log_sigmoid(g))

    tril = _block_tril(Tt, L)
    hi, mid, lo = _split3(g)
    cs = _dot(tril, lo) + _dot(tril, mid) + _dot(tril, hi)
    v_s[...] = _dot(u, w_ref[0, :, 2 * QK_M:2 * QK_M + VM]).astype(BF16)
    qk = _silu(_causal_conv(conv_s, qk, cw_ref, cb_ref, Tt))
    q = qk[:, :QK_M]
    k = qk[:, QK_M:] * DQK_M ** -0.5
    q_s[...] = q
    qb_s[...] = q.astype(BF16)
    kb_s[...] = k.astype(BF16)
    o_s[...] = _dot(u, w_ref[0, :, 2 * QK_M + VM:2 * QK_M + 2 * VM])
    b0 = pltpu.roll(cs, LANES - NH_M, axis=1)
    r = g - b0
    row = lax.broadcasted_iota(jnp.int32, g.shape, 0) % L
    a = r
    sh = 1
    while sh < L:
        a = jnp.maximum(a, jnp.where(row >= sh, pltpu.roll(a, sh, axis=0), NEG_INF))
        sh *= 2
    al = a
    sh = 1
    while sh < L:
        al = jnp.maximum(al, jnp.where(row + sh < L, pltpu.roll(al, Tt - sh, axis=0), NEG_INF))
        sh *= 2
    b_s[...] = b0
    ml_s[...] = b0 + a
    wsrc = jnp.exp(r - al)
    r_rows = _transpose_f32(r)[0:8, :]
    col = lax.broadcasted_iota(jnp.int32, (DQK_M, Tt), 1)
    rr = lax.broadcasted_iota(jnp.int32, (L, L), 0)
    cc = lax.broadcasted_iota(jnp.int32, (L, L), 1)
    causal = cc <= rr
    for h in range(NH_M):
        ks = slice(h * DQK_M, (h + 1) * DQK_M)
        kw = wsrc[:, h:h + 1] * k[:, ks]
        kwt_s[h] = _transpose_f32(kw).astype(BF16)
        for c in range(nc):
            nu_s[c * NH_M + h:c * NH_M + h + 1, :] = jnp.sum(kw[c * L:(c + 1) * L], axis=0, keepdims=True)
    for c in range(nc):
        for h in range(NH_M):
            vs = slice(h * DV_M, (h + 1) * DV_M)
            kwm = jnp.where((col // L) == c, kwt_s[h], jnp.zeros((), BF16)) if nc > 1 else kwt_s[h]
            un_s[c * NH_M + h] = _dot(kwm, v_s[:, vs])
    for c in range(nc):
        rows = slice(c * L, (c + 1) * L)
        sl = []
        for h in range(NH_M):
            ks = slice(h * DQK_M, (h + 1) * DQK_M)
            dloc = jnp.exp(jnp.where(causal, r_rows[h:h + 1, rows] - a[rows, h:h + 1], NEG_INF))
            sl.append(_dot_nt(qb_s[rows, ks], kb_s[rows, ks]) * dloc)
        for h in range(NH_M):
            vs = slice(h * DV_M, (h + 1) * DV_M)
            p_s[rows, vs] = _dot(sl[h].astype(BF16), v_s[rows, vs])
            rs_s[rows, h:h + 1] = jnp.sum(sl[h], axis=1, keepdims=True)

    lane_l = lax.broadcasted_iota(jnp.int32, (L, LANES), 1)
    for c in range(nc):
        rows = slice(c * L, (c + 1) * L)
        m_prev = m_s[...]
        bc = b_s[rows, :]
        mlc = ml_s[rows, :]
        mt = jnp.maximum(bc + m_prev, mlc)
        corr = jnp.exp(mlc - mt)
        w_inter = jnp.exp(bc + m_prev - mt)
        inter = []
        qn = jnp.zeros((L, LANES), F32)
        for h in range(NH_M):
            ks = slice(h * DQK_M, (h + 1) * DQK_M)
            inter.append(_dot(qb_s[rows, ks], ct_s[h][...].astype(BF16)))
            qn_h = jnp.sum(q_s[rows, ks] * n_s[h:h + 1, :], axis=1, keepdims=True)
            qn = jnp.where(lane_l == h, qn_h, qn)
        den = corr * rs_s[rows, :] + w_inter * qn
        inv = 1.0 / jnp.maximum(jnp.abs(den), jnp.exp(-mt))
        ca = corr * inv
        cb2 = w_inter * inv
        for h in range(NH_M):
            vs = slice(h * DV_M, (h + 1) * DV_M)
            p_s[rows, vs] = ca[:, h:h + 1] * p_s[rows, vs] + cb2[:, h:h + 1] * inter[h]
        m_last = mt[L - 1:L, :]
        c_l = jnp.exp(mlc[L - 1:L, :] - m_last)
        decay = jnp.exp(bc[L - 1:L, :] + m_prev - m_last)
        for h in range(NH_M):
            ct_s[h][...] = decay[:, h:h + 1] * ct_s[h][...] + c_l[:, h:h + 1] * un_s[c * NH_M + h]
            n_s[h:h + 1, :] = decay[:, h:h + 1] * n_s[h:h + 1, :] + c_l[:, h:h + 1] * nu_s[c * NH_M + h:c * NH_M + h + 1, :]
        m_s[...] = m_last

    hn = _group_norm(p_s[...], NH_M, True) * ng_ref[0] * jax.nn.sigmoid(o_s[...])
    out_ref[0] = hn.astype(out_ref.dtype)

    @pl.when(t == pl.num_programs(1) - 1)
    def _():
        for h in range(NH_M):
            c_out[0, h] = ct_s[h][...]
        n_out[0] = n_s[...]
        m_out[0] = m_s[...]
        cv_out[0] = conv_s[5:8, :]


def _mlstm_call(l, x, ada, w, cw, cb, bif, ng, c0t, n0, m0, cv0, Tt):
    B, T, D = x.shape
    L = min(CHUNK, T)
    wcols = w.shape[2]
    bmap = lambda b, t: (b, 0, 0)
    lmap = lambda b, t: (l, 0, 0)
    return pl.pallas_call(
        functools.partial(_mlstm_kernel, Tt=Tt, L=L),
        grid=(B, T // Tt),
        in_specs=[pl.BlockSpec((1, Tt, D), lambda b, t: (b, t, 0)),
                  pl.BlockSpec((1, 6, D), bmap),
                  pl.BlockSpec((1, D, wcols), lmap),
                  pl.BlockSpec((1, CONV_W, 2 * QK_M), lmap),
                  pl.BlockSpec((1, 1, 2 * QK_M), lmap),
                  pl.BlockSpec((1, 1, LANES), lmap),
                  pl.BlockSpec((1, 1, VM), lmap),
                  pl.BlockSpec((1, NH_M, DQK_M, DV_M), lambda b, t: (b, 0, 0, 0)),
                  pl.BlockSpec((1, NH_M, DQK_M), bmap),
                  pl.BlockSpec((1, 1, LANES), bmap),
                  pl.BlockSpec((1, CONV_W - 1, 2 * QK_M), bmap)],
        out_specs=[pl.BlockSpec((1, Tt, VM), lambda b, t: (b, t, 0)),
                   pl.BlockSpec((1, NH_M, DQK_M, DV_M), lambda b, t: (b, 0, 0, 0)),
                   pl.BlockSpec((1, NH_M, DQK_M), bmap),
                   pl.BlockSpec((1, 1, LANES), bmap),
                   pl.BlockSpec((1, CONV_W - 1, 2 * QK_M), bmap)],
        out_shape=[jax.ShapeDtypeStruct((B, T, VM), BF16),
                   jax.ShapeDtypeStruct((B, NH_M, DQK_M, DV_M), F32),
                   jax.ShapeDtypeStruct((B, NH_M, DQK_M), F32),
                   jax.ShapeDtypeStruct((B, 1, LANES), F32),
                   jax.ShapeDtypeStruct((B, CONV_W - 1, 2 * QK_M), F32)],
        scratch_shapes=[pltpu.VMEM((8 + Tt, 2 * QK_M), F32),
                        pltpu.VMEM((Tt, QK_M), F32),
                        pltpu.VMEM((Tt, QK_M), BF16),
                        pltpu.VMEM((Tt, QK_M), BF16),
                        pltpu.VMEM((Tt, VM), BF16),
                        pltpu.VMEM((Tt, VM), F32),
                        pltpu.VMEM((NH_M, DQK_M, Tt), BF16),
                        pltpu.VMEM((Tt // L * NH_M, DQK_M, DV_M), F32),
                        pltpu.VMEM((Tt // L * NH_M, DQK_M), F32),
                        pltpu.VMEM((Tt, VM), F32),
                        pltpu.VMEM((Tt, LANES), F32),
                        pltpu.VMEM((Tt, LANES), F32),
                        pltpu.VMEM((Tt, LANES), F32),
                        pltpu.VMEM((NH_M, DQK_M), F32),
                        pltpu.VMEM((1, LANES), F32)] + [pltpu.VMEM((DQK_M, DV_M), F32)] * NH_M,
        compiler_params=_cparams(2),
        name="mlstm",
    )(x, ada, w, cw, cb, bif, ng, c0t, n0, m0, cv0)


def _pair_cols(x, hd, lane):
    return jnp.where(lane < P_S, x[:, hd:hd + 1], x[:, hd + 1:hd + 2])


def _ssd_kernel(x_ref, ada_ref, w_ref, cw_ref, cb_ref, dtb_ref, alog_ref, d_ref, ng_ref, h0_ref, cv0_ref,
                out_ref, h_out, cv_out,
                conv_s, zg_s, x_s, xw_s, b_s, c_s, bt_s, e2_s, un_s, dec_s, y_s, *ht_s, Tt, L):
    t = pl.program_id(1)
    nc = Tt // L
    n_pairs = NH_S // 2

    @pl.when(t == 0)
    def _():
        for g in range(G_S):
            ht_s[g][...] = h0_ref[0, g]
        conv_s[5:8, :] = cv0_ref[0]

    u = _modulate(x_ref[0], ada_ref[0], 0, 1).astype(BF16)
    dt = _softplus(_dot(u, w_ref[0, :, XS + CONV_S_DIM:]) + dtb_ref[0])
    xbc = _dot(u, w_ref[0, :, XS:XS + CONV_S_DIM])
    a_row = -jnp.exp(alog_ref[0])

    tril = _block_tril(Tt, L)
    full = _block_full(Tt, L)
    hi, mid, lo = _split3(dt * a_row)
    cs = _dot(tril, lo) + _dot(tril, mid) + _dot(tril, hi)
    csl = _dot(full, lo) + _dot(full, mid) + _dot(full, hi)
    xbc = _silu(_causal_conv(conv_s, xbc, cw_ref, cb_ref, Tt))
    x = xbc[:, :XS]
    bm = xbc[:, XS:XS + BC_S]
    x_s[...] = x
    b_s[...] = bm.astype(BF16)
    c_s[...] = xbc[:, XS + BC_S:].astype(BF16)
    wsrc = jnp.exp(csl - cs) * dt
    dec = jnp.exp(csl)
    cs_rows = _transpose_f32(cs)[0:NH_S, :]
    dt_rows = _transpose_f32(dt)[0:NH_S, :]
    for g in range(G_S):
        bt_s[g] = _transpose_f32(bm[:, g * N_S:(g + 1) * N_S]).astype(BF16)
    zg_s[...] = _silu(_dot(u, w_ref[0, :, 0:XS]))
    lane_t = lax.broadcasted_iota(jnp.int32, (Tt, LANES), 1)
    for pp in range(n_pairs):
        pc = slice(pp * LANES, (pp + 1) * LANES)
        e2_s[:, pc] = _pair_cols(cs, 2 * pp, lane_t)
        xw_s[:, pc] = (x[:, pc] * _pair_cols(wsrc, 2 * pp, lane_t)).astype(BF16)
    lane1 = lax.broadcasted_iota(jnp.int32, (1, LANES), 1)
    for c in range(nc):
        for pp in range(n_pairs):
            r0 = c * L
            dec_s[c * n_pairs + pp:c * n_pairs + pp + 1, :] = _pair_cols(dec[r0:r0 + 1, :], 2 * pp, lane1)

    rr = lax.broadcasted_iota(jnp.int32, (L, LANES), 0)
    lane_l = lax.broadcasted_iota(jnp.int32, (L, LANES), 1)
    causal2 = (lane_l % P_S) <= rr
    col = lax.broadcasted_iota(jnp.int32, (N_S, Tt), 1)
    zero_b = jnp.zeros((), BF16)
    for c in range(nc):
        rows = slice(c * L, (c + 1) * L)
        for g in range(G_S):
            gs = slice(g * N_S, (g + 1) * N_S)
            bg = b_s[rows, gs]
            cb2 = _dot_nt(c_s[rows, gs], jnp.concatenate([bg, bg], axis=0))
            btm = jnp.where((col // L) == c, bt_s[g], zero_b) if nc > 1 else bt_s[g]
            un_s[c * G_S + g] = _dot(btm, xw_s[:, g * 2 * LANES:(g + 1) * 2 * LANES])
            for p2 in range(2):
                pp = g * 2 + p2
                hd = 2 * pp
                pc = slice(pp * LANES, (pp + 1) * LANES)
                csr2 = jnp.concatenate([cs_rows[hd:hd + 1, rows], cs_rows[hd + 1:hd + 2, rows]], axis=1)
                dtr2 = jnp.concatenate([dt_rows[hd:hd + 1, rows], dt_rows[hd + 1:hd + 2, rows]], axis=1)
                seg2 = jnp.exp(jnp.where(causal2, e2_s[rows, pc] - csr2, NEG_INF))
                m2 = (cb2 * seg2 * dtr2).astype(BF16)
                xp = x_s[rows, pc].astype(BF16)
                xbd = jnp.concatenate([jnp.where(lane_l < P_S, xp, zero_b), jnp.where(lane_l >= P_S, xp, zero_b)], axis=0)
                y_s[rows, pc] = _dot(m2, xbd)

    for c in range(nc):
        rows = slice(c * L, (c + 1) * L)
        for g in range(G_S):
            gc = slice(g * 2 * LANES, (g + 1) * 2 * LANES)
            yi = _dot(c_s[rows, g * N_S:(g + 1) * N_S], ht_s[g][...].astype(BF16))
            y_s[rows, gc] += jnp.exp(e2_s[rows, gc]) * yi
        for g in range(G_S):
            for p2 in range(2):
                pp = g * 2 + p2
                hc = slice(p2 * LANES, (p2 + 1) * LANES)
                ht_s[g][:, hc] = (dec_s[c * n_pairs + pp:c * n_pairs + pp + 1, :] * ht_s[g][:, hc]
                                  + un_s[c * G_S + g][:, hc])

    y = (y_s[...] + d_ref[0] * x_s[...]) * zg_s[...]
    out_ref[0] = (_group_norm(y, G_S, False) * ng_ref[0]).astype(out_ref.dtype)

    @pl.when(t == pl.num_programs(1) - 1)
    def _():
        for g in range(G_S):
            h_out[0, g] = ht_s[g][...]
        cv_out[0] = conv_s[5:8, :]


def _ssd_call(l, x, ada, w, cw, cb, dtb, alog, dfull, ng, h0t, cv0, Tt):
    B, T, D = x.shape
    L = min(CHUNK, T)
    wcols = w.shape[2]
    bmap = lambda b, t: (b, 0, 0)
    lmap = lambda b, t: (l, 0, 0)
    hshape = (G_S, N_S, HPG_S * P_S)
    assert L == P_S and 2 * L == LANES and T % Tt == 0 and Tt % L == 0
    return pl.pallas_call(
        functools.partial(_ssd_kernel, Tt=Tt, L=L),
        grid=(B, T // Tt),
        in_specs=[pl.BlockSpec((1, Tt, D), lambda b, t: (b, t, 0)),
                  pl.BlockSpec((1, 6, D), bmap),
                  pl.BlockSpec((1, D, wcols), lmap),
                  pl.BlockSpec((1, CONV_W, CONV_S_DIM), lmap),
                  pl.BlockSpec((1, 1, CONV_S_DIM), lmap),
                  pl.BlockSpec((1, 1, LANES), lmap),
                  pl.BlockSpec((1, 1, LANES), lmap),
                  pl.BlockSpec((1, 1, XS), lmap),
                  pl.BlockSpec((1, 1, XS), lmap),
                  pl.BlockSpec((1,) + hshape, lambda b, t: (b, 0, 0, 0)),
                  pl.BlockSpec((1, CONV_W - 1, CONV_S_DIM), bmap)],
        out_specs=[pl.BlockSpec((1, Tt, XS), lambda b, t: (b, t, 0)),
                   pl.BlockSpec((1,) + hshape, lambda b, t: (b, 0, 0, 0)),
                   pl.BlockSpec((1, CONV_W - 1, CONV_S_DIM), bmap)],
        out_shape=[jax.ShapeDtypeStruct((B, T, XS), BF16),
                   jax.ShapeDtypeStruct((B,) + hshape, F32),
                   jax.ShapeDtypeStruct((B, CONV_W - 1, CONV_S_DIM), F32)],
        scratch_shapes=[pltpu.VMEM((8 + Tt, CONV_S_DIM), F32),
                        pltpu.VMEM((Tt, XS), F32),
                        pltpu.VMEM((Tt, XS), F32),
                        pltpu.VMEM((Tt, XS), BF16),
                        pltpu.VMEM((Tt, BC_S), BF16),
                        pltpu.VMEM((Tt, BC_S), BF16),
                        pltpu.VMEM((G_S, N_S, Tt), BF16),
                        pltpu.VMEM((Tt, XS), F32),
                        pltpu.VMEM((Tt // L * G_S, N_S, HPG_S * P_S), F32),
                        pltpu.VMEM((Tt // L * (NH_S // 2), LANES), F32),
                        pltpu.VMEM((Tt, XS), F32)] + [pltpu.VMEM(hshape[1:], F32)] * G_S,
        compiler_params=_cparams(2),
        name="ssd",
    )(x, ada, w, cw, cb, dtb, alog, dfull, ng, h0t, cv0)


def _gla_kernel(x_ref, ada_ref, w_ref, wlr_ref, blr_ref, ng_ref, s0_ref,
                out_ref, s_out,
                sg_s, qg_s, kg_s, v_s, kdt_s, egt_s, un_s, o_s, *st_s, Tt, L):
    t = pl.program_id(1)
    nc = Tt // L

    @pl.when(t == 0)
    def _():
        for h in range(NH_G):
            st_s[h][...] = s0_ref[0, h]

    u = _modulate(x_ref[0], ada_ref[0], 0, 1).astype(BF16)
    lr = _dot(u, w_ref[0, :, 2 * QK_G + 2 * VG:]).astype(BF16)
    lg = _log_sigmoid(_dot(lr, wlr_ref[0]) + blr_ref[0]) / TAU_G
    q = _dot(u, w_ref[0, :, 0:QK_G])
    k = _dot(u, w_ref[0, :, QK_G:2 * QK_G]) * DK_G ** -0.5

    tril = _block_tril(Tt, L)
    hi, mid, lo = _split3(lg)
    G = _dot(tril, lo) + _dot(tril, mid) + _dot(tril, hi)
    v_s[...] = _dot(u, w_ref[0, :, 2 * QK_G:2 * QK_G + VG]).astype(BF16)
    if nc > 1:
        Gl = _chunk_last(G, L)
    else:
        full = _block_full(Tt, L)
        Gl = _dot(full, lo) + _dot(full, mid) + _dot(full, hi)
    qg_s[...] = (q * jnp.exp(G)).astype(BF16)
    kg_s[...] = (k * jnp.exp(-G)).astype(BF16)
    kd = k * jnp.exp(Gl - G)
    eg = jnp.exp(Gl)
    sg_s[...] = _silu(_dot(u, w_ref[0, :, 2 * QK_G + VG:2 * QK_G + 2 * VG]))
    for h in range(NH_G):
        ks = slice(h * DK_G, (h + 1) * DK_G)
        kdt_s[h] = _transpose_f32(kd[:, ks]).astype(BF16)
        egt_s[h] = _transpose_f32(eg[:, ks])
    col = lax.broadcasted_iota(jnp.int32, (DK_G, Tt), 1)
    for c in range(nc):
        for h in range(NH_G):
            vs = slice(h * DV_G, (h + 1) * DV_G)
            kdm = jnp.where((col // L) == c, kdt_s[h], jnp.zeros((), BF16)) if nc > 1 else kdt_s[h]
            un_s[c * NH_G + h] = _dot(kdm, v_s[:, vs])

    rr = lax.broadcasted_iota(jnp.int32, (L, L), 0)
    cc = lax.broadcasted_iota(jnp.int32, (L, L), 1)
    causal = cc <= rr
    for c in range(nc):
        rows = slice(c * L, (c + 1) * L)
        att = []
        for h in range(NH_G):
            ks = slice(h * DK_G, (h + 1) * DK_G)
            att.append(_dot_nt(qg_s[rows, ks], kg_s[rows, ks]))
        for h in range(NH_G):
            ks = slice(h * DK_G, (h + 1) * DK_G)
            vs = slice(h * DV_G, (h + 1) * DV_G)
            o_s[rows, vs] = _dot(qg_s[rows, ks], st_s[h][...].astype(BF16))
        for h in range(NH_G):
            vs = slice(h * DV_G, (h + 1) * DV_G)
            o_s[rows, vs] += _dot(jnp.where(causal, att[h], 0.0).astype(BF16), v_s[rows, vs])
        for h in range(NH_G):
            st_s[h][...] = egt_s[h][:, c * L:c * L + 1] * st_s[h][...] + un_s[c * NH_G + h]

    out_ref[0] = (_group_norm(o_s[...], NH_G, False) * ng_ref[0] * sg_s[...]).astype(out_ref.dtype)

    @pl.when(t == pl.num_programs(1) - 1)
    def _():
        for h in range(NH_G):
            s_out[0, h] = st_s[h][...]


def _gla_call(l, x, ada, w, wlr, blr, ng, s0, Tt):
    B, T, D = x.shape
    L = min(CHUNK, T)
    wcols = w.shape[2]
    bmap = lambda b, t: (b, 0, 0)
    lmap = lambda b, t: (l, 0, 0)
    sshape = (NH_G, DK_G, DV_G)
    return pl.pallas_call(
        functools.partial(_gla_kernel, Tt=Tt, L=L),
        grid=(B, T // Tt),
        in_specs=[pl.BlockSpec((1, Tt, D), lambda b, t: (b, t, 0)),
                  pl.BlockSpec((1, 6, D), bmap),
                  pl.BlockSpec((1, D, wcols), lmap),
                  pl.BlockSpec((1, LANES, QK_G), lmap),
                  pl.BlockSpec((1, 1, QK_G), lmap),
                  pl.BlockSpec((1, 1, VG), lmap),
                  pl.BlockSpec((1,) + sshape, lambda b, t: (b, 0, 0, 0))],
        out_specs=[pl.BlockSpec((1, Tt, VG), lambda b, t: (b, t, 0)),
                   pl.BlockSpec((1,) + sshape, lambda b, t: (b, 0, 0, 0))],
        out_shape=[jax.ShapeDtypeStruct((B, T, VG), BF16),
                   jax.ShapeDtypeStruct((B,) + sshape, F32)],
        scratch_shapes=[pltpu.VMEM((Tt, VG), F32),
                        pltpu.VMEM((Tt, QK_G), BF16),
                        pltpu.VMEM((Tt, QK_G), BF16),
                        pltpu.VMEM((Tt, VG), BF16),
                        pltpu.VMEM((NH_G, DK_G, Tt), BF16),
                        pltpu.VMEM((NH_G, DK_G, Tt), F32),
                        pltpu.VMEM((Tt // L * NH_G, DK_G, DV_G), F32),
                        pltpu.VMEM((Tt, VG), F32)] + [pltpu.VMEM(sshape[1:], F32)] * NH_G,
        compiler_params=_cparams(2),
        name="gla",
    )(x, ada, w, wlr, blr, ng, s0)


def _layer_norm(x, g, b):
    mu = jnp.mean(x, axis=1, keepdims=True)
    xc = x - mu
    var = jnp.mean(xc * xc, axis=1, keepdims=True)
    return xc * lax.rsqrt(var + LN_EPS) * g + b


def _merge_kernel(x_ref, ada_ref, hm_ref, ys_ref, og_ref, wg_ref, bb_ref, wb_ref, wo_ref, lng_ref, lnb_ref,
                  wrt_ref, brt_ref,
                  x1_ref, u2_ref, eid_ref, wts_ref, *, alpha):
    x = x_ref[0]
    ada = ada_ref[0]
    D = x.shape[1]
    u = _modulate(x, ada, 0, 1).astype(BF16)
    gate = jax.nn.sigmoid(_dot(u, wg_ref[0]) + bb_ref[0])
    merged = (gate[:, 0:D] * _dot(hm_ref[0], wb_ref[0, 0])
              + gate[:, D:2 * D] * _dot(ys_ref[0], wb_ref[0, 1])
              + gate[:, 2 * D:3 * D] * _dot(og_ref[0], wb_ref[0, 2]))
    y = _dot(merged.astype(BF16), wo_ref[0])
    x1 = _layer_norm(alpha * x + ada[2:3] * y, lng_ref[0, 0:1], lnb_ref[0, 0:1])
    x1_ref[0] = x1
    u2 = _modulate(x1, ada, 3, 4).astype(BF16)
    bits = lax.bitcast_convert_type(u2.astype(F32), jnp.uint32)
    u2_ref[0] = (bits[:, :D // 2] >> 16) | bits[:, D // 2:]

    logits = _dot(u2, wrt_ref[0]) + brt_ref[0]
    lane = lax.broadcasted_iota(jnp.int32, logits.shape, 1)
    lane_f = lane.astype(F32)
    big = float(LANES)
    is_g = (lane >= N_EXPERTS) & (lane < N_EXPERTS + N_GROUPS)
    gmax = jnp.max(jnp.where(is_g, logits, NEG_INF), axis=1, keepdims=True)
    gsum = jnp.sum(jnp.where(is_g, jnp.exp(logits - gmax), 0.0), axis=1, keepdims=True)
    pg_top = 1.0 / gsum
    g_lane = jnp.min(jnp.where(is_g & (logits == gmax), lane_f, big), axis=1, keepdims=True)
    g_idx = g_lane.astype(jnp.int32) - N_EXPERTS
    in_grp = (lane < N_EXPERTS) & ((lane // EXP_PER_GROUP) == g_idx)
    el = jnp.where(in_grp, logits, NEG_INF)
    v1 = jnp.max(el, axis=1, keepdims=True)
    i1 = jnp.min(jnp.where(in_grp & (el == v1), lane_f, big), axis=1, keepdims=True)
    rest = in_grp & (lane_f != i1)
    el2 = jnp.where(rest, logits, NEG_INF)
    v2 = jnp.max(el2, axis=1, keepdims=True)
    i2 = jnp.min(jnp.where(rest & (el2 == v2), lane_f, big), axis=1, keepdims=True)
    e = jnp.exp(v2 - v1)
    w1 = pg_top / (1.0 + e)
    w2 = pg_top * e / (1.0 + e)
    eid_ref[0] = jnp.where(lane == 0, i1, jnp.where(lane == 1, i2, 0.0)).astype(jnp.int32)
    wts_ref[0] = jnp.where(lane == 0, w1, jnp.where(lane == 1, w2, 0.0))


def _merge_call(l, x, ada, hm, ys, og, wg, bb, wb, wo, lng, lnb, wrt, brt, tm, alpha):
    B, T, D = x.shape
    bmap = lambda b, t: (b, 0, 0)
    lmap = lambda b, t: (l, 0, 0)
    tmap = lambda b, t: (b, t, 0)
    return pl.pallas_call(
        functools.partial(_merge_kernel, alpha=alpha),
        grid=(B, T // tm),
        in_specs=[pl.BlockSpec((1, tm, D), tmap),
                  pl.BlockSpec((1, 6, D), bmap),
                  pl.BlockSpec((1, tm, D), tmap),
                  pl.BlockSpec((1, tm, D), tmap),
                  pl.BlockSpec((1, tm, D), tmap),
                  pl.BlockSpec((1, D, 3 * D), lmap),
                  pl.BlockSpec((1, 1, 3 * D), lmap),
                  pl.BlockSpec((1, 3, D, D), lambda b, t: (l, 0, 0, 0)),
                  pl.BlockSpec((1, D, D), lmap),
                  pl.BlockSpec((1, 2, D), lmap),
                  pl.BlockSpec((1, 2, D), lmap),
                  pl.BlockSpec((1, D, LANES), lmap),
                  pl.BlockSpec((1, 1, LANES), lmap)],
        out_specs=[pl.BlockSpec((1, tm, D), tmap),
                   pl.BlockSpec((1, tm, D // 2), tmap),
                   pl.BlockSpec((1, tm, LANES), tmap),
                   pl.BlockSpec((1, tm, LANES), tmap)],
        out_shape=[jax.ShapeDtypeStruct((B, T, D), F32),
                   jax.ShapeDtypeStruct((B, T, D // 2), jnp.uint32),
                   jax.ShapeDtypeStruct((B, T, LANES), jnp.int32),
                   jax.ShapeDtypeStruct((B, T, LANES), F32)],
        compiler_params=_cparams(2),
        name="merge",
    )(x, ada, hm, ys, og, wg, bb, wb, wo, lng, lnb, wrt, brt)


def _moe_kernel(te_ref, xs_ref, wg_ref, wu_ref, wd_ref, o_ref, wg_b, wu_b, wd_b):
    i = pl.program_id(0)
    prev = te_ref[jnp.maximum(i - 1, 0)]

    @pl.when((i == 0) | (te_ref[i] != prev))
    def _():
        wg_b[...] = wg_ref[0, 0].astype(BF16)
        wu_b[...] = wu_ref[0, 0].astype(BF16)
        wd_b[...] = wd_ref[0, 0].astype(BF16)

    w = xs_ref[...]
    lo = lax.bitcast_convert_type(w << 16, F32)
    hi = lax.bitcast_convert_type(w & jnp.uint32(0xFFFF0000), F32)
    xs = jnp.concatenate([lo, hi], axis=1).astype(BF16)
    hg = _dot(xs, wg_b[...])
    hu = _dot(xs, wu_b[...])
    o_ref[...] = _dot((_silu(hg) * hu).astype(BF16), wd_b[...])


def _moe_call(l, tile_expert, xs, weg, weu, wed):
    R = xs.shape[0]
    D, dexp = weg.shape[2], weg.shape[3]
    tm = MOE_TILE
    return pl.pallas_call(
        _moe_kernel,
        grid_spec=pltpu.PrefetchScalarGridSpec(
            num_scalar_prefetch=1,
            grid=(R // tm,),
            in_specs=[pl.BlockSpec((tm, D // 2), lambda i, te: (i, 0)),
                      pl.BlockSpec((1, 1, D, dexp), lambda i, te: (l, te[i], 0, 0)),
                      pl.BlockSpec((1, 1, D, dexp), lambda i, te: (l, te[i], 0, 0)),
                      pl.BlockSpec((1, 1, dexp, D), lambda i, te: (l, te[i], 0, 0))],
            out_specs=pl.BlockSpec((tm, D), lambda i, te: (i, 0)),
            scratch_shapes=[pltpu.VMEM((D, dexp), BF16), pltpu.VMEM((D, dexp), BF16), pltpu.VMEM((dexp, D), BF16)]),
        out_shape=jax.ShapeDtypeStruct((R, D), F32),
        compiler_params=_cparams(1),
        name="moe",
    )(tile_expert, xs, weg, weu, wed)


def _ln2_kernel(x1_ref, ada_ref, y0_ref, y1_ref, wts_ref, lng_ref, lnb_ref, o_ref, *, alpha):
    ada = ada_ref[0]
    wts = wts_ref[0]
    moe = y0_ref[0] * wts[:, 0:1] + y1_ref[0] * wts[:, 1:2]
    o_ref[0] = _layer_norm(alpha * x1_ref[0] + ada[5:6] * moe, lng_ref[0, 1:2], lnb_ref[0, 1:2])


def _ln2_call(l, x1, ada, y0, y1, wts, lng, lnb, tm, alpha):
    B, T, D = x1.shape
    bmap = lambda b, t: (b, 0, 0)
    lmap = lambda b, t: (l, 0, 0)
    tmap = lambda b, t: (b, t, 0)
    return pl.pallas_call(
        functools.partial(_ln2_kernel, alpha=alpha),
        grid=(B, T // tm),
        in_specs=[pl.BlockSpec((1, tm, D), tmap),
                  pl.BlockSpec((1, 6, D), bmap),
                  pl.BlockSpec((1, tm, D), tmap),
                  pl.BlockSpec((1, tm, D), tmap),
                  pl.BlockSpec((1, tm, LANES), tmap),
                  pl.BlockSpec((1, 2, D), lmap),
                  pl.BlockSpec((1, 2, D), lmap)],
        out_specs=pl.BlockSpec((1, tm, D), tmap),
        out_shape=jax.ShapeDtypeStruct((B, T, D), F32),
        compiler_params=_cparams(2),
        name="ln2",
    )(x1, ada, y0, y1, wts, lng, lnb)


def _take_rows(a, idx):
    return a.at[idx].get(mode="promise_in_bounds")


def _lookup(table, idx):
    sel = idx[:, None] == jnp.arange(table.shape[0], dtype=idx.dtype)[None, :]
    return jnp.sum(jnp.where(sel, table[None, :], 0), axis=1)


def _route(eid, n_tok):
    tm = MOE_TILE
    flat = eid.reshape(-1)
    n_asg = flat.shape[0]
    n_rows = (-(-n_asg // tm) + N_EXPERTS) * tm
    order = jnp.argsort(flat).astype(jnp.int32)
    inv = jnp.argsort(order).astype(jnp.int32)
    experts = jnp.arange(N_EXPERTS, dtype=jnp.int32)
    sizes = jnp.sum((flat[:, None] == experts[None, :]).astype(jnp.int32), axis=0)
    off = jnp.cumsum(sizes) - sizes
    psz = ((sizes + tm - 1) // tm) * tm
    pend = jnp.cumsum(psz)
    poff = pend - psz
    dest = _lookup(poff - off, flat) + inv
    tile_start = jnp.arange(n_rows // tm, dtype=jnp.int32) * tm
    tile_expert = jnp.minimum(jnp.sum((pend[None, :] <= tile_start[:, None]).astype(jnp.int32), axis=1),
                              N_EXPERTS - 1)
    within = jnp.arange(n_rows, dtype=jnp.int32) - jnp.repeat(poff[tile_expert], tm)
    valid = within < jnp.repeat(sizes[tile_expert], tm)
    pos = jnp.clip(jnp.repeat(off[tile_expert], tm) + within, 0, n_asg - 1)
    row_tok = jnp.where(valid, _take_rows(order, pos) // TOP_K, 0)
    return row_tok, dest.reshape(n_tok, TOP_K), tile_expert


def _pad_lanes(a, width=LANES):
    return jnp.pad(a, [(0, 0)] * (a.ndim - 1) + [(0, width - a.shape[-1])])


def _time_block(T):
    return min(T, 256)


def _merge_rows(T):
    return min(T, 512)


def _trunk(x, ada_all, states, P, depth, alpha):
    B, T, D = x.shape
    Tt = _time_block(T)
    c_m, n_m, m_m, cv_m, h_s, cv_s, s_g = states
    new = [[] for _ in range(7)]
    for l in range(depth):
        ada = ada_all[l].reshape(B, 6, D)
        c0t = jnp.swapaxes(c_m[l], -1, -2)
        m0 = _pad_lanes(m_m[l])[:, None, :]
        h0t = (h_s[l].reshape(B, G_S, HPG_S, P_S, N_S).transpose(0, 1, 4, 2, 3)
               .reshape(B, G_S, N_S, HPG_S * P_S))
        hm, c_t, n_n, m_n, cvm_n = _mlstm_call(l, x, ada, P["w_mlstm"], P["mlstm_conv_w"], P["mlstm_conv_b"],
                                               P["mlstm_bif"], P["mlstm_norm_g"], c0t, n_m[l], m0, cv_m[l], Tt)
        ys, h_t, cvs_n = _ssd_call(l, x, ada, P["w_ssd"], P["ssd_conv_w"], P["ssd_conv_b"], P["ssd_dtb"],
                                   P["ssd_alog"], P["ssd_dfull"], P["ssd_norm_g"], h0t, cv_s[l], Tt)
        og, s_n = _gla_call(l, x, ada, P["w_gla"], P["gla_w_lr"], P["gla_b_lr"], P["gla_norm_g"], s_g[l], Tt)
        x1, u2, eid, wts = _merge_call(l, x, ada, hm, ys, og, P["w_gate"], P["b_branch"], P["w_branch"],
                                       P["w_out"], P["ln_g"], P["ln_b"], P["w_rt"], P["b_rt"], _merge_rows(T), alpha)
        row_tok, dest, tile_expert = _route(eid[:, :, :TOP_K], B * T)
        xs = _take_rows(u2.reshape(B * T, D // 2), row_tok)
        ye = _moe_call(l, tile_expert, xs, P["w_e_gate"], P["w_e_up"], P["w_e_down"])
        y0 = _take_rows(ye, dest[:, 0]).reshape(B, T, D)
        y1 = _take_rows(ye, dest[:, 1]).reshape(B, T, D)
        x = _ln2_call(l, x1, ada, y0, y1, wts, P["ln_g"], P["ln_b"], Tt, alpha)

        new[0].append(jnp.swapaxes(c_t, -1, -2))
        new[1].append(n_n)
        new[2].append(m_n[:, 0, :NH_M])
        new[3].append(cvm_n)
        new[4].append(h_t.reshape(B, G_S, N_S, HPG_S, P_S).transpose(0, 1, 3, 4, 2).reshape(B, NH_S, P_S, N_S))
        new[5].append(cvs_n)
        new[6].append(s_n)
    return x, tuple(jnp.stack(lst) for lst in new)


def kernel(x_prompt, x_sample, state_mlstm_c, state_mlstm_n, state_mlstm_m, state_mlstm_conv, state_ssd, state_ssd_conv, state_gla, c_prompt, c_sample, w_ada, b_ada, w_in, mlstm_b_i, mlstm_b_f, mlstm_conv_w, mlstm_conv_b, mlstm_norm_g, ssd_conv_w, ssd_conv_b, ssd_dt_bias, ssd_a_log, ssd_d, ssd_norm_g, gla_w_lr, gla_b_lr, gla_norm_g, b_branch, w_branch, w_out, ln_g, ln_b, w_grp, b_grp, w_router, b_router, w_e_gate, w_e_up, w_e_down):
    depth, D, _ = w_in.shape
    alpha = (2 * depth) ** 0.25
    nbp = x_prompt.shape[0]

    edges = np.concatenate([[0], np.cumsum(COL_SIZES)])
    col = {n: w_in[:, :, int(edges[i]):int(edges[i + 1])] for i, n in enumerate(COL_NAMES)}
    w_gate = w_in[:, :, int(edges[-1]):].astype(BF16)
    cat = lambda parts: jnp.concatenate(parts, axis=-1).astype(BF16)
    row = lambda a: a[:, None, :]
    P = {
        "w_mlstm": cat([col["qk_m"], col["v_m"], col["o_m"], _pad_lanes(jnp.concatenate([col["i_m"], col["f_m"]], -1))]),
        "w_ssd": cat([col["z_s"], col["xbc_s"], _pad_lanes(col["dt_s"])]),
        "w_gla": cat([col["q_g"], col["k_g"], col["v_g"], col["g_g"], _pad_lanes(col["lr_g"])]),
        "w_gate": w_gate,
        "mlstm_conv_w": mlstm_conv_w, "mlstm_conv_b": row(mlstm_conv_b),
        "mlstm_bif": row(_pad_lanes(jnp.concatenate([mlstm_b_i, mlstm_b_f], -1))),
        "mlstm_norm_g": row(mlstm_norm_g),
        "ssd_conv_w": ssd_conv_w, "ssd_conv_b": row(ssd_conv_b),
        "ssd_dtb": row(_pad_lanes(ssd_dt_bias)), "ssd_alog": row(_pad_lanes(ssd_a_log)),
        "ssd_dfull": row(jnp.repeat(ssd_d, P_S, axis=-1)), "ssd_norm_g": row(ssd_norm_g),
        "gla_w_lr": jnp.pad(gla_w_lr, ((0, 0), (0, LANES - R_G), (0, 0))).astype(BF16),
        "gla_b_lr": row(gla_b_lr), "gla_norm_g": row(gla_norm_g),
        "b_branch": row(b_branch), "w_branch": w_branch.astype(BF16), "w_out": w_out.astype(BF16),
        "ln_g": ln_g, "ln_b": ln_b,
        "w_rt": _pad_lanes(jnp.concatenate([w_router, w_grp], -1)).astype(BF16),
        "b_rt": row(_pad_lanes(jnp.concatenate([b_router, b_grp], -1))),
        "w_e_gate": w_e_gate, "w_e_up": w_e_up, "w_e_down": w_e_down,
    }

    ada_all = _ada_call(jnp.concatenate([c_prompt, c_sample], axis=0), w_ada, b_ada)

    sample_states = (state_mlstm_c, state_mlstm_n, state_mlstm_m, state_mlstm_conv,
                     state_ssd, state_ssd_conv, state_gla)
    prompt_states = tuple(jnp.zeros((s.shape[0], nbp) + s.shape[2:], x_prompt.dtype) for s in sample_states)

    y_prompt, new_p = _trunk(x_prompt, ada_all[:, :nbp], prompt_states, P, depth, alpha)
    y_sample, new_s = _trunk(x_sample, ada_all[:, nbp:], sample_states, P, depth, alpha)
    return (y_prompt, y_sample) + new_p + new_s
```

```python
import functools

import jax
import jax.numpy as jnp
import numpy as np
from jax import lax
from jax.experimental import pallas as pl
from jax.experimental.pallas import tpu as pltpu

F32 = jnp.float32
BF16 = jnp.bfloat16
NEG_INF = float("-inf")

CHUNK = 64
CONV_W = 4
NH_M, DQK_M, DV_M = 4, 128, 256
QK_M, VM = NH_M * DQK_M, NH_M * DV_M
NH_S, P_S, N_S, G_S = 16, 64, 128, 4
HPG_S = NH_S // G_S
XS, BC_S = NH_S * P_S, G_S * N_S
CONV_S_DIM = XS + 2 * BC_S
NH_G, DK_G, DV_G = 4, 128, 256
QK_G, VG = NH_G * DK_G, NH_G * DV_G
R_G = 16
TAU_G = 16.0
N_GROUPS, EXP_PER_GROUP, TOP_K = 4, 8, 2
N_EXPERTS = N_GROUPS * EXP_PER_GROUP
LN_EPS = 1e-5
LANES = 128

COL_SIZES = (2 * QK_M, VM, VM, NH_M, NH_M, XS, CONV_S_DIM, NH_S, QK_G, QK_G, VG, VG, R_G)
COL_NAMES = ("qk_m", "v_m", "o_m", "i_m", "f_m", "z_s", "xbc_s", "dt_s", "q_g", "k_g", "v_g", "g_g", "lr_g")

VMEM_LIMIT = 56 * 1024 * 1024
MOE_TILE = 256
PROMPT_SPLIT = 2


def _cparams(n_axes):
    return pltpu.CompilerParams(dimension_semantics=("arbitrary",) * n_axes, vmem_limit_bytes=VMEM_LIMIT)


def _dot(a, b):
    return jnp.dot(a, b, preferred_element_type=F32)


def _dot_nt(a, b):
    return lax.dot_general(a, b, (((1,), (1,)), ((), ())), preferred_element_type=F32)


def _split3(x):
    hi = x.astype(BF16)
    r = x - hi.astype(F32)
    mid = r.astype(BF16)
    lo = (r - mid.astype(F32)).astype(BF16)
    return hi, mid, lo


def _eye(n, m):
    r = lax.broadcasted_iota(jnp.int32, (n, m), 0)
    c = lax.broadcasted_iota(jnp.int32, (n, m), 1)
    return jnp.where(r == c, 1.0, 0.0).astype(BF16)


def _transpose_rows(x, n):
    e = _eye(n, x.shape[1])
    hi, mid, lo = _split3(x)
    return _dot_nt(e, lo) + _dot_nt(e, mid) + _dot_nt(e, hi)


def _block_tril(Tt, L):
    r = lax.broadcasted_iota(jnp.int32, (Tt, Tt), 0)
    c = lax.broadcasted_iota(jnp.int32, (Tt, Tt), 1)
    return jnp.where(((r // L) == (c // L)) & (c <= r), 1.0, 0.0).astype(BF16)


def _block_full(Tt, L):
    r = lax.broadcasted_iota(jnp.int32, (Tt, Tt), 0)
    c = lax.broadcasted_iota(jnp.int32, (Tt, Tt), 1)
    return jnp.where((r // L) == (c // L), 1.0, 0.0).astype(BF16)


def _chunk_last(x, L):
    n = x.shape[0] // L
    return jnp.concatenate([jnp.broadcast_to(x[(c + 1) * L - 1:(c + 1) * L, :], (L, x.shape[1])) for c in range(n)], axis=0)


def _transpose_f32(x):
    if x.shape[0] % LANES == 0:
        return x.T
    return _transpose_rows(x, x.shape[1])


def _log_sigmoid(x):
    return jnp.minimum(x, 0.0) - jnp.log1p(jnp.exp(-jnp.abs(x)))


def _softplus(x):
    return jnp.maximum(x, 0.0) + jnp.log1p(jnp.exp(-jnp.abs(x)))


def _silu(x):
    return x * jax.nn.sigmoid(x)


def _modulate(x, ada, shift_row, scale_row):
    return x * (1.0 + ada[scale_row:scale_row + 1]) + ada[shift_row:shift_row + 1]


def _causal_conv(buf, x, w_ref, b_ref, Tt):
    buf[8:8 + Tt, :] = x
    y = buf[5:5 + Tt, :] * w_ref[0, 0:1, :]
    y = y + buf[6:6 + Tt, :] * w_ref[0, 1:2, :]
    y = y + buf[7:7 + Tt, :] * w_ref[0, 2:3, :]
    y = y + x * w_ref[0, 3:4, :]
    y = y + b_ref[0]
    buf[5:8, :] = buf[5 + Tt:8 + Tt, :]
    return y


def _group_norm(x, n_groups, center):
    w = x.shape[1] // n_groups
    outs = []
    for g in range(n_groups):
        xg = x[:, g * w:(g + 1) * w]
        if center:
            xg = xg - jnp.mean(xg, axis=1, keepdims=True)
        outs.append(xg * lax.rsqrt(jnp.mean(xg * xg, axis=1, keepdims=True) + LN_EPS))
    return jnp.concatenate(outs, axis=1)


def _ada_kernel(c_ref, w_ref, b_ref, o_ref):
    c = _silu(c_ref[...]).astype(BF16)
    o_ref[0] = _dot(c, w_ref[0].astype(BF16)) + b_ref[0]


def _ada_call(c_all, w_ada, b_ada):
    depth, d, n6 = w_ada.shape
    nb = c_all.shape[0]
    tn = 1536
    return pl.pallas_call(
        _ada_kernel,
        grid=(depth, n6 // tn),
        in_specs=[pl.BlockSpec((nb, d), lambda l, j: (0, 0)),
                  pl.BlockSpec((1, d, tn), lambda l, j: (l, 0, j)),
                  pl.BlockSpec((1, 1, tn), lambda l, j: (l, 0, j))],
        out_specs=pl.BlockSpec((1, nb, tn), lambda l, j: (l, 0, j)),
        out_shape=jax.ShapeDtypeStruct((depth, nb, n6), F32),
        compiler_params=_cparams(2),
        name="ada",
    )(c_all, w_ada, b_ada.reshape(depth, 1, n6))


def _mlstm_kernel(x_ref, ada_ref, w_ref, cw_ref, cb_ref, bif_ref, ng_ref, c0_ref, n0_ref, m0_ref, cv0_ref,
                  out_ref, c_out, n_out, m_out, cv_out,
                  conv_s, q_s, qb_s, kb_s, v_s, o_s, kwt_s, un_s, nu_s, p_s, rs_s, b_s, ml_s, n_s, m_s, *ct_s, Tt, L):
    t = pl.program_id(1)
    nc = Tt // L

    @pl.when(t == 0)
    def _():
        for h in range(NH_M):
            ct_s[h][...] = c0_ref[0, h]
        n_s[...] = n0_ref[0]
        m_s[...] = m0_ref[0]
        conv_s[5:8, :] = cv0_ref[0]

    u = _modulate(x_ref[0], ada_ref[0], 0, 1).astype(BF16)
    g = _dot(u, w_ref[0, :, 2 * QK_M + 2 * VM:]) + bif_ref[0]
    qk = _dot(u, w_ref[0, :, 0:2 * QK_M])
    lane = lax.broadcasted_iota(jnp.int32, g.shape, 1)
    g = jnp.where(lane < NH_M, g, _log_sigmoid(g))

    tril = _block_tril(Tt, L)
    hi, mid, lo = _split3(g)
    cs = _dot(tril, lo) + _dot(tril, mid) + _dot(tril, hi)
    v_s[...] = _dot(u, w_ref[0, :, 2 * QK_M:2 * QK_M + VM]).astype(BF16)
    qk = _silu(_causal_conv(conv_s, qk, cw_ref, cb_ref, Tt))
    q = qk[:, :QK_M]
    k = qk[:, QK_M:] * DQK_M ** -0.5
    q_s[...] = q
    qb_s[...] = q.astype(BF16)
    kb_s[...] = k.astype(BF16)
    o_s[...] = _dot(u, w_ref[0, :, 2 * QK_M + VM:2 * QK_M + 2 * VM])
    b0 = pltpu.roll(cs, LANES - NH_M, axis=1)
    r = g - b0
    row = lax.broadcasted_iota(jnp.int32, g.shape, 0) % L
    a = r
    sh = 1
    while sh < L:
        a = jnp.maximum(a, jnp.where(row >= sh, pltpu.roll(a, sh, axis=0), NEG_INF))
        sh *= 2
    al = a
    sh = 1
    while sh < L:
        al = jnp.maximum(al, jnp.where(row + sh < L, pltpu.roll(al, Tt - sh, axis=0), NEG_INF))
        sh *= 2
    b_s[...] = b0
    ml_s[...] = b0 + a
    wsrc = jnp.exp(r - al)
    r_rows = _transpose_f32(r)[0:8, :]
    col = lax.broadcasted_iota(jnp.int32, (DQK_M, Tt), 1)
    rr = lax.broadcasted_iota(jnp.int32, (L, L), 0)
    cc = lax.broadcasted_iota(jnp.int32, (L, L), 1)
    causal = cc <= rr
    for h in range(NH_M):
        ks = slice(h * DQK_M, (h + 1) * DQK_M)
        kw = wsrc[:, h:h + 1] * k[:, ks]
        kwt_s[h] = _transpose_f32(kw).astype(BF16)
        for c in range(nc):
            nu_s[c * NH_M + h:c * NH_M + h + 1, :] = jnp.sum(kw[c * L:(c + 1) * L], axis=0, keepdims=True)
    for c in range(nc):
        for h in range(NH_M):
            vs = slice(h * DV_M, (h + 1) * DV_M)
            kwm = jnp.where((col // L) == c, kwt_s[h], jnp.zeros((), BF16)) if nc > 1 else kwt_s[h]
            un_s[c * NH_M + h] = _dot(kwm, v_s[:, vs])
    for c in range(nc):
        rows = slice(c * L, (c + 1) * L)
        sl = []
        for h in range(NH_M):
            ks = slice(h * DQK_M, (h + 1) * DQK_M)
            dloc = jnp.exp(jnp.where(causal, r_rows[h:h + 1, rows] - a[rows, h:h + 1], NEG_INF))
            sl.append(_dot_nt(qb_s[rows, ks], kb_s[rows, ks]) * dloc)
        for h in range(NH_M):
            vs = slice(h * DV_M, (h + 1) * DV_M)
            p_s[rows, vs] = _dot(sl[h].astype(BF16), v_s[rows, vs])
            rs_s[rows, h:h + 1] = jnp.sum(sl[h], axis=1, keepdims=True)

    lane_l = lax.broadcasted_iota(jnp.int32, (L, LANES), 1)
    for c in range(nc):
        rows = slice(c * L, (c + 1) * L)
        m_prev = m_s[...]
        bc = b_s[rows, :]
        mlc = ml_s[rows, :]
        mt = jnp.maximum(bc + m_prev, mlc)
        corr = jnp.exp(mlc - mt)
        w_inter = jnp.exp(bc + m_prev - mt)
        inter = []
        qn = jnp.zeros((L, LANES), F32)
        for h in range(NH_M):
            ks = slice(h * DQK_M, (h + 1) * DQK_M)
            inter.append(_dot(qb_s[rows, ks], ct_s[h][...].astype(BF16)))
            qn_h = jnp.sum(q_s[rows, ks] * n_s[h:h + 1, :], axis=1, keepdims=True)
            qn = jnp.where(lane_l == h, qn_h, qn)
        den = corr * rs_s[rows, :] + w_inter * qn
        inv = 1.0 / jnp.maximum(jnp.abs(den), jnp.exp(-mt))
        ca = corr * inv
        cb2 = w_inter * inv
        for h in range(NH_M):
            vs = slice(h * DV_M, (h + 1) * DV_M)
            p_s[rows, vs] = ca[:, h:h + 1] * p_s[rows, vs] + cb2[:, h:h + 1] * inter[h]
        m_last = mt[L - 1:L, :]
        c_l = jnp.exp(mlc[L - 1:L, :] - m_last)
        decay = jnp.exp(bc[L - 1:L, :] + m_prev - m_last)
        for h in range(NH_M):
            ct_s[h][...] = decay[:, h:h + 1] * ct_s[h][...] + c_l[:, h:h + 1] * un_s[c * NH_M + h]
            n_s[h:h + 1, :] = decay[:, h:h + 1] * n_s[h:h + 1, :] + c_l[:, h:h + 1] * nu_s[c * NH_M + h:c * NH_M + h + 1, :]
        m_s[...] = m_last

    hn = _group_norm(p_s[...], NH_M, True) * ng_ref[0] * jax.nn.sigmoid(o_s[...])
    out_ref[0] = hn.astype(out_ref.dtype)

    @pl.when(t == pl.num_programs(1) - 1)
    def _():
        for h in range(NH_M):
            c_out[0, h] = ct_s[h][...]
        n_out[0] = n_s[...]
        m_out[0] = m_s[...]
        cv_out[0] = conv_s[5:8, :]


def _mlstm_call(l, x, ada, w, cw, cb, bif, ng, c0t, n0, m0, cv0, Tt):
    B, T, D = x.shape
    L = min(CHUNK, T)
    wcols = w.shape[2]
    bmap = lambda b, t: (b, 0, 0)
    lmap = lambda b, t: (l, 0, 0)
    return pl.pallas_call(
        functools.partial(_mlstm_kernel, Tt=Tt, L=L),
        grid=(B, T // Tt),
        in_specs=[pl.BlockSpec((1, Tt, D), lambda b, t: (b, t, 0)),
                  pl.BlockSpec((1, 6, D), bmap),
                  pl.BlockSpec((1, D, wcols), lmap),
                  pl.BlockSpec((1, CONV_W, 2 * QK_M), lmap),
                  pl.BlockSpec((1, 1, 2 * QK_M), lmap),
                  pl.BlockSpec((1, 1, LANES), lmap),
                  pl.BlockSpec((1, 1, VM), lmap),
                  pl.BlockSpec((1, NH_M, DQK_M, DV_M), lambda b, t: (b, 0, 0, 0)),
                  pl.BlockSpec((1, NH_M, DQK_M), bmap),
                  pl.BlockSpec((1, 1, LANES), bmap),
                  pl.BlockSpec((1, CONV_W - 1, 2 * QK_M), bmap)],
        out_specs=[pl.BlockSpec((1, Tt, VM), lambda b, t: (b, t, 0)),
                   pl.BlockSpec((1, NH_M, DQK_M, DV_M), lambda b, t: (b, 0, 0, 0)),
                   pl.BlockSpec((1, NH_M, DQK_M), bmap),
                   pl.BlockSpec((1, 1, LANES), bmap),
                   pl.BlockSpec((1, CONV_W - 1, 2 * QK_M), bmap)],
        out_shape=[jax.ShapeDtypeStruct((B, T, VM), BF16),
                   jax.ShapeDtypeStruct((B, NH_M, DQK_M, DV_M), F32),
                   jax.ShapeDtypeStruct((B, NH_M, DQK_M), F32),
                   jax.ShapeDtypeStruct((B, 1, LANES), F32),
                   jax.ShapeDtypeStruct((B, CONV_W - 1, 2 * QK_M), F32)],
        scratch_shapes=[pltpu.VMEM((8 + Tt, 2 * QK_M), F32),
                        pltpu.VMEM((Tt, QK_M), F32),
                        pltpu.VMEM((Tt, QK_M), BF16),
                        pltpu.VMEM((Tt, QK_M), BF16),
                        pltpu.VMEM((Tt, VM), BF16),
                        pltpu.VMEM((Tt, VM), F32),
                        pltpu.VMEM((NH_M, DQK_M, Tt), BF16),
                        pltpu.VMEM((Tt // L * NH_M, DQK_M, DV_M), F32),
                        pltpu.VMEM((Tt // L * NH_M, DQK_M), F32),
                        pltpu.VMEM((Tt, VM), F32),
                        pltpu.VMEM((Tt, LANES), F32),
                        pltpu.VMEM((Tt, LANES), F32),
                        pltpu.VMEM((Tt, LANES), F32),
                        pltpu.VMEM((NH_M, DQK_M), F32),
                        pltpu.VMEM((1, LANES), F32)] + [pltpu.VMEM((DQK_M, DV_M), F32)] * NH_M,
        compiler_params=_cparams(2),
        name="mlstm",
    )(x, ada, w, cw, cb, bif, ng, c0t, n0, m0, cv0)


def _pair_cols(x, hd, lane):
    return jnp.where(lane < P_S, x[:, hd:hd + 1], x[:, hd + 1:hd + 2])


def _ssd_kernel(x_ref, ada_ref, w_ref, cw_ref, cb_ref, dtb_ref, alog_ref, d_ref, ng_ref, h0_ref, cv0_ref,
                out_ref, h_out, cv_out,
                conv_s, zg_s, x_s, xw_s, b_s, c_s, bt_s, e2_s, un_s, dec_s, y_s, *ht_s, Tt, L):
    t = pl.program_id(1)
    nc = Tt // L
    n_pairs = NH_S // 2

    @pl.when(t == 0)
    def _():
        for g in range(G_S):
            ht_s[g][...] = h0_ref[0, g]
        conv_s[5:8, :] = cv0_ref[0]

    u = _modulate(x_ref[0], ada_ref[0], 0, 1).astype(BF16)
    dt = _softplus(_dot(u, w_ref[0, :, XS + CONV_S_DIM:]) + dtb_ref[0])
    xbc = _dot(u, w_ref[0, :, XS:XS + CONV_S_DIM])
    a_row = -jnp.exp(alog_ref[0])

    tril = _block_tril(Tt, L)
    full = _block_full(Tt, L)
    hi, mid, lo = _split3(dt * a_row)
    cs = _dot(tril, lo) + _dot(tril, mid) + _dot(tril, hi)
    csl = _dot(full, lo) + _dot(full, mid) + _dot(full, hi)
    xbc = _silu(_causal_conv(conv_s, xbc, cw_ref, cb_ref, Tt))
    x = xbc[:, :XS]
    bm = xbc[:, XS:XS + BC_S]
    x_s[...] = x
    b_s[...] = bm.astype(BF16)
    c_s[...] = xbc[:, XS + BC_S:].astype(BF16)
    wsrc = jnp.exp(csl - cs) * dt
    dec = jnp.exp(csl)
    cs_rows = _transpose_f32(cs)[0:NH_S, :]
    dt_rows = _transpose_f32(dt)[0:NH_S, :]
    for g in range(G_S):
        bt_s[g] = _transpose_f32(bm[:, g * N_S:(g + 1) * N_S]).astype(BF16)
    zg_s[...] = _silu(_dot(u, w_ref[0, :, 0:XS]))
    lane_t = lax.broadcasted_iota(jnp.int32, (Tt, LANES), 1)
    for pp in range(n_pairs):
        pc = slice(pp * LANES, (pp + 1) * LANES)
        e2_s[:, pc] = _pair_cols(cs, 2 * pp, lane_t)
        xw_s[:, pc] = (x[:, pc] * _pair_cols(wsrc, 2 * pp, lane_t)).astype(BF16)
    lane1 = lax.broadcasted_iota(jnp.int32, (1, LANES), 1)
    for c in range(nc):
        for pp in range(n_pairs):
            r0 = c * L
            dec_s[c * n_pairs + pp:c * n_pairs + pp + 1, :] = _pair_cols(dec[r0:r0 + 1, :], 2 * pp, lane1)

    rr = lax.broadcasted_iota(jnp.int32, (L, LANES), 0)
    lane_l = lax.broadcasted_iota(jnp.int32, (L, LANES), 1)
    causal2 = (lane_l % P_S) <= rr
    col = lax.broadcasted_iota(jnp.int32, (N_S, Tt), 1)
    zero_b = jnp.zeros((), BF16)
    for c in range(nc):
        rows = slice(c * L, (c + 1) * L)
        for g in range(G_S):
            gs = slice(g * N_S, (g + 1) * N_S)
            bg = b_s[rows, gs]
            cb2 = _dot_nt(c_s[rows, gs], jnp.concatenate([bg, bg], axis=0))
            btm = jnp.where((col // L) == c, bt_s[g], zero_b) if nc > 1 else bt_s[g]
            un_s[c * G_S + g] = _dot(btm, xw_s[:, g * 2 * LANES:(g + 1) * 2 * LANES])
            for p2 in range(2):
                pp = g * 2 + p2
                hd = 2 * pp
                pc = slice(pp * LANES, (pp + 1) * LANES)
                csr2 = jnp.concatenate([cs_rows[hd:hd + 1, rows], cs_rows[hd + 1:hd + 2, rows]], axis=1)
                dtr2 = jnp.concatenate([dt_rows[hd:hd + 1, rows], dt_rows[hd + 1:hd + 2, rows]], axis=1)
                seg2 = jnp.exp(jnp.where(causal2, e2_s[rows, pc] - csr2, NEG_INF))
                m2 = (cb2 * seg2 * dtr2).astype(BF16)
                xp = x_s[rows, pc].astype(BF16)
                xbd = jnp.concatenate([jnp.where(lane_l < P_S, xp, zero_b), jnp.where(lane_l >= P_S, xp, zero_b)], axis=0)
                y_s[rows, pc] = _dot(m2, xbd)

    for c in range(nc):
        rows = slice(c * L, (c + 1) * L)
        for g in range(G_S):
            gc = slice(g * 2 * LANES, (g + 1) * 2 * LANES)
            yi = _dot(c_s[rows, g * N_S:(g + 1) * N_S], ht_s[g][...].astype(BF16))
            y_s[rows, gc] += jnp.exp(e2_s[rows, gc]) * yi
        for g in range(G_S):
            for p2 in range(2):
                pp = g * 2 + p2
                hc = slice(p2 * LANES, (p2 + 1) * LANES)
                ht_s[g][:, hc] = (dec_s[c * n_pairs + pp:c * n_pairs + pp + 1, :] * ht_s[g][:, hc]
                                  + un_s[c * G_S + g][:, hc])

    y = (y_s[...] + d_ref[0] * x_s[...]) * zg_s[...]
    out_ref[0] = (_group_norm(y, G_S, False) * ng_ref[0]).astype(out_ref.dtype)

    @pl.when(t == pl.num_programs(1) - 1)
    def _():
        for g in range(G_S):
            h_out[0, g] = ht_s[g][...]
        cv_out[0] = conv_s[5:8, :]


def _ssd_call(l, x, ada, w, cw, cb, dtb, alog, dfull, ng, h0t, cv0, Tt):
    B, T, D = x.shape
    L = min(CHUNK, T)
    wcols = w.shape[2]
    bmap = lambda b, t: (b, 0, 0)
    lmap = lambda b, t: (l, 0, 0)
    hshape = (G_S, N_S, HPG_S * P_S)
    assert L == P_S and 2 * L == LANES and T % Tt == 0 and Tt % L == 0
    return pl.pallas_call(
        functools.partial(_ssd_kernel, Tt=Tt, L=L),
        grid=(B, T // Tt),
        in_specs=[pl.BlockSpec((1, Tt, D), lambda b, t: (b, t, 0)),
                  pl.BlockSpec((1, 6, D), bmap),
                  pl.BlockSpec((1, D, wcols), lmap),
                  pl.BlockSpec((1, CONV_W, CONV_S_DIM), lmap),
                  pl.BlockSpec((1, 1, CONV_S_DIM), lmap),
                  pl.BlockSpec((1, 1, LANES), lmap),
                  pl.BlockSpec((1, 1, LANES), lmap),
                  pl.BlockSpec((1, 1, XS), lmap),
                  pl.BlockSpec((1, 1, XS), lmap),
                  pl.BlockSpec((1,) + hshape, lambda b, t: (b, 0, 0, 0)),
                  pl.BlockSpec((1, CONV_W - 1, CONV_S_DIM), bmap)],
        out_specs=[pl.BlockSpec((1, Tt, XS), lambda b, t: (b, t, 0)),
                   pl.BlockSpec((1,) + hshape, lambda b, t: (b, 0, 0, 0)),
                   pl.BlockSpec((1, CONV_W - 1, CONV_S_DIM), bmap)],
        out_shape=[jax.ShapeDtypeStruct((B, T, XS), BF16),
                   jax.ShapeDtypeStruct((B,) + hshape, F32),
                   jax.ShapeDtypeStruct((B, CONV_W - 1, CONV_S_DIM), F32)],
        scratch_shapes=[pltpu.VMEM((8 + Tt, CONV_S_DIM), F32),
                        pltpu.VMEM((Tt, XS), F32),
                        pltpu.VMEM((Tt, XS), F32),
                        pltpu.VMEM((Tt, XS), BF16),
                        pltpu.VMEM((Tt, BC_S), BF16),
                        pltpu.VMEM((Tt, BC_S), BF16),
                        pltpu.VMEM((G_S, N_S, Tt), BF16),
                        pltpu.VMEM((Tt, XS), F32),
                        pltpu.VMEM((Tt // L * G_S, N_S, HPG_S * P_S), F32),
                        pltpu.VMEM((Tt // L * (NH_S // 2), LANES), F32),
                        pltpu.VMEM((Tt, XS), F32)] + [pltpu.VMEM(hshape[1:], F32)] * G_S,
        compiler_params=_cparams(2),
        name="ssd",
    )(x, ada, w, cw, cb, dtb, alog, dfull, ng, h0t, cv0)


def _gla_kernel(x_ref, ada_ref, w_ref, wlr_ref, blr_ref, ng_ref, s0_ref,
                out_ref, s_out,
                sg_s, qg_s, kg_s, v_s, kdt_s, egt_s, un_s, o_s, *st_s, Tt, L):
    t = pl.program_id(1)
    nc = Tt // L

    @pl.when(t == 0)
    def _():
        for h in range(NH_G):
            st_s[h][...] = s0_ref[0, h]

    u = _modulate(x_ref[0], ada_ref[0], 0, 1).astype(BF16)
    lr = _dot(u, w_ref[0, :, 2 * QK_G + 2 * VG:]).astype(BF16)
    lg = _log_sigmoid(_dot(lr, wlr_ref[0]) + blr_ref[0]) / TAU_G
    q = _dot(u, w_ref[0, :, 0:QK_G])
    k = _dot(u, w_ref[0, :, QK_G:2 * QK_G]) * DK_G ** -0.5

    tril = _block_tril(Tt, L)
    hi, mid, lo = _split3(lg)
    G = _dot(tril, lo) + _dot(tril, mid) + _dot(tril, hi)
    v_s[...] = _dot(u, w_ref[0, :, 2 * QK_G:2 * QK_G + VG]).astype(BF16)
    if nc > 1:
        Gl = _chunk_last(G, L)
    else:
        full = _block_full(Tt, L)
        Gl = _dot(full, lo) + _dot(full, mid) + _dot(full, hi)
    qg_s[...] = (q * jnp.exp(G)).astype(BF16)
    kg_s[...] = (k * jnp.exp(-G)).astype(BF16)
    kd = k * jnp.exp(Gl - G)
    eg = jnp.exp(Gl)
    sg_s[...] = _silu(_dot(u, w_ref[0, :, 2 * QK_G + VG:2 * QK_G + 2 * VG]))
    for h in range(NH_G):
        ks = slice(h * DK_G, (h + 1) * DK_G)
        kdt_s[h] = _transpose_f32(kd[:, ks]).astype(BF16)
        egt_s[h] = _transpose_f32(eg[:, ks])
    col = lax.broadcasted_iota(jnp.int32, (DK_G, Tt), 1)
    for c in range(nc):
        for h in range(NH_G):
            vs = slice(h * DV_G, (h + 1) * DV_G)
            kdm = jnp.where((col // L) == c, kdt_s[h], jnp.zeros((), BF16)) if nc > 1 else kdt_s[h]
            un_s[c * NH_G + h] = _dot(kdm, v_s[:, vs])

    rr = lax.broadcasted_iota(jnp.int32, (L, L), 0)
    cc = lax.broadcasted_iota(jnp.int32, (L, L), 1)
    causal = cc <= rr
    for c in range(nc):
        rows = slice(c * L, (c + 1) * L)
        att = []
        for h in range(NH_G):
            ks = slice(h * DK_G, (h + 1) * DK_G)
            att.append(_dot_nt(qg_s[rows, ks], kg_s[rows, ks]))
        for h in range(NH_G):
            ks = slice(h * DK_G, (h + 1) * DK_G)
            vs = slice(h * DV_G, (h + 1) * DV_G)
            o_s[rows, vs] = _dot(qg_s[rows, ks], st_s[h][...].astype(BF16))
        for h in range(NH_G):
            vs = slice(h * DV_G, (h + 1) * DV_G)
            o_s[rows, vs] += _dot(jnp.where(causal, att[h], 0.0).astype(BF16), v_s[rows, vs])
        for h in range(NH_G):
            st_s[h][...] = egt_s[h][:, c * L:c * L + 1] * st_s[h][...] + un_s[c * NH_G + h]

    out_ref[0] = (_group_norm(o_s[...], NH_G, False) * ng_ref[0] * sg_s[...]).astype(out_ref.dtype)

    @pl.when(t == pl.num_programs(1) - 1)
    def _():
        for h in range(NH_G):
            s_out[0, h] = st_s[h][...]


def _gla_call(l, x, ada, w, wlr, blr, ng, s0, Tt):
    B, T, D = x.shape
    L = min(CHUNK, T)
    wcols = w.shape[2]
    bmap = lambda b, t: (b, 0, 0)
    lmap = lambda b, t: (l, 0, 0)
    sshape = (NH_G, DK_G, DV_G)
    return pl.pallas_call(
        functools.partial(_gla_kernel, Tt=Tt, L=L),
        grid=(B, T // Tt),
        in_specs=[pl.BlockSpec((1, Tt, D), lambda b, t: (b, t, 0)),
                  pl.BlockSpec((1, 6, D), bmap),
                  pl.BlockSpec((1, D, wcols), lmap),
                  pl.BlockSpec((1, LANES, QK_G), lmap),
                  pl.BlockSpec((1, 1, QK_G), lmap),
                  pl.BlockSpec((1, 1, VG), lmap),
                  pl.BlockSpec((1,) + sshape, lambda b, t: (b, 0, 0, 0))],
        out_specs=[pl.BlockSpec((1, Tt, VG), lambda b, t: (b, t, 0)),
                   pl.BlockSpec((1,) + sshape, lambda b, t: (b, 0, 0, 0))],
        out_shape=[jax.ShapeDtypeStruct((B, T, VG), BF16),
                   jax.ShapeDtypeStruct((B,) + sshape, F32)],
        scratch_shapes=[pltpu.VMEM((Tt, VG), F32),
                        pltpu.VMEM((Tt, QK_G), BF16),
                        pltpu.VMEM((Tt, QK_G), BF16),
                        pltpu.VMEM((Tt, VG), BF16),
                        pltpu.VMEM((NH_G, DK_G, Tt), BF16),
                        pltpu.VMEM((NH_G, DK_G, Tt), F32),
                        pltpu.VMEM((Tt // L * NH_G, DK_G, DV_G), F32),
                        pltpu.VMEM((Tt, VG), F32)] + [pltpu.VMEM(sshape[1:], F32)] * NH_G,
        compiler_params=_cparams(2),
        name="gla",
    )(x, ada, w, wlr, blr, ng, s0)


def _layer_norm(x, g, b):
    mu = jnp.mean(x, axis=1, keepdims=True)
    xc = x - mu
    var = jnp.mean(xc * xc, axis=1, keepdims=True)
    return xc * lax.rsqrt(var + LN_EPS) * g + b


def _merge_kernel(x_ref, ada_ref, hm_ref, ys_ref, og_ref, wg_ref, bb_ref, wb_ref, wo_ref, lng_ref, lnb_ref,
                  wrt_ref, brt_ref,
                  x1_ref, u2_ref, eid_ref, wts_ref, *, alpha):
    x = x_ref[0]
    ada = ada_ref[0]
    D = x.shape[1]
    u = _modulate(x, ada, 0, 1).astype(BF16)
    gate = jax.nn.sigmoid(_dot(u, wg_ref[0]) + bb_ref[0])
    merged = (gate[:, 0:D] * _dot(hm_ref[0], wb_ref[0, 0])
              + gate[:, D:2 * D] * _dot(ys_ref[0], wb_ref[0, 1])
              + gate[:, 2 * D:3 * D] * _dot(og_ref[0], wb_ref[0, 2]))
    y = _dot(merged.astype(BF16), wo_ref[0])
    x1 = _layer_norm(alpha * x + ada[2:3] * y, lng_ref[0, 0:1], lnb_ref[0, 0:1])
    x1_ref[0] = x1
    u2 = _modulate(x1, ada, 3, 4).astype(BF16)
    bits = lax.bitcast_convert_type(u2.astype(F32), jnp.uint32)
    u2_ref[0] = (bits[:, :D // 2] >> 16) | bits[:, D // 2:]

    logits = _dot(u2, wrt_ref[0]) + brt_ref[0]
    lane = lax.broadcasted_iota(jnp.int32, logits.shape, 1)
    lane_f = lane.astype(F32)
    big = float(LANES)
    is_g = (lane >= N_EXPERTS) & (lane < N_EXPERTS + N_GROUPS)
    gmax = jnp.max(jnp.where(is_g, logits, NEG_INF), axis=1, keepdims=True)
    gsum = jnp.sum(jnp.where(is_g, jnp.exp(logits - gmax), 0.0), axis=1, keepdims=True)
    pg_top = 1.0 / gsum
    g_lane = jnp.min(jnp.where(is_g & (logits == gmax), lane_f, big), axis=1, keepdims=True)
    g_idx = g_lane.astype(jnp.int32) - N_EXPERTS
    in_grp = (lane < N_EXPERTS) & ((lane // EXP_PER_GROUP) == g_idx)
    el = jnp.where(in_grp, logits, NEG_INF)
    v1 = jnp.max(el, axis=1, keepdims=True)
    i1 = jnp.min(jnp.where(in_grp & (el == v1), lane_f, big), axis=1, keepdims=True)
    rest = in_grp & (lane_f != i1)
    el2 = jnp.where(rest, logits, NEG_INF)
    v2 = jnp.max(el2, axis=1, keepdims=True)
    i2 = jnp.min(jnp.where(rest & (el2 == v2), lane_f, big), axis=1, keepdims=True)
    e = jnp.exp(v2 - v1)
    w1 = pg_top / (1.0 + e)
    w2 = pg_top * e / (1.0 + e)
    eid_ref[0] = jnp.where(lane == 0, i1, jnp.where(lane == 1, i2, 0.0)).astype(jnp.int32)
    wts_ref[0] = jnp.where(lane == 0, w1, jnp.where(lane == 1, w2, 0.0))


def _merge_call(l, x, ada, hm, ys, og, wg, bb, wb, wo, lng, lnb, wrt, brt, tm, alpha):
    B, T, D = x.shape
    bmap = lambda b, t: (b, 0, 0)
    lmap = lambda b, t: (l, 0, 0)
    tmap = lambda b, t: (b, t, 0)
    return pl.pallas_call(
        functools.partial(_merge_kernel, alpha=alpha),
        grid=(B, T // tm),
        in_specs=[pl.BlockSpec((1, tm, D), tmap),
                  pl.BlockSpec((1, 6, D), bmap),
                  pl.BlockSpec((1, tm, D), tmap),
                  pl.BlockSpec((1, tm, D), tmap),
                  pl.BlockSpec((1, tm, D), tmap),
                  pl.BlockSpec((1, D, 3 * D), lmap),
                  pl.BlockSpec((1, 1, 3 * D), lmap),
                  pl.BlockSpec((1, 3, D, D), lambda b, t: (l, 0, 0, 0)),
                  pl.BlockSpec((1, D, D), lmap),
                  pl.BlockSpec((1, 2, D), lmap),
                  pl.BlockSpec((1, 2, D), lmap),
                  pl.BlockSpec((1, D, LANES), lmap),
                  pl.BlockSpec((1, 1, LANES), lmap)],
        out_specs=[pl.BlockSpec((1, tm, D), tmap),
                   pl.BlockSpec((1, tm, D // 2), tmap),
                   pl.BlockSpec((1, tm, LANES), tmap),
                   pl.BlockSpec((1, tm, LANES), tmap)],
        out_shape=[jax.ShapeDtypeStruct((B, T, D), F32),
                   jax.ShapeDtypeStruct((B, T, D // 2), jnp.uint32),
                   jax.ShapeDtypeStruct((B, T, LANES), jnp.int32),
                   jax.ShapeDtypeStruct((B, T, LANES), F32)],
        compiler_params=_cparams(2),
        name="merge",
    )(x, ada, hm, ys, og, wg, bb, wb, wo, lng, lnb, wrt, brt)


def _moe_kernel(te_ref, xs_ref, wg_ref, wu_ref, wd_ref, o_ref, wg_b, wu_b, wd_b):
    i = pl.program_id(0)
    prev = te_ref[jnp.maximum(i - 1, 0)]

    @pl.when((i == 0) | (te_ref[i] != prev))
    def _():
        wg_b[...] = wg_ref[0, 0].astype(BF16)
        wu_b[...] = wu_ref[0, 0].astype(BF16)
        wd_b[...] = wd_ref[0, 0].astype(BF16)

    w = xs_ref[...]
    lo = lax.bitcast_convert_type(w << 16, F32)
    hi = lax.bitcast_convert_type(w & jnp.uint32(0xFFFF0000), F32)
    xs = jnp.concatenate([lo, hi], axis=1).astype(BF16)
    hg = _dot(xs, wg_b[...])
    hu = _dot(xs, wu_b[...])
    o_ref[...] = _dot((_silu(hg) * hu).astype(BF16), wd_b[...])


def _moe_call(l, tile_expert, xs, weg, weu, wed):
    R = xs.shape[0]
    D, dexp = weg.shape[2], weg.shape[3]
    tm = MOE_TILE
    return pl.pallas_call(
        _moe_kernel,
        grid_spec=pltpu.PrefetchScalarGridSpec(
            num_scalar_prefetch=1,
            grid=(R // tm,),
            in_specs=[pl.BlockSpec((tm, D // 2), lambda i, te: (i, 0)),
                      pl.BlockSpec((1, 1, D, dexp), lambda i, te: (l, te[i], 0, 0)),
                      pl.BlockSpec((1, 1, D, dexp), lambda i, te: (l, te[i], 0, 0)),
                      pl.BlockSpec((1, 1, dexp, D), lambda i, te: (l, te[i], 0, 0))],
            out_specs=pl.BlockSpec((tm, D), lambda i, te: (i, 0)),
            scratch_shapes=[pltpu.VMEM((D, dexp), BF16), pltpu.VMEM((D, dexp), BF16), pltpu.VMEM((dexp, D), BF16)]),
        out_shape=jax.ShapeDtypeStruct((R, D), F32),
        compiler_params=_cparams(1),
        name="moe",
    )(tile_expert, xs, weg, weu, wed)


def _ln2_kernel(x1_ref, ada_ref, y0_ref, y1_ref, wts_ref, lng_ref, lnb_ref, o_ref, *, alpha):
    ada = ada_ref[0]
    wts = wts_ref[0]
    moe = y0_ref[0] * wts[:, 0:1] + y1_ref[0] * wts[:, 1:2]
    o_ref[0] = _layer_norm(alpha * x1_ref[0] + ada[5:6] * moe, lng_ref[0, 1:2], lnb_ref[0, 1:2])


def _ln2_call(l, x1, ada, y0, y1, wts, lng, lnb, tm, alpha):
    B, T, D = x1.shape
    bmap = lambda b, t: (b, 0, 0)
    lmap = lambda b, t: (l, 0, 0)
    tmap = lambda b, t: (b, t, 0)
    return pl.pallas_call(
        functools.partial(_ln2_kernel, alpha=alpha),
        grid=(B, T // tm),
        in_specs=[pl.BlockSpec((1, tm, D), tmap),
                  pl.BlockSpec((1, 6, D), bmap),
                  pl.BlockSpec((1, tm, D), tmap),
                  pl.BlockSpec((1, tm, D), tmap),
                  pl.BlockSpec((1, tm, LANES), tmap),
                  pl.BlockSpec((1, 2, D), lmap),
                  pl.BlockSpec((1, 2, D), lmap)],
        out_specs=pl.BlockSpec((1, tm, D), tmap),
        out_shape=jax.ShapeDtypeStruct((B, T, D), F32),
        compiler_params=_cparams(2),
        name="ln2",
    )(x1, ada, y0, y1, wts, lng, lnb)


def _take_rows(a, idx):
    return a.at[idx].get(mode="promise_in_bounds")


def _lookup(table, idx):
    sel = idx[:, None] == jnp.arange(table.shape[0], dtype=idx.dtype)[None, :]
    return jnp.sum(jnp.where(sel, table[None, :], 0), axis=1)


def _route(eid, n_tok):
    tm = MOE_TILE
    flat = eid.reshape(-1)
    n_asg = flat.shape[0]
    n_rows = (-(-n_asg // tm) + N_EXPERTS) * tm
    order = jnp.argsort(flat).astype(jnp.int32)
    inv = jnp.argsort(order).astype(jnp.int32)
    experts = jnp.arange(N_EXPERTS, dtype=jnp.int32)
    sizes = jnp.sum((flat[:, None] == experts[None, :]).astype(jnp.int32), axis=0)
    off = jnp.cumsum(sizes) - sizes
    psz = ((sizes + tm - 1) // tm) * tm
    pend = jnp.cumsum(psz)
    poff = pend - psz
    dest = _lookup(poff - off, flat) + inv
    tile_start = jnp.arange(n_rows // tm, dtype=jnp.int32) * tm
    tile_expert = jnp.minimum(jnp.sum((pend[None, :] <= tile_start[:, None]).astype(jnp.int32), axis=1),
                              N_EXPERTS - 1)
    within = jnp.arange(n_rows, dtype=jnp.int32) - jnp.repeat(poff[tile_expert], tm)
    valid = within < jnp.repeat(sizes[tile_expert], tm)
    pos = jnp.clip(jnp.repeat(off[tile_expert], tm) + within, 0, n_asg - 1)
    row_tok = jnp.where(valid, _take_rows(order, pos) // TOP_K, 0)
    return row_tok, dest.reshape(n_tok, TOP_K), tile_expert


def _pad_lanes(a, width=LANES):
    return jnp.pad(a, [(0, 0)] * (a.ndim - 1) + [(0, width - a.shape[-1])])


def _time_block(T):
    return min(T, 256)


def _merge_rows(T):
    return min(T, 512)


def _trunk(x, ada_all, states, P, depth, alpha):
    B, T, D = x.shape
    Tt = _time_block(T)
    c_m, n_m, m_m, cv_m, h_s, cv_s, s_g = states
    new = [[] for _ in range(7)]
    for l in range(depth):
        ada = ada_all[l].reshape(B, 6, D)
        c0t = jnp.swapaxes(c_m[l], -1, -2)
        m0 = _pad_lanes(m_m[l])[:, None, :]
        h0t = (h_s[l].reshape(B, G_S, HPG_S, P_S, N_S).transpose(0, 1, 4, 2, 3)
               .reshape(B, G_S, N_S, HPG_S * P_S))
        hm, c_t, n_n, m_n, cvm_n = _mlstm_call(l, x, ada, P["w_mlstm"], P["mlstm_conv_w"], P["mlstm_conv_b"],
                                               P["mlstm_bif"], P["mlstm_norm_g"], c0t, n_m[l], m0, cv_m[l], Tt)
        ys, h_t, cvs_n = _ssd_call(l, x, ada, P["w_ssd"], P["ssd_conv_w"], P["ssd_conv_b"], P["ssd_dtb"],
                                   P["ssd_alog"], P["ssd_dfull"], P["ssd_norm_g"], h0t, cv_s[l], Tt)
        og, s_n = _gla_call(l, x, ada, P["w_gla"], P["gla_w_lr"], P["gla_b_lr"], P["gla_norm_g"], s_g[l], Tt)
        x1, u2, eid, wts = _merge_call(l, x, ada, hm, ys, og, P["w_gate"], P["b_branch"], P["w_branch"],
                                       P["w_out"], P["ln_g"], P["ln_b"], P["w_rt"], P["b_rt"], _merge_rows(T), alpha)
        row_tok, dest, tile_expert = _route(eid[:, :, :TOP_K], B * T)
        xs = _take_rows(u2.reshape(B * T, D // 2), row_tok)
        ye = _moe_call(l, tile_expert, xs, P["w_e_gate"], P["w_e_up"], P["w_e_down"])
        y0 = _take_rows(ye, dest[:, 0]).reshape(B, T, D)
        y1 = _take_rows(ye, dest[:, 1]).reshape(B, T, D)
        x = _ln2_call(l, x1, ada, y0, y1, wts, P["ln_g"], P["ln_b"], Tt, alpha)

        new[0].append(jnp.swapaxes(c_t, -1, -2))
        new[1].append(n_n)
        new[2].append(m_n[:, 0, :NH_M])
        new[3].append(cvm_n)
        new[4].append(h_t.reshape(B, G_S, N_S, HPG_S, P_S).transpose(0, 1, 3, 4, 2).reshape(B, NH_S, P_S, N_S))
        new[5].append(cvs_n)
        new[6].append(s_n)
    return x, tuple(jnp.stack(lst) for lst in new)


def kernel(x_prompt, x_sample, state_mlstm_c, state_mlstm_n, state_mlstm_m, state_mlstm_conv, state_ssd, state_ssd_conv, state_gla, c_prompt, c_sample, w_ada, b_ada, w_in, mlstm_b_i, mlstm_b_f, mlstm_conv_w, mlstm_conv_b, mlstm_norm_g, ssd_conv_w, ssd_conv_b, ssd_dt_bias, ssd_a_log, ssd_d, ssd_norm_g, gla_w_lr, gla_b_lr, gla_norm_g, b_branch, w_branch, w_out, ln_g, ln_b, w_grp, b_grp, w_router, b_router, w_e_gate, w_e_up, w_e_down):
    depth, D, _ = w_in.shape
    alpha = (2 * depth) ** 0.25
    nbp = x_prompt.shape[0]

    edges = np.concatenate([[0], np.cumsum(COL_SIZES)])
    col = {n: w_in[:, :, int(edges[i]):int(edges[i + 1])] for i, n in enumerate(COL_NAMES)}
    w_gate = w_in[:, :, int(edges[-1]):].astype(BF16)
    cat = lambda parts: jnp.concatenate(parts, axis=-1).astype(BF16)
    row = lambda a: a[:, None, :]
    P = {
        "w_mlstm": cat([col["qk_m"], col["v_m"], col["o_m"], _pad_lanes(jnp.concatenate([col["i_m"], col["f_m"]], -1))]),
        "w_ssd": cat([col["z_s"], col["xbc_s"], _pad_lanes(col["dt_s"])]),
        "w_gla": cat([col["q_g"], col["k_g"], col["v_g"], col["g_g"], _pad_lanes(col["lr_g"])]),
        "w_gate": w_gate,
        "mlstm_conv_w": mlstm_conv_w, "mlstm_conv_b": row(mlstm_conv_b),
        "mlstm_bif": row(_pad_lanes(jnp.concatenate([mlstm_b_i, mlstm_b_f], -1))),
        "mlstm_norm_g": row(mlstm_norm_g),
        "ssd_conv_w": ssd_conv_w, "ssd_conv_b": row(ssd_conv_b),
        "ssd_dtb": row(_pad_lanes(ssd_dt_bias)), "ssd_alog": row(_pad_lanes(ssd_a_log)),
        "ssd_dfull": row(jnp.repeat(ssd_d, P_S, axis=-1)), "ssd_norm_g": row(ssd_norm_g),
        "gla_w_lr": jnp.pad(gla_w_lr, ((0, 0), (0, LANES - R_G), (0, 0))).astype(BF16),
        "gla_b_lr": row(gla_b_lr), "gla_norm_g": row(gla_norm_g),
        "b_branch": row(b_branch), "w_branch": w_branch.astype(BF16), "w_out": w_out.astype(BF16),
        "ln_g": ln_g, "ln_b": ln_b,
        "w_rt": _pad_lanes(jnp.concatenate([w_router, w_grp], -1)).astype(BF16),
        "b_rt": row(_pad_lanes(jnp.concatenate([b_router, b_grp], -1))),
        "w_e_gate": w_e_gate, "w_e_up": w_e_up, "w_e_down": w_e_down,
    }

    ada_all = _ada_call(jnp.concatenate([c_prompt, c_sample], axis=0), w_ada, b_ada)

    sample_states = (state_mlstm_c, state_mlstm_n, state_mlstm_m, state_mlstm_conv,
                     state_ssd, state_ssd_conv, state_gla)
    prompt_states = tuple(jnp.zeros((s.shape[0], nbp) + s.shape[2:], x_prompt.dtype) for s in sample_states)

    n_split = PROMPT_SPLIT if nbp % PROMPT_SPLIT == 0 else 1
    hb = nbp // n_split
    parts = [_trunk(x_prompt[i * hb:(i + 1) * hb], ada_all[:, i * hb:(i + 1) * hb],
                    tuple(s[:, i * hb:(i + 1) * hb] for s in prompt_states), P, depth, alpha) for i in range(n_split)]
    y_prompt = jnp.concatenate([p[0] for p in parts], axis=0)
    new_p = tuple(jnp.concatenate([p[1][j] for p in parts], axis=1) for j in range(len(prompt_states)))
    y_sample, new_s = _trunk(x_sample, ada_all[:, nbp:], sample_states, P, depth, alpha)
    return (y_prompt, y_sample) + new_p + new_s
```

```python
import functools

import jax
import jax.numpy as jnp
import numpy as np
from jax import lax
from jax.experimental import pallas as pl
from jax.experimental.pallas import tpu as pltpu

F32 = jnp.float32
BF16 = jnp.bfloat16
NEG_INF = float("-inf")

CHUNK = 64
CONV_W = 4
NH_M, DQK_M, DV_M = 4, 128, 256
QK_M, VM = NH_M * DQK_M, NH_M * DV_M
NH_S, P_S, N_S, G_S = 16, 64, 128, 4
HPG_S = NH_S // G_S
XS, BC_S = NH_S * P_S, G_S * N_S
CONV_S_DIM = XS + 2 * BC_S
NH_G, DK_G, DV_G = 4, 128, 256
QK_G, VG = NH_G * DK_G, NH_G * DV_G
R_G = 16
TAU_G = 16.0
N_GROUPS, EXP_PER_GROUP, TOP_K = 4, 8, 2
N_EXPERTS = N_GROUPS * EXP_PER_GROUP
LN_EPS = 1e-5
LANES = 128

COL_SIZES = (2 * QK_M, VM, VM, NH_M, NH_M, XS, CONV_S_DIM, NH_S, QK_G, QK_G, VG, VG, R_G)
COL_NAMES = ("qk_m", "v_m", "o_m", "i_m", "f_m", "z_s", "xbc_s", "dt_s", "q_g", "k_g", "v_g", "g_g", "lr_g")

VMEM_LIMIT = 56 * 1024 * 1024
MOE_TILE = 512
MOE_SUB = 2


def _cparams(n_axes):
    return pltpu.CompilerParams(dimension_semantics=("arbitrary",) * n_axes, vmem_limit_bytes=VMEM_LIMIT)


def _dot(a, b):
    return jnp.dot(a, b, preferred_element_type=F32)


def _dot_nt(a, b):
    return lax.dot_general(a, b, (((1,), (1,)), ((), ())), preferred_element_type=F32)


def _split3(x):
    hi = x.astype(BF16)
    r = x - hi.astype(F32)
    mid = r.astype(BF16)
    lo = (r - mid.astype(F32)).astype(BF16)
    return hi, mid, lo


def _eye(n, m):
    r = lax.broadcasted_iota(jnp.int32, (n, m), 0)
    c = lax.broadcasted_iota(jnp.int32, (n, m), 1)
    return jnp.where(r == c, 1.0, 0.0).astype(BF16)


def _transpose_rows(x, n):
    e = _eye(n, x.shape[1])
    hi, mid, lo = _split3(x)
    return _dot_nt(e, lo) + _dot_nt(e, mid) + _dot_nt(e, hi)


def _block_tril(Tt, L):
    r = lax.broadcasted_iota(jnp.int32, (Tt, Tt), 0)
    c = lax.broadcasted_iota(jnp.int32, (Tt, Tt), 1)
    return jnp.where(((r // L) == (c // L)) & (c <= r), 1.0, 0.0).astype(BF16)


def _block_full(Tt, L):
    r = lax.broadcasted_iota(jnp.int32, (Tt, Tt), 0)
    c = lax.broadcasted_iota(jnp.int32, (Tt, Tt), 1)
    return jnp.where((r // L) == (c // L), 1.0, 0.0).astype(BF16)


def _chunk_last(x, L):
    n = x.shape[0] // L
    return jnp.concatenate([jnp.broadcast_to(x[(c + 1) * L - 1:(c + 1) * L, :], (L, x.shape[1])) for c in range(n)], axis=0)


def _transpose_f32(x):
    if x.shape[0] % LANES == 0:
        return x.T
    return _transpose_rows(x, x.shape[1])


def _log_sigmoid(x):
    return jnp.minimum(x, 0.0) - jnp.log1p(jnp.exp(-jnp.abs(x)))


def _softplus(x):
    return jnp.maximum(x, 0.0) + jnp.log1p(jnp.exp(-jnp.abs(x)))


def _silu(x):
    return x * jax.nn.sigmoid(x)


def _modulate(x, ada, shift_row, scale_row):
    return x * (1.0 + ada[scale_row:scale_row + 1]) + ada[shift_row:shift_row + 1]


def _causal_conv(buf, x, w_ref, b_ref, Tt):
    buf[8:8 + Tt, :] = x
    y = buf[5:5 + Tt, :] * w_ref[0, 0:1, :]
    y = y + buf[6:6 + Tt, :] * w_ref[0, 1:2, :]
    y = y + buf[7:7 + Tt, :] * w_ref[0, 2:3, :]
    y = y + x * w_ref[0, 3:4, :]
    y = y + b_ref[0]
    buf[5:8, :] = buf[5 + Tt:8 + Tt, :]
    return y


def _group_norm(x, n_groups, center):
    w = x.shape[1] // n_groups
    outs = []
    for g in range(n_groups):
        xg = x[:, g * w:(g + 1) * w]
        if center:
            xg = xg - jnp.mean(xg, axis=1, keepdims=True)
        outs.append(xg * lax.rsqrt(jnp.mean(xg * xg, axis=1, keepdims=True) + LN_EPS))
    return jnp.concatenate(outs, axis=1)


def _ada_kernel(c_ref, w_ref, b_ref, o_ref):
    c = _silu(c_ref[...]).astype(BF16)
    o_ref[0] = _dot(c, w_ref[0].astype(BF16)) + b_ref[0]


def _ada_call(c_all, w_ada, b_ada):
    depth, d, n6 = w_ada.shape
    nb = c_all.shape[0]
    tn = 1536
    return pl.pallas_call(
        _ada_kernel,
        grid=(depth, n6 // tn),
        in_specs=[pl.BlockSpec((nb, d), lambda l, j: (0, 0)),
                  pl.BlockSpec((1, d, tn), lambda l, j: (l, 0, j)),
                  pl.BlockSpec((1, 1, tn), lambda l, j: (l, 0, j))],
        out_specs=pl.BlockSpec((1, nb, tn), lambda l, j: (l, 0, j)),
        out_shape=jax.ShapeDtypeStruct((depth, nb, n6), F32),
        compiler_params=_cparams(2),
        name="ada",
    )(c_all, w_ada, b_ada.reshape(depth, 1, n6))


def _mlstm_kernel(x_ref, ada_ref, w_ref, cw_ref, cb_ref, bif_ref, ng_ref, c0_ref, n0_ref, m0_ref, cv0_ref,
                  out_ref, c_out, n_out, m_out, cv_out,
                  conv_s, q_s, qb_s, kb_s, v_s, o_s, kwt_s, un_s, nu_s, p_s, rs_s, b_s, ml_s, n_s, m_s, *ct_s, Tt, L):
    t = pl.program_id(1)
    nc = Tt // L

    @pl.when(t == 0)
    def _():
        for h in range(NH_M):
            ct_s[h][...] = c0_ref[0, h]
        n_s[...] = n0_ref[0]
        m_s[...] = m0_ref[0]
        conv_s[5:8, :] = cv0_ref[0]

    u = _modulate(x_ref[0], ada_ref[0], 0, 1).astype(BF16)
    g = _dot(u, w_ref[0, :, 2 * QK_M + 2 * VM:]) + bif_ref[0]
    qk = _dot(u, w_ref[0, :, 0:2 * QK_M])
    lane = lax.broadcasted_iota(jnp.int32, g.shape, 1)
    g = jnp.where(lane < NH_M, g, _log_sigmoid(g))

    tril = _block_tril(Tt, L)
    hi, mid, lo = _split3(g)
    cs = _dot(tril, lo) + _dot(tril, mid) + _dot(tril, hi)
    v_s[...] = _dot(u, w_ref[0, :, 2 * QK_M:2 * QK_M + VM]).astype(BF16)
    qk = _silu(_causal_conv(conv_s, qk, cw_ref, cb_ref, Tt))
    q = qk[:, :QK_M]
    k = qk[:, QK_M:] * DQK_M ** -0.5
    q_s[...] = q
    qb_s[...] = q.astype(BF16)
    kb_s[...] = k.astype(BF16)
    o_s[...] = _dot(u, w_ref[0, :, 2 * QK_M + VM:2 * QK_M + 2 * VM])
    b0 = pltpu.roll(cs, LANES - NH_M, axis=1)
    r = g - b0
    row = lax.broadcasted_iota(jnp.int32, g.shape, 0) % L
    a = r
    sh = 1
    while sh < L:
        a = jnp.maximum(a, jnp.where(row >= sh, pltpu.roll(a, sh, axis=0), NEG_INF))
        sh *= 2
    al = a
    sh = 1
    while sh < L:
        al = jnp.maximum(al, jnp.where(row + sh < L, pltpu.roll(al, Tt - sh, axis=0), NEG_INF))
        sh *= 2
    b_s[...] = b0
    ml_s[...] = b0 + a
    wsrc = jnp.exp(r - al)
    r_rows = _transpose_f32(r)[0:8, :]
    col = lax.broadcasted_iota(jnp.int32, (DQK_M, Tt), 1)
    rr = lax.broadcasted_iota(jnp.int32, (L, L), 0)
    cc = lax.broadcasted_iota(jnp.int32, (L, L), 1)
    causal = cc <= rr
    for h in range(NH_M):
        ks = slice(h * DQK_M, (h + 1) * DQK_M)
        kw = wsrc[:, h:h + 1] * k[:, ks]
        kwt_s[h] = _transpose_f32(kw).astype(BF16)
        for c in range(nc):
            nu_s[c * NH_M + h:c * NH_M + h + 1, :] = jnp.sum(kw[c * L:(c + 1) * L], axis=0, keepdims=True)
    for c in range(nc):
        for h in range(NH_M):
            vs = slice(h * DV_M, (h + 1) * DV_M)
            kwm = jnp.where((col // L) == c, kwt_s[h], jnp.zeros((), BF16)) if nc > 1 else kwt_s[h]
            un_s[c * NH_M + h] = _dot(kwm, v_s[:, vs])
    for c in range(nc):
        rows = slice(c * L, (c + 1) * L)
        sl = []
        for h in range(NH_M):
            ks = slice(h * DQK_M, (h + 1) * DQK_M)
            dloc = jnp.exp(jnp.where(causal, r_rows[h:h + 1, rows] - a[rows, h:h + 1], NEG_INF))
            sl.append(_dot_nt(qb_s[rows, ks], kb_s[rows, ks]) * dloc)
        for h in range(NH_M):
            vs = slice(h * DV_M, (h + 1) * DV_M)
            p_s[rows, vs] = _dot(sl[h].astype(BF16), v_s[rows, vs])
            rs_s[rows, h:h + 1] = jnp.sum(sl[h], axis=1, keepdims=True)

    lane_l = lax.broadcasted_iota(jnp.int32, (L, LANES), 1)
    for c in range(nc):
        rows = slice(c * L, (c + 1) * L)
        m_prev = m_s[...]
        bc = b_s[rows, :]
        mlc = ml_s[rows, :]
        mt = jnp.maximum(bc + m_prev, mlc)
        corr = jnp.exp(mlc - mt)
        w_inter = jnp.exp(bc + m_prev - mt)
        inter = []
        qn = jnp.zeros((L, LANES), F32)
        for h in range(NH_M):
            ks = slice(h * DQK_M, (h + 1) * DQK_M)
            inter.append(_dot(qb_s[rows, ks], ct_s[h][...].astype(BF16)))
            qn_h = jnp.sum(q_s[rows, ks] * n_s[h:h + 1, :], axis=1, keepdims=True)
            qn = jnp.where(lane_l == h, qn_h, qn)
        den = corr * rs_s[rows, :] + w_inter * qn
        inv = 1.0 / jnp.maximum(jnp.abs(den), jnp.exp(-mt))
        ca = corr * inv
        cb2 = w_inter * inv
        for h in range(NH_M):
            vs = slice(h * DV_M, (h + 1) * DV_M)
            p_s[rows, vs] = ca[:, h:h + 1] * p_s[rows, vs] + cb2[:, h:h + 1] * inter[h]
        m_last = mt[L - 1:L, :]
        c_l = jnp.exp(mlc[L - 1:L, :] - m_last)
        decay = jnp.exp(bc[L - 1:L, :] + m_prev - m_last)
        for h in range(NH_M):
            ct_s[h][...] = decay[:, h:h + 1] * ct_s[h][...] + c_l[:, h:h + 1] * un_s[c * NH_M + h]
            n_s[h:h + 1, :] = decay[:, h:h + 1] * n_s[h:h + 1, :] + c_l[:, h:h + 1] * nu_s[c * NH_M + h:c * NH_M + h + 1, :]
        m_s[...] = m_last

    hn = _group_norm(p_s[...], NH_M, True) * ng_ref[0] * jax.nn.sigmoid(o_s[...])
    out_ref[0] = hn.astype(out_ref.dtype)

    @pl.when(t == pl.num_programs(1) - 1)
    def _():
        for h in range(NH_M):
            c_out[0, h] = ct_s[h][...]
        n_out[0] = n_s[...]
        m_out[0] = m_s[...]
        cv_out[0] = conv_s[5:8, :]


def _mlstm_call(l, x, ada, w, cw, cb, bif, ng, c0t, n0, m0, cv0, Tt):
    B, T, D = x.shape
    L = min(CHUNK, T)
    wcols = w.shape[2]
    bmap = lambda b, t: (b, 0, 0)
    lmap = lambda b, t: (l, 0, 0)
    return pl.pallas_call(
        functools.partial(_mlstm_kernel, Tt=Tt, L=L),
        grid=(B, T // Tt),
        in_specs=[pl.BlockSpec((1, Tt, D), lambda b, t: (b, t, 0)),
                  pl.BlockSpec((1, 6, D), bmap),
                  pl.BlockSpec((1, D, wcols), lmap),
                  pl.BlockSpec((1, CONV_W, 2 * QK_M), lmap),
                  pl.BlockSpec((1, 1, 2 * QK_M), lmap),
                  pl.BlockSpec((1, 1, LANES), lmap),
                  pl.BlockSpec((1, 1, VM), lmap),
                  pl.BlockSpec((1, NH_M, DQK_M, DV_M), lambda b, t: (b, 0, 0, 0)),
                  pl.BlockSpec((1, NH_M, DQK_M), bmap),
                  pl.BlockSpec((1, 1, LANES), bmap),
                  pl.BlockSpec((1, CONV_W - 1, 2 * QK_M), bmap)],
        out_specs=[pl.BlockSpec((1, Tt, VM), lambda b, t: (b, t, 0)),
                   pl.BlockSpec((1, NH_M, DQK_M, DV_M), lambda b, t: (b, 0, 0, 0)),
                   pl.BlockSpec((1, NH_M, DQK_M), bmap),
                   pl.BlockSpec((1, 1, LANES), bmap),
                   pl.BlockSpec((1, CONV_W - 1, 2 * QK_M), bmap)],
        out_shape=[jax.ShapeDtypeStruct((B, T, VM), BF16),
                   jax.ShapeDtypeStruct((B, NH_M, DQK_M, DV_M), F32),
                   jax.ShapeDtypeStruct((B, NH_M, DQK_M), F32),
                   jax.ShapeDtypeStruct((B, 1, LANES), F32),
                   jax.ShapeDtypeStruct((B, CONV_W - 1, 2 * QK_M), F32)],
        scratch_shapes=[pltpu.VMEM((8 + Tt, 2 * QK_M), F32),
                        pltpu.VMEM((Tt, QK_M), F32),
                        pltpu.VMEM((Tt, QK_M), BF16),
                        pltpu.VMEM((Tt, QK_M), BF16),
                        pltpu.VMEM((Tt, VM), BF16),
                        pltpu.VMEM((Tt, VM), F32),
                        pltpu.VMEM((NH_M, DQK_M, Tt), BF16),
                        pltpu.VMEM((Tt // L * NH_M, DQK_M, DV_M), F32),
                        pltpu.VMEM((Tt // L * NH_M, DQK_M), F32),
                        pltpu.VMEM((Tt, VM), F32),
                        pltpu.VMEM((Tt, LANES), F32),
                        pltpu.VMEM((Tt, LANES), F32),
                        pltpu.VMEM((Tt, LANES), F32),
                        pltpu.VMEM((NH_M, DQK_M), F32),
                        pltpu.VMEM((1, LANES), F32)] + [pltpu.VMEM((DQK_M, DV_M), F32)] * NH_M,
        compiler_params=_cparams(2),
        name="mlstm",
    )(x, ada, w, cw, cb, bif, ng, c0t, n0, m0, cv0)


def _pair_cols(x, hd, lane):
    return jnp.where(lane < P_S, x[:, hd:hd + 1], x[:, hd + 1:hd + 2])


def _ssd_kernel(x_ref, ada_ref, w_ref, cw_ref, cb_ref, dtb_ref, alog_ref, d_ref, ng_ref, h0_ref, cv0_ref,
                out_ref, h_out, cv_out,
                conv_s, zg_s, x_s, xw_s, b_s, c_s, bt_s, e2_s, un_s, dec_s, y_s, *ht_s, Tt, L):
    t = pl.program_id(1)
    nc = Tt // L
    n_pairs = NH_S // 2

    @pl.when(t == 0)
    def _():
        for g in range(G_S):
            ht_s[g][...] = h0_ref[0, g]
        conv_s[5:8, :] = cv0_ref[0]

    u = _modulate(x_ref[0], ada_ref[0], 0, 1).astype(BF16)
    dt = _softplus(_dot(u, w_ref[0, :, XS + CONV_S_DIM:]) + dtb_ref[0])
    xbc = _dot(u, w_ref[0, :, XS:XS + CONV_S_DIM])
    a_row = -jnp.exp(alog_ref[0])

    tril = _block_tril(Tt, L)
    full = _block_full(Tt, L)
    hi, mid, lo = _split3(dt * a_row)
    cs = _dot(tril, lo) + _dot(tril, mid) + _dot(tril, hi)
    csl = _dot(full, lo) + _dot(full, mid) + _dot(full, hi)
    xbc = _silu(_causal_conv(conv_s, xbc, cw_ref, cb_ref, Tt))
    x = xbc[:, :XS]
    bm = xbc[:, XS:XS + BC_S]
    x_s[...] = x
    b_s[...] = bm.astype(BF16)
    c_s[...] = xbc[:, XS + BC_S:].astype(BF16)
    wsrc = jnp.exp(csl - cs) * dt
    dec = jnp.exp(csl)
    cs_rows = _transpose_f32(cs)[0:NH_S, :]
    dt_rows = _transpose_f32(dt)[0:NH_S, :]
    for g in range(G_S):
        bt_s[g] = _transpose_f32(bm[:, g * N_S:(g + 1) * N_S]).astype(BF16)
    zg_s[...] = _silu(_dot(u, w_ref[0, :, 0:XS]))
    lane_t = lax.broadcasted_iota(jnp.int32, (Tt, LANES), 1)
    for pp in range(n_pairs):
        pc = slice(pp * LANES, (pp + 1) * LANES)
        e2_s[:, pc] = _pair_cols(cs, 2 * pp, lane_t)
        xw_s[:, pc] = (x[:, pc] * _pair_cols(wsrc, 2 * pp, lane_t)).astype(BF16)
    lane1 = lax.broadcasted_iota(jnp.int32, (1, LANES), 1)
    for c in range(nc):
        for pp in range(n_pairs):
            r0 = c * L
            dec_s[c * n_pairs + pp:c * n_pairs + pp + 1, :] = _pair_cols(dec[r0:r0 + 1, :], 2 * pp, lane1)

    rr = lax.broadcasted_iota(jnp.int32, (L, LANES), 0)
    lane_l = lax.broadcasted_iota(jnp.int32, (L, LANES), 1)
    causal2 = (lane_l % P_S) <= rr
    col = lax.broadcasted_iota(jnp.int32, (N_S, Tt), 1)
    zero_b = jnp.zeros((), BF16)
    for c in range(nc):
        rows = slice(c * L, (c + 1) * L)
        for g in range(G_S):
            gs = slice(g * N_S, (g + 1) * N_S)
            bg = b_s[rows, gs]
            cb2 = _dot_nt(c_s[rows, gs], jnp.concatenate([bg, bg], axis=0))
            btm = jnp.where((col // L) == c, bt_s[g], zero_b) if nc > 1 else bt_s[g]
            un_s[c * G_S + g] = _dot(btm, xw_s[:, g * 2 * LANES:(g + 1) * 2 * LANES])
            for p2 in range(2):
                pp = g * 2 + p2
                hd = 2 * pp
                pc = slice(pp * LANES, (pp + 1) * LANES)
                csr2 = jnp.concatenate([cs_rows[hd:hd + 1, rows], cs_rows[hd + 1:hd + 2, rows]], axis=1)
                dtr2 = jnp.concatenate([dt_rows[hd:hd + 1, rows], dt_rows[hd + 1:hd + 2, rows]], axis=1)
                seg2 = jnp.exp(jnp.where(causal2, e2_s[rows, pc] - csr2, NEG_INF))
                m2 = (cb2 * seg2 * dtr2).astype(BF16)
                xp = x_s[rows, pc].astype(BF16)
                xbd = jnp.concatenate([jnp.where(lane_l < P_S, xp, zero_b), jnp.where(lane_l >= P_S, xp, zero_b)], axis=0)
                y_s[rows, pc] = _dot(m2, xbd)

    for c in range(nc):
        rows = slice(c * L, (c + 1) * L)
        for g in range(G_S):
            gc = slice(g * 2 * LANES, (g + 1) * 2 * LANES)
            yi = _dot(c_s[rows, g * N_S:(g + 1) * N_S], ht_s[g][...].astype(BF16))
            y_s[rows, gc] += jnp.exp(e2_s[rows, gc]) * yi
        for g in range(G_S):
            for p2 in range(2):
                pp = g * 2 + p2
                hc = slice(p2 * LANES, (p2 + 1) * LANES)
                ht_s[g][:, hc] = (dec_s[c * n_pairs + pp:c * n_pairs + pp + 1, :] * ht_s[g][:, hc]
                                  + un_s[c * G_S + g][:, hc])

    y = (y_s[...] + d_ref[0] * x_s[...]) * zg_s[...]
    out_ref[0] = (_group_norm(y, G_S, False) * ng_ref[0]).astype(out_ref.dtype)

    @pl.when(t == pl.num_programs(1) - 1)
    def _():
        for g in range(G_S):
            h_out[0, g] = ht_s[g][...]
        cv_out[0] = conv_s[5:8, :]


def _ssd_call(l, x, ada, w, cw, cb, dtb, alog, dfull, ng, h0t, cv0, Tt):
    B, T, D = x.shape
    L = min(CHUNK, T)
    wcols = w.shape[2]
    bmap = lambda b, t: (b, 0, 0)
    lmap = lambda b, t: (l, 0, 0)
    hshape = (G_S, N_S, HPG_S * P_S)
    assert L == P_S and 2 * L == LANES and T % Tt == 0 and Tt % L == 0
    return pl.pallas_call(
        functools.partial(_ssd_kernel, Tt=Tt, L=L),
        grid=(B, T // Tt),
        in_specs=[pl.BlockSpec((1, Tt, D), lambda b, t: (b, t, 0)),
                  pl.BlockSpec((1, 6, D), bmap),
                  pl.BlockSpec((1, D, wcols), lmap),
                  pl.BlockSpec((1, CONV_W, CONV_S_DIM), lmap),
                  pl.BlockSpec((1, 1, CONV_S_DIM), lmap),
                  pl.BlockSpec((1, 1, LANES), lmap),
                  pl.BlockSpec((1, 1, LANES), lmap),
                  pl.BlockSpec((1, 1, XS), lmap),
                  pl.BlockSpec((1, 1, XS), lmap),
                  pl.BlockSpec((1,) + hshape, lambda b, t: (b, 0, 0, 0)),
                  pl.BlockSpec((1, CONV_W - 1, CONV_S_DIM), bmap)],
        out_specs=[pl.BlockSpec((1, Tt, XS), lambda b, t: (b, t, 0)),
                   pl.BlockSpec((1,) + hshape, lambda b, t: (b, 0, 0, 0)),
                   pl.BlockSpec((1, CONV_W - 1, CONV_S_DIM), bmap)],
        out_shape=[jax.ShapeDtypeStruct((B, T, XS), BF16),
                   jax.ShapeDtypeStruct((B,) + hshape, F32),
                   jax.ShapeDtypeStruct((B, CONV_W - 1, CONV_S_DIM), F32)],
        scratch_shapes=[pltpu.VMEM((8 + Tt, CONV_S_DIM), F32),
                        pltpu.VMEM((Tt, XS), F32),
                        pltpu.VMEM((Tt, XS), F32),
                        pltpu.VMEM((Tt, XS), BF16),
                        pltpu.VMEM((Tt, BC_S), BF16),
                        pltpu.VMEM((Tt, BC_S), BF16),
                        pltpu.VMEM((G_S, N_S, Tt), BF16),
                        pltpu.VMEM((Tt, XS), F32),
                        pltpu.VMEM((Tt // L * G_S, N_S, HPG_S * P_S), F32),
                        pltpu.VMEM((Tt // L * (NH_S // 2), LANES), F32),
                        pltpu.VMEM((Tt, XS), F32)] + [pltpu.VMEM(hshape[1:], F32)] * G_S,
        compiler_params=_cparams(2),
        name="ssd",
    )(x, ada, w, cw, cb, dtb, alog, dfull, ng, h0t, cv0)


def _gla_kernel(x_ref, ada_ref, w_ref, wlr_ref, blr_ref, ng_ref, s0_ref,
                out_ref, s_out,
                sg_s, qg_s, kg_s, v_s, kdt_s, egt_s, un_s, o_s, *st_s, Tt, L):
    t = pl.program_id(1)
    nc = Tt // L

    @pl.when(t == 0)
    def _():
        for h in range(NH_G):
            st_s[h][...] = s0_ref[0, h]

    u = _modulate(x_ref[0], ada_ref[0], 0, 1).astype(BF16)
    lr = _dot(u, w_ref[0, :, 2 * QK_G + 2 * VG:]).astype(BF16)
    lg = _log_sigmoid(_dot(lr, wlr_ref[0]) + blr_ref[0]) / TAU_G
    q = _dot(u, w_ref[0, :, 0:QK_G])
    k = _dot(u, w_ref[0, :, QK_G:2 * QK_G]) * DK_G ** -0.5

    tril = _block_tril(Tt, L)
    hi, mid, lo = _split3(lg)
    G = _dot(tril, lo) + _dot(tril, mid) + _dot(tril, hi)
    v_s[...] = _dot(u, w_ref[0, :, 2 * QK_G:2 * QK_G + VG]).astype(BF16)
    if nc > 1:
        Gl = _chunk_last(G, L)
    else:
        full = _block_full(Tt, L)
        Gl = _dot(full, lo) + _dot(full, mid) + _dot(full, hi)
    qg_s[...] = (q * jnp.exp(G)).astype(BF16)
    kg_s[...] = (k * jnp.exp(-G)).astype(BF16)
    kd = k * jnp.exp(Gl - G)
    eg = jnp.exp(Gl)
    sg_s[...] = _silu(_dot(u, w_ref[0, :, 2 * QK_G + VG:2 * QK_G + 2 * VG]))
    for h in range(NH_G):
        ks = slice(h * DK_G, (h + 1) * DK_G)
        kdt_s[h] = _transpose_f32(kd[:, ks]).astype(BF16)
        egt_s[h] = _transpose_f32(eg[:, ks])
    col = lax.broadcasted_iota(jnp.int32, (DK_G, Tt), 1)
    for c in range(nc):
        for h in range(NH_G):
            vs = slice(h * DV_G, (h + 1) * DV_G)
            kdm = jnp.where((col // L) == c, kdt_s[h], jnp.zeros((), BF16)) if nc > 1 else kdt_s[h]
            un_s[c * NH_G + h] = _dot(kdm, v_s[:, vs])

    rr = lax.broadcasted_iota(jnp.int32, (L, L), 0)
    cc = lax.broadcasted_iota(jnp.int32, (L, L), 1)
    causal = cc <= rr
    for c in range(nc):
        rows = slice(c * L, (c + 1) * L)
        att = []
        for h in range(NH_G):
            ks = slice(h * DK_G, (h + 1) * DK_G)
            att.append(_dot_nt(qg_s[rows, ks], kg_s[rows, ks]))
        for h in range(NH_G):
            ks = slice(h * DK_G, (h + 1) * DK_G)
            vs = slice(h * DV_G, (h + 1) * DV_G)
            o_s[rows, vs] = _dot(qg_s[rows, ks], st_s[h][...].astype(BF16))
        for h in range(NH_G):
            vs = slice(h * DV_G, (h + 1) * DV_G)
            o_s[rows, vs] += _dot(jnp.where(causal, att[h], 0.0).astype(BF16), v_s[rows, vs])
        for h in range(NH_G):
            st_s[h][...] = egt_s[h][:, c * L:c * L + 1] * st_s[h][...] + un_s[c * NH_G + h]

    out_ref[0] = (_group_norm(o_s[...], NH_G, False) * ng_ref[0] * sg_s[...]).astype(out_ref.dtype)

    @pl.when(t == pl.num_programs(1) - 1)
    def _():
        for h in range(NH_G):
            s_out[0, h] = st_s[h][...]


def _gla_call(l, x, ada, w, wlr, blr, ng, s0, Tt):
    B, T, D = x.shape
    L = min(CHUNK, T)
    wcols = w.shape[2]
    bmap = lambda b, t: (b, 0, 0)
    lmap = lambda b, t: (l, 0, 0)
    sshape = (NH_G, DK_G, DV_G)
    return pl.pallas_call(
        functools.partial(_gla_kernel, Tt=Tt, L=L),
        grid=(B, T // Tt),
        in_specs=[pl.BlockSpec((1, Tt, D), lambda b, t: (b, t, 0)),
                  pl.BlockSpec((1, 6, D), bmap),
                  pl.BlockSpec((1, D, wcols), lmap),
                  pl.BlockSpec((1, LANES, QK_G), lmap),
                  pl.BlockSpec((1, 1, QK_G), lmap),
                  pl.BlockSpec((1, 1, VG), lmap),
                  pl.BlockSpec((1,) + sshape, lambda b, t: (b, 0, 0, 0))],
        out_specs=[pl.BlockSpec((1, Tt, VG), lambda b, t: (b, t, 0)),
                   pl.BlockSpec((1,) + sshape, lambda b, t: (b, 0, 0, 0))],
        out_shape=[jax.ShapeDtypeStruct((B, T, VG), BF16),
                   jax.ShapeDtypeStruct((B,) + sshape, F32)],
        scratch_shapes=[pltpu.VMEM((Tt, VG), F32),
                        pltpu.VMEM((Tt, QK_G), BF16),
                        pltpu.VMEM((Tt, QK_G), BF16),
                        pltpu.VMEM((Tt, VG), BF16),
                        pltpu.VMEM((NH_G, DK_G, Tt), BF16),
                        pltpu.VMEM((NH_G, DK_G, Tt), F32),
                        pltpu.VMEM((Tt // L * NH_G, DK_G, DV_G), F32),
                        pltpu.VMEM((Tt, VG), F32)] + [pltpu.VMEM(sshape[1:], F32)] * NH_G,
        compiler_params=_cparams(2),
        name="gla",
    )(x, ada, w, wlr, blr, ng, s0)


def _layer_norm(x, g, b):
    mu = jnp.mean(x, axis=1, keepdims=True)
    xc = x - mu
    var = jnp.mean(xc * xc, axis=1, keepdims=True)
    return xc * lax.rsqrt(var + LN_EPS) * g + b


def _merge_kernel(x_ref, ada_ref, hm_ref, ys_ref, og_ref, wg_ref, bb_ref, wb_ref, wo_ref, lng_ref, lnb_ref,
                  wrt_ref, brt_ref,
                  x1_ref, u2_ref, eid_ref, wts_ref, *, alpha):
    x = x_ref[0]
    ada = ada_ref[0]
    D = x.shape[1]
    u = _modulate(x, ada, 0, 1).astype(BF16)
    gate = jax.nn.sigmoid(_dot(u, wg_ref[0]) + bb_ref[0])
    merged = (gate[:, 0:D] * _dot(hm_ref[0], wb_ref[0, 0])
              + gate[:, D:2 * D] * _dot(ys_ref[0], wb_ref[0, 1])
              + gate[:, 2 * D:3 * D] * _dot(og_ref[0], wb_ref[0, 2]))
    y = _dot(merged.astype(BF16), wo_ref[0])
    x1 = _layer_norm(alpha * x + ada[2:3] * y, lng_ref[0, 0:1], lnb_ref[0, 0:1])
    x1_ref[0] = x1
    u2 = _modulate(x1, ada, 3, 4).astype(BF16)
    bits = lax.bitcast_convert_type(u2.astype(F32), jnp.uint32)
    u2_ref[0] = (bits[:, :D // 2] >> 16) | bits[:, D // 2:]

    logits = _dot(u2, wrt_ref[0]) + brt_ref[0]
    lane = lax.broadcasted_iota(jnp.int32, logits.shape, 1)
    lane_f = lane.astype(F32)
    big = float(LANES)
    is_g = (lane >= N_EXPERTS) & (lane < N_EXPERTS + N_GROUPS)
    gmax = jnp.max(jnp.where(is_g, logits, NEG_INF), axis=1, keepdims=True)
    gsum = jnp.sum(jnp.where(is_g, jnp.exp(logits - gmax), 0.0), axis=1, keepdims=True)
    pg_top = 1.0 / gsum
    g_lane = jnp.min(jnp.where(is_g & (logits == gmax), lane_f, big), axis=1, keepdims=True)
    g_idx = g_lane.astype(jnp.int32) - N_EXPERTS
    in_grp = (lane < N_EXPERTS) & ((lane // EXP_PER_GROUP) == g_idx)
    el = jnp.where(in_grp, logits, NEG_INF)
    v1 = jnp.max(el, axis=1, keepdims=True)
    i1 = jnp.min(jnp.where(in_grp & (el == v1), lane_f, big), axis=1, keepdims=True)
    rest = in_grp & (lane_f != i1)
    el2 = jnp.where(rest, logits, NEG_INF)
    v2 = jnp.max(el2, axis=1, keepdims=True)
    i2 = jnp.min(jnp.where(rest & (el2 == v2), lane_f, big), axis=1, keepdims=True)
    e = jnp.exp(v2 - v1)
    w1 = pg_top / (1.0 + e)
    w2 = pg_top * e / (1.0 + e)
    eid_ref[0] = jnp.where(lane == 0, i1, jnp.where(lane == 1, i2, 0.0)).astype(jnp.int32)
    wts_ref[0] = jnp.where(lane == 0, w1, jnp.where(lane == 1, w2, 0.0))


def _merge_call(l, x, ada, hm, ys, og, wg, bb, wb, wo, lng, lnb, wrt, brt, tm, alpha):
    B, T, D = x.shape
    bmap = lambda b, t: (b, 0, 0)
    lmap = lambda b, t: (l, 0, 0)
    tmap = lambda b, t: (b, t, 0)
    return pl.pallas_call(
        functools.partial(_merge_kernel, alpha=alpha),
        grid=(B, T // tm),
        in_specs=[pl.BlockSpec((1, tm, D), tmap),
                  pl.BlockSpec((1, 6, D), bmap),
                  pl.BlockSpec((1, tm, D), tmap),
                  pl.BlockSpec((1, tm, D), tmap),
                  pl.BlockSpec((1, tm, D), tmap),
                  pl.BlockSpec((1, D, 3 * D), lmap),
                  pl.BlockSpec((1, 1, 3 * D), lmap),
                  pl.BlockSpec((1, 3, D, D), lambda b, t: (l, 0, 0, 0)),
                  pl.BlockSpec((1, D, D), lmap),
                  pl.BlockSpec((1, 2, D), lmap),
                  pl.BlockSpec((1, 2, D), lmap),
                  pl.BlockSpec((1, D, LANES), lmap),
                  pl.BlockSpec((1, 1, LANES), lmap)],
        out_specs=[pl.BlockSpec((1, tm, D), tmap),
                   pl.BlockSpec((1, tm, D // 2), tmap),
                   pl.BlockSpec((1, tm, LANES), tmap),
                   pl.BlockSpec((1, tm, LANES), tmap)],
        out_shape=[jax.ShapeDtypeStruct((B, T, D), F32),
                   jax.ShapeDtypeStruct((B, T, D // 2), jnp.uint32),
                   jax.ShapeDtypeStruct((B, T, LANES), jnp.int32),
                   jax.ShapeDtypeStruct((B, T, LANES), F32)],
        compiler_params=_cparams(2),
        name="merge",
    )(x, ada, hm, ys, og, wg, bb, wb, wo, lng, lnb, wrt, brt)


def _moe_kernel(te_ref, tok_ref, nxt_ref, u2_hbm, wg_ref, wu_ref, wd_ref, o_ref, xa, xb, sem, wg_b, wu_b, wd_b):
    i = pl.program_id(0)
    n = pl.num_programs(0)
    tm = xa.shape[0]
    bufs = (xa, xb)

    def row_copy(idx_ref, k, r, s):
        return pltpu.make_async_copy(u2_hbm.at[pl.ds(idx_ref[0, 0, k * tm + r], 1), :], bufs[s].at[pl.ds(r, 1), :], sem.at[s])

    def tile_wait(s):
        pltpu.make_async_copy(u2_hbm.at[pl.ds(0, tm), :], bufs[s], sem.at[s]).wait()

    @pl.when(i == 0)
    def _():
        for r in range(tm):
            row_copy(tok_ref, 0, r, 0).start()

    prev = te_ref[jnp.maximum(i - 1, 0)]

    @pl.when((i == 0) | (te_ref[i] != prev))
    def _():
        wg_b[...] = wg_ref[0, 0].astype(BF16)
        wu_b[...] = wu_ref[0, 0].astype(BF16)
        wd_b[...] = wd_ref[0, 0].astype(BF16)

    for k in range(MOE_SUB):
        s = k % 2
        tile_wait(s)
        if k + 1 < MOE_SUB:
            for r in range(tm):
                row_copy(tok_ref, k + 1, r, 1 - s).start()
        else:
            for r in range(tm):
                row_copy(nxt_ref, 0, r, 1 - s).start()
        w = bufs[s][...]
        lo = lax.bitcast_convert_type(w << 16, F32)
        hi = lax.bitcast_convert_type(w & jnp.uint32(0xFFFF0000), F32)
        xs = jnp.concatenate([lo, hi], axis=1).astype(BF16)
        hg = _dot(xs, wg_b[...])
        hu = _dot(xs, wu_b[...])
        o_ref[k * tm:(k + 1) * tm, :] = _dot((_silu(hg) * hu).astype(BF16), wd_b[...])

    @pl.when(i == n - 1)
    def _():
        tile_wait(MOE_SUB % 2)


def _moe_call(l, tile_expert, row_tok, u2p, weg, weu, wed):
    D, dexp = weg.shape[2], weg.shape[3]
    tm = MOE_TILE // MOE_SUB
    nt = row_tok.shape[0] // MOE_TILE
    tok3 = row_tok.reshape(nt, 1, MOE_TILE)
    return pl.pallas_call(
        _moe_kernel,
        grid_spec=pltpu.PrefetchScalarGridSpec(
            num_scalar_prefetch=1,
            grid=(nt,),
            in_specs=[pl.BlockSpec((1, 1, MOE_TILE), lambda i, te: (i, 0, 0), memory_space=pltpu.SMEM),
                      pl.BlockSpec((1, 1, MOE_TILE), lambda i, te: (jnp.minimum(i + 1, nt - 1), 0, 0), memory_space=pltpu.SMEM),
                      pl.BlockSpec(memory_space=pl.ANY),
                      pl.BlockSpec((1, 1, D, dexp), lambda i, te: (l, te[i], 0, 0)),
                      pl.BlockSpec((1, 1, D, dexp), lambda i, te: (l, te[i], 0, 0)),
                      pl.BlockSpec((1, 1, dexp, D), lambda i, te: (l, te[i], 0, 0))],
            out_specs=pl.BlockSpec((MOE_TILE, D), lambda i, te: (i, 0)),
            scratch_shapes=[pltpu.VMEM((tm, D // 2), jnp.uint32), pltpu.VMEM((tm, D // 2), jnp.uint32),
                            pltpu.SemaphoreType.DMA((2,)),
                            pltpu.VMEM((D, dexp), BF16), pltpu.VMEM((D, dexp), BF16), pltpu.VMEM((dexp, D), BF16)]),
        out_shape=jax.ShapeDtypeStruct((nt * MOE_TILE, D), F32),
        compiler_params=_cparams(1),
        name="moe",
    )(tile_expert, tok3, tok3, u2p, weg, weu, wed)


def _ln2_kernel(x1_ref, ada_ref, y0_ref, y1_ref, wts_ref, lng_ref, lnb_ref, o_ref, *, alpha):
    ada = ada_ref[0]
    wts = wts_ref[0]
    moe = y0_ref[0] * wts[:, 0:1] + y1_ref[0] * wts[:, 1:2]
    o_ref[0] = _layer_norm(alpha * x1_ref[0] + ada[5:6] * moe, lng_ref[0, 1:2], lnb_ref[0, 1:2])


def _ln2_call(l, x1, ada, y0, y1, wts, lng, lnb, tm, alpha):
    B, T, D = x1.shape
    bmap = lambda b, t: (b, 0, 0)
    lmap = lambda b, t: (l, 0, 0)
    tmap = lambda b, t: (b, t, 0)
    return pl.pallas_call(
        functools.partial(_ln2_kernel, alpha=alpha),
        grid=(B, T // tm),
        in_specs=[pl.BlockSpec((1, tm, D), tmap),
                  pl.BlockSpec((1, 6, D), bmap),
                  pl.BlockSpec((1, tm, D), tmap),
                  pl.BlockSpec((1, tm, D), tmap),
                  pl.BlockSpec((1, tm, LANES), tmap),
                  pl.BlockSpec((1, 2, D), lmap),
                  pl.BlockSpec((1, 2, D), lmap)],
        out_specs=pl.BlockSpec((1, tm, D), tmap),
        out_shape=jax.ShapeDtypeStruct((B, T, D), F32),
        compiler_params=_cparams(2),
        name="ln2",
    )(x1, ada, y0, y1, wts, lng, lnb)


def _take_rows(a, idx):
    return a.at[idx].get(mode="promise_in_bounds")


def _lookup(table, idx):
    sel = idx[:, None] == jnp.arange(table.shape[0], dtype=idx.dtype)[None, :]
    return jnp.sum(jnp.where(sel, table[None, :], 0), axis=1)


def _route(eid, n_tok):
    tm = MOE_TILE
    flat = eid.reshape(-1)
    n_asg = flat.shape[0]
    n_rows = (-(-n_asg // tm) + N_EXPERTS) * tm
    order = jnp.argsort(flat).astype(jnp.int32)
    inv = jnp.argsort(order).astype(jnp.int32)
    experts = jnp.arange(N_EXPERTS, dtype=jnp.int32)
    sizes = jnp.sum((flat[:, None] == experts[None, :]).astype(jnp.int32), axis=0)
    off = jnp.cumsum(sizes) - sizes
    psz = ((sizes + tm - 1) // tm) * tm
    pend = jnp.cumsum(psz)
    poff = pend - psz
    dest = _lookup(poff - off, flat) + inv
    tile_start = jnp.arange(n_rows // tm, dtype=jnp.int32) * tm
    tile_expert = jnp.minimum(jnp.sum((pend[None, :] <= tile_start[:, None]).astype(jnp.int32), axis=1),
                              N_EXPERTS - 1)
    within = jnp.arange(n_rows, dtype=jnp.int32) - jnp.repeat(poff[tile_expert], tm)
    valid = within < jnp.repeat(sizes[tile_expert], tm)
    pos = jnp.clip(jnp.repeat(off[tile_expert], tm) + within, 0, n_asg - 1)
    row_tok = jnp.where(valid, _take_rows(order, pos) // TOP_K, 0)
    return row_tok, dest.reshape(n_tok, TOP_K), tile_expert


def _pad_lanes(a, width=LANES):
    return jnp.pad(a, [(0, 0)] * (a.ndim - 1) + [(0, width - a.shape[-1])])


def _time_block(T):
    return min(T, 256)


def _merge_rows(T):
    return min(T, 512)


def _trunk(x, ada_all, states, P, depth, alpha):
    B, T, D = x.shape
    Tt = _time_block(T)
    c_m, n_m, m_m, cv_m, h_s, cv_s, s_g = states
    new = [[] for _ in range(7)]
    for l in range(depth):
        ada = ada_all[l].reshape(B, 6, D)
        c0t = jnp.swapaxes(c_m[l], -1, -2)
        m0 = _pad_lanes(m_m[l])[:, None, :]
        h0t = (h_s[l].reshape(B, G_S, HPG_S, P_S, N_S).transpose(0, 1, 4, 2, 3)
               .reshape(B, G_S, N_S, HPG_S * P_S))
        hm, c_t, n_n, m_n, cvm_n = _mlstm_call(l, x, ada, P["w_mlstm"], P["mlstm_conv_w"], P["mlstm_conv_b"],
                                               P["mlstm_bif"], P["mlstm_norm_g"], c0t, n_m[l], m0, cv_m[l], Tt)
        ys, h_t, cvs_n = _ssd_call(l, x, ada, P["w_ssd"], P["ssd_conv_w"], P["ssd_conv_b"], P["ssd_dtb"],
                                   P["ssd_alog"], P["ssd_dfull"], P["ssd_norm_g"], h0t, cv_s[l], Tt)
        og, s_n = _gla_call(l, x, ada, P["w_gla"], P["gla_w_lr"], P["gla_b_lr"], P["gla_norm_g"], s_g[l], Tt)
        x1, u2, eid, wts = _merge_call(l, x, ada, hm, ys, og, P["w_gate"], P["b_branch"], P["w_branch"],
                                       P["w_out"], P["ln_g"], P["ln_b"], P["w_rt"], P["b_rt"], _merge_rows(T), alpha)
        row_tok, dest, tile_expert = _route(eid[:, :, :TOP_K], B * T)
        ye = _moe_call(l, tile_expert, row_tok, u2.reshape(B * T, D // 2), P["w_e_gate"], P["w_e_up"], P["w_e_down"])
        y0 = _take_rows(ye, dest[:, 0]).reshape(B, T, D)
        y1 = _take_rows(ye, dest[:, 1]).reshape(B, T, D)
        x = _ln2_call(l, x1, ada, y0, y1, wts, P["ln_g"], P["ln_b"], Tt, alpha)

        new[0].append(jnp.swapaxes(c_t, -1, -2))
        new[1].append(n_n)
        new[2].append(m_n[:, 0, :NH_M])
        new[3].append(cvm_n)
        new[4].append(h_t.reshape(B, G_S, N_S, HPG_S, P_S).transpose(0, 1, 3, 4, 2).reshape(B, NH_S, P_S, N_S))
        new[5].append(cvs_n)
        new[6].append(s_n)
    return x, tuple(jnp.stack(lst) for lst in new)


def kernel(x_prompt, x_sample, state_mlstm_c, state_mlstm_n, state_mlstm_m, state_mlstm_conv, state_ssd, state_ssd_conv, state_gla, c_prompt, c_sample, w_ada, b_ada, w_in, mlstm_b_i, mlstm_b_f, mlstm_conv_w, mlstm_conv_b, mlstm_norm_g, ssd_conv_w, ssd_conv_b, ssd_dt_bias, ssd_a_log, ssd_d, ssd_norm_g, gla_w_lr, gla_b_lr, gla_norm_g, b_branch, w_branch, w_out, ln_g, ln_b, w_grp, b_grp, w_router, b_router, w_e_gate, w_e_up, w_e_down):
    depth, D, _ = w_in.shape
    alpha = (2 * depth) ** 0.25
    nbp = x_prompt.shape[0]

    edges = np.concatenate([[0], np.cumsum(COL_SIZES)])
    col = {n: w_in[:, :, int(edges[i]):int(edges[i + 1])] for i, n in enumerate(COL_NAMES)}
    w_gate = w_in[:, :, int(edges[-1]):].astype(BF16)
    cat = lambda parts: jnp.concatenate(parts, axis=-1).astype(BF16)
    row = lambda a: a[:, None, :]
    P = {
        "w_mlstm": cat([col["qk_m"], col["v_m"], col["o_m"], _pad_lanes(jnp.concatenate([col["i_m"], col["f_m"]], -1))]),
        "w_ssd": cat([col["z_s"], col["xbc_s"], _pad_lanes(col["dt_s"])]),
        "w_gla": cat([col["q_g"], col["k_g"], col["v_g"], col["g_g"], _pad_lanes(col["lr_g"])]),
        "w_gate": w_gate,
        "mlstm_conv_w": mlstm_conv_w, "mlstm_conv_b": row(mlstm_conv_b),
        "mlstm_bif": row(_pad_lanes(jnp.concatenate([mlstm_b_i, mlstm_b_f], -1))),
        "mlstm_norm_g": row(mlstm_norm_g),
        "ssd_conv_w": ssd_conv_w, "ssd_conv_b": row(ssd_conv_b),
        "ssd_dtb": row(_pad_lanes(ssd_dt_bias)), "ssd_alog": row(_pad_lanes(ssd_a_log)),
        "ssd_dfull": row(jnp.repeat(ssd_d, P_S, axis=-1)), "ssd_norm_g": row(ssd_norm_g),
        "gla_w_lr": jnp.pad(gla_w_lr, ((0, 0), (0, LANES - R_G), (0, 0))).astype(BF16),
        "gla_b_lr": row(gla_b_lr), "gla_norm_g": row(gla_norm_g),
        "b_branch": row(b_branch), "w_branch": w_branch.astype(BF16), "w_out": w_out.astype(BF16),
        "ln_g": ln_g, "ln_b": ln_b,
        "w_rt": _pad_lanes(jnp.concatenate([w_router, w_grp], -1)).astype(BF16),
        "b_rt": row(_pad_lanes(jnp.concatenate([b_router, b_grp], -1))),
        "w_e_gate": w_e_gate, "w_e_up": w_e_up, "w_e_down": w_e_down,
    }

    ada_all = _ada_call(jnp.concatenate([c_prompt, c_sample], axis=0), w_ada, b_ada)

    sample_states = (state_mlstm_c, state_mlstm_n, state_mlstm_m, state_mlstm_conv,
                     state_ssd, state_ssd_conv, state_gla)
    prompt_states = tuple(jnp.zeros((s.shape[0], nbp) + s.shape[2:], x_prompt.dtype) for s in sample_states)

    y_prompt, new_p = _trunk(x_prompt, ada_all[:, :nbp], prompt_states, P, depth, alpha)
    y_sample, new_s = _trunk(x_sample, ada_all[:, nbp:], sample_states, P, depth, alpha)
    return (y_prompt, y_sample) + new_p + new_s
```

```python
import functools

import jax
import jax.numpy as jnp
import numpy as np
from jax import lax
from jax.experimental import pallas as pl
from jax.experimental.pallas import tpu as pltpu

F32 = jnp.float32
BF16 = jnp.bfloat16
NEG_INF = float("-inf")

CHUNK = 64
CONV_W = 4
NH_M, DQK_M, DV_M = 4, 128, 256
QK_M, VM = NH_M * DQK_M, NH_M * DV_M
NH_S, P_S, N_S, G_S = 16, 64, 128, 4
HPG_S = NH_S // G_S
XS, BC_S = NH_S * P_S, G_S * N_S
CONV_S_DIM = XS + 2 * BC_S
NH_G, DK_G, DV_G = 4, 128, 256
QK_G, VG = NH_G * DK_G, NH_G * DV_G
R_G = 16
TAU_G = 16.0
N_GROUPS, EXP_PER_GROUP, TOP_K = 4, 8, 2
N_EXPERTS = N_GROUPS * EXP_PER_GROUP
LN_EPS = 1e-5
LANES = 128

COL_SIZES = (2 * QK_M, VM, VM, NH_M, NH_M, XS, CONV_S_DIM, NH_S, QK_G, QK_G, VG, VG, R_G)
COL_NAMES = ("qk_m", "v_m", "o_m", "i_m", "f_m", "z_s", "xbc_s", "dt_s", "q_g", "k_g", "v_g", "g_g", "lr_g")

VMEM_LIMIT = 56 * 1024 * 1024
MOE_TILE = 512


def _cparams(n_axes):
    return pltpu.CompilerParams(dimension_semantics=("arbitrary",) * n_axes, vmem_limit_bytes=VMEM_LIMIT)


def _dot(a, b):
    return jnp.dot(a, b, preferred_element_type=F32)


def _dot_nt(a, b):
    return lax.dot_general(a, b, (((1,), (1,)), ((), ())), preferred_element_type=F32)


def _split3(x):
    hi = x.astype(BF16)
    r = x - hi.astype(F32)
    mid = r.astype(BF16)
    lo = (r - mid.astype(F32)).astype(BF16)
    return hi, mid, lo


def _eye(n, m):
    r = lax.broadcasted_iota(jnp.int32, (n, m), 0)
    c = lax.broadcasted_iota(jnp.int32, (n, m), 1)
    return jnp.where(r == c, 1.0, 0.0).astype(BF16)


def _transpose_rows(x, n):
    e = _eye(n, x.shape[1])
    hi, mid, lo = _split3(x)
    return _dot_nt(e, lo) + _dot_nt(e, mid) + _dot_nt(e, hi)


def _block_tril(Tt, L):
    r = lax.broadcasted_iota(jnp.int32, (Tt, Tt), 0)
    c = lax.broadcasted_iota(jnp.int32, (Tt, Tt), 1)
    return jnp.where(((r // L) == (c // L)) & (c <= r), 1.0, 0.0).astype(BF16)


def _block_full(Tt, L):
    r = lax.broadcasted_iota(jnp.int32, (Tt, Tt), 0)
    c = lax.broadcasted_iota(jnp.int32, (Tt, Tt), 1)
    return jnp.where((r // L) == (c // L), 1.0, 0.0).astype(BF16)


def _chunk_last(x, L):
    n = x.shape[0] // L
    return jnp.concatenate([jnp.broadcast_to(x[(c + 1) * L - 1:(c + 1) * L, :], (L, x.shape[1])) for c in range(n)], axis=0)


def _transpose_f32(x):
    if x.shape[0] % LANES == 0:
        return x.T
    return _transpose_rows(x, x.shape[1])


def _log_sigmoid(x):
    return jnp.minimum(x, 0.0) - jnp.log1p(jnp.exp(-jnp.abs(x)))


def _softplus(x):
    return jnp.maximum(x, 0.0) + jnp.log1p(jnp.exp(-jnp.abs(x)))


def _silu(x):
    return x * jax.nn.sigmoid(x)


def _modulate(x, ada, shift_row, scale_row):
    return x * (1.0 + ada[scale_row:scale_row + 1]) + ada[shift_row:shift_row + 1]


def _causal_conv(buf, x, w_ref, b_ref, Tt):
    buf[8:8 + Tt, :] = x
    y = buf[5:5 + Tt, :] * w_ref[0, 0:1, :]
    y = y + buf[6:6 + Tt, :] * w_ref[0, 1:2, :]
    y = y + buf[7:7 + Tt, :] * w_ref[0, 2:3, :]
    y = y + x * w_ref[0, 3:4, :]
    y = y + b_ref[0]
    buf[5:8, :] = buf[5 + Tt:8 + Tt, :]
    return y


def _group_norm(x, n_groups, center):
    w = x.shape[1] // n_groups
    outs = []
    for g in range(n_groups):
        xg = x[:, g * w:(g + 1) * w]
        if center:
            xg = xg - jnp.mean(xg, axis=1, keepdims=True)
        outs.append(xg * lax.rsqrt(jnp.mean(xg * xg, axis=1, keepdims=True) + LN_EPS))
    return jnp.concatenate(outs, axis=1)


def _ada_kernel(c_ref, w_ref, b_ref, o_ref):
    c = _silu(c_ref[...]).astype(BF16)
    o_ref[0] = _dot(c, w_ref[0].astype(BF16)) + b_ref[0]


def _ada_call(c_all, w_ada, b_ada):
    depth, d, n6 = w_ada.shape
    nb = c_all.shape[0]
    tn = 1536
    return pl.pallas_call(
        _ada_kernel,
        grid=(depth, n6 // tn),
        in_specs=[pl.BlockSpec((nb, d), lambda l, j: (0, 0)),
                  pl.BlockSpec((1, d, tn), lambda l, j: (l, 0, j)),
                  pl.BlockSpec((1, 1, tn), lambda l, j: (l, 0, j))],
        out_specs=pl.BlockSpec((1, nb, tn), lambda l, j: (l, 0, j)),
        out_shape=jax.ShapeDtypeStruct((depth, nb, n6), F32),
        compiler_params=_cparams(2),
        name="ada",
    )(c_all, w_ada, b_ada.reshape(depth, 1, n6))


def _mlstm_kernel(x_ref, ada_ref, w_ref, cw_ref, cb_ref, bif_ref, ng_ref, c0_ref, n0_ref, m0_ref, cv0_ref,
                  out_ref, c_out, n_out, m_out, cv_out,
                  conv_s, q_s, qb_s, kb_s, v_s, o_s, kwt_s, un_s, nu_s, p_s, rs_s, b_s, ml_s, n_s, m_s, *ct_s, Tt, L):
    t = pl.program_id(1)
    nc = Tt // L

    @pl.when(t == 0)
    def _():
        for h in range(NH_M):
            ct_s[h][...] = c0_ref[0, h]
        n_s[...] = n0_ref[0]
        m_s[...] = m0_ref[0]
        conv_s[5:8, :] = cv0_ref[0]

    u = _modulate(x_ref[0], ada_ref[0], 0, 1).astype(BF16)
    g = _dot(u, w_ref[0, :, 2 * QK_M + 2 * VM:]) + bif_ref[0]
    qk = _dot(u, w_ref[0, :, 0:2 * QK_M])
    lane = lax.broadcasted_iota(jnp.int32, g.shape, 1)
    g = jnp.where(lane < NH_M, g, _log_sigmoid(g))

    tril = _block_tril(Tt, L)
    hi, mid, lo = _split3(g)
    cs = _dot(tril, lo) + _dot(tril, mid) + _dot(tril, hi)
    v_s[...] = _dot(u, w_ref[0, :, 2 * QK_M:2 * QK_M + VM]).astype(BF16)
    qk = _silu(_causal_conv(conv_s, qk, cw_ref, cb_ref, Tt))
    q = qk[:, :QK_M]
    k = qk[:, QK_M:] * DQK_M ** -0.5
    q_s[...] = q
    qb_s[...] = q.astype(BF16)
    kb_s[...] = k.astype(BF16)
    o_s[...] = _dot(u, w_ref[0, :, 2 * QK_M + VM:2 * QK_M + 2 * VM])
    b0 = pltpu.roll(cs, LANES - NH_M, axis=1)
    r = g - b0
    row = lax.broadcasted_iota(jnp.int32, g.shape, 0) % L
    a = r
    sh = 1
    while sh < L:
        a = jnp.maximum(a, jnp.where(row >= sh, pltpu.roll(a, sh, axis=0), NEG_INF))
        sh *= 2
    al = a
    sh = 1
    while sh < L:
        al = jnp.maximum(al, jnp.where(row + sh < L, pltpu.roll(al, Tt - sh, axis=0), NEG_INF))
        sh *= 2
    b_s[...] = b0
    ml_s[...] = b0 + a
    wsrc = jnp.exp(r - al)
    r_rows = _transpose_f32(r)[0:8, :]
    col = lax.broadcasted_iota(jnp.int32, (DQK_M, Tt), 1)
    rr = lax.broadcasted_iota(jnp.int32, (L, L), 0)
    cc = lax.broadcasted_iota(jnp.int32, (L, L), 1)
    causal = cc <= rr
    for h in range(NH_M):
        ks = slice(h * DQK_M, (h + 1) * DQK_M)
        kw = wsrc[:, h:h + 1] * k[:, ks]
        kwt_s[h] = _transpose_f32(kw).astype(BF16)
        for c in range(nc):
            nu_s[c * NH_M + h:c * NH_M + h + 1, :] = jnp.sum(kw[c * L:(c + 1) * L], axis=0, keepdims=True)
    for c in range(nc):
        for h in range(NH_M):
            vs = slice(h * DV_M, (h + 1) * DV_M)
            kwm = jnp.where((col // L) == c, kwt_s[h], jnp.zeros((), BF16)) if nc > 1 else kwt_s[h]
            un_s[c * NH_M + h] = _dot(kwm, v_s[:, vs])
    for c in range(nc):
        rows = slice(c * L, (c + 1) * L)
        sl = []
        for h in range(NH_M):
            ks = slice(h * DQK_M, (h + 1) * DQK_M)
            dloc = jnp.exp(jnp.where(causal, r_rows[h:h + 1, rows] - a[rows, h:h + 1], NEG_INF))
            sl.append(_dot_nt(qb_s[rows, ks], kb_s[rows, ks]) * dloc)
        for h in range(NH_M):
            vs = slice(h * DV_M, (h + 1) * DV_M)
            p_s[rows, vs] = _dot(sl[h].astype(BF16), v_s[rows, vs])
            rs_s[rows, h:h + 1] = jnp.sum(sl[h], axis=1, keepdims=True)

    lane_l = lax.broadcasted_iota(jnp.int32, (L, LANES), 1)
    for c in range(nc):
        rows = slice(c * L, (c + 1) * L)
        m_prev = m_s[...]
        bc = b_s[rows, :]
        mlc = ml_s[rows, :]
        mt = jnp.maximum(bc + m_prev, mlc)
        corr = jnp.exp(mlc - mt)
        w_inter = jnp.exp(bc + m_prev - mt)
        inter = []
        qn = jnp.zeros((L, LANES), F32)
        for h in range(NH_M):
            ks = slice(h * DQK_M, (h + 1) * DQK_M)
            inter.append(_dot(qb_s[rows, ks], ct_s[h][...].astype(BF16)))
            qn_h = jnp.sum(q_s[rows, ks] * n_s[h:h + 1, :], axis=1, keepdims=True)
            qn = jnp.where(lane_l == h, qn_h, qn)
        den = corr * rs_s[rows, :] + w_inter * qn
        inv = 1.0 / jnp.maximum(jnp.abs(den), jnp.exp(-mt))
        ca = corr * inv
        cb2 = w_inter * inv
        for h in range(NH_M):
            vs = slice(h * DV_M, (h + 1) * DV_M)
            p_s[rows, vs] = ca[:, h:h + 1] * p_s[rows, vs] + cb2[:, h:h + 1] * inter[h]
        m_last = mt[L - 1:L, :]
        c_l = jnp.exp(mlc[L - 1:L, :] - m_last)
        decay = jnp.exp(bc[L - 1:L, :] + m_prev - m_last)
        for h in range(NH_M):
            ct_s[h][...] = decay[:, h:h + 1] * ct_s[h][...] + c_l[:, h:h + 1] * un_s[c * NH_M + h]
            n_s[h:h + 1, :] = decay[:, h:h + 1] * n_s[h:h + 1, :] + c_l[:, h:h + 1] * nu_s[c * NH_M + h:c * NH_M + h + 1, :]
        m_s[...] = m_last

    hn = _group_norm(p_s[...], NH_M, True) * ng_ref[0] * jax.nn.sigmoid(o_s[...])
    out_ref[0] = hn.astype(out_ref.dtype)

    @pl.when(t == pl.num_programs(1) - 1)
    def _():
        for h in range(NH_M):
            c_out[0, h] = ct_s[h][...]
        n_out[0] = n_s[...]
        m_out[0] = m_s[...]
        cv_out[0] = conv_s[5:8, :]


def _mlstm_call(l, x, ada, w, cw, cb, bif, ng, c0t, n0, m0, cv0, Tt):
    B, T, D = x.shape
    L = min(CHUNK, T)
    wcols = w.shape[2]
    bmap = lambda b, t: (b, 0, 0)
    lmap = lambda b, t: (l, 0, 0)
    return pl.pallas_call(
        functools.partial(_mlstm_kernel, Tt=Tt, L=L),
        grid=(B, T // Tt),
        in_specs=[pl.BlockSpec((1, Tt, D), lambda b, t: (b, t, 0)),
                  pl.BlockSpec((1, 6, D), bmap),
                  pl.BlockSpec((1, D, wcols), lmap),
                  pl.BlockSpec((1, CONV_W, 2 * QK_M), lmap),
                  pl.BlockSpec((1, 1, 2 * QK_M), lmap),
                  pl.BlockSpec((1, 1, LANES), lmap),
                  pl.BlockSpec((1, 1, VM), lmap),
                  pl.BlockSpec((1, NH_M, DQK_M, DV_M), lambda b, t: (b, 0, 0, 0)),
                  pl.BlockSpec((1, NH_M, DQK_M), bmap),
                  pl.BlockSpec((1, 1, LANES), bmap),
                  pl.BlockSpec((1, CONV_W - 1, 2 * QK_M), bmap)],
        out_specs=[pl.BlockSpec((1, Tt, VM), lambda b, t: (b, t, 0)),
                   pl.BlockSpec((1, NH_M, DQK_M, DV_M), lambda b, t: (b, 0, 0, 0)),
                   pl.BlockSpec((1, NH_M, DQK_M), bmap),
                   pl.BlockSpec((1, 1, LANES), bmap),
                   pl.BlockSpec((1, CONV_W - 1, 2 * QK_M), bmap)],
        out_shape=[jax.ShapeDtypeStruct((B, T, VM), BF16),
                   jax.ShapeDtypeStruct((B, NH_M, DQK_M, DV_M), F32),
                   jax.ShapeDtypeStruct((B, NH_M, DQK_M), F32),
                   jax.ShapeDtypeStruct((B, 1, LANES), F32),
                   jax.ShapeDtypeStruct((B, CONV_W - 1, 2 * QK_M), F32)],
        scratch_shapes=[pltpu.VMEM((8 + Tt, 2 * QK_M), F32),
                        pltpu.VMEM((Tt, QK_M), F32),
                        pltpu.VMEM((Tt, QK_M), BF16),
                        pltpu.VMEM((Tt, QK_M), BF16),
                        pltpu.VMEM((Tt, VM), BF16),
                        pltpu.VMEM((Tt, VM), F32),
                        pltpu.VMEM((NH_M, DQK_M, Tt), BF16),
                        pltpu.VMEM((Tt // L * NH_M, DQK_M, DV_M), F32),
                        pltpu.VMEM((Tt // L * NH_M, DQK_M), F32),
                        pltpu.VMEM((Tt, VM), F32),
                        pltpu.VMEM((Tt, LANES), F32),
                        pltpu.VMEM((Tt, LANES), F32),
                        pltpu.VMEM((Tt, LANES), F32),
                        pltpu.VMEM((NH_M, DQK_M), F32),
                        pltpu.VMEM((1, LANES), F32)] + [pltpu.VMEM((DQK_M, DV_M), F32)] * NH_M,
        compiler_params=_cparams(2),
        name="mlstm",
    )(x, ada, w, cw, cb, bif, ng, c0t, n0, m0, cv0)


def _pair_cols(x, hd, lane):
    return jnp.where(lane < P_S, x[:, hd:hd + 1], x[:, hd + 1:hd + 2])


def _ssd_kernel(x_ref, ada_ref, w_ref, cw_ref, cb_ref, dtb_ref, alog_ref, d_ref, ng_ref, h0_ref, cv0_ref,
                out_ref, h_out, cv_out,
                conv_s, zg_s, x_s, xw_s, b_s, c_s, bt_s, e2_s, un_s, dec_s, y_s, *ht_s, Tt, L):
    t = pl.program_id(1)
    nc = Tt // L
    n_pairs = NH_S // 2

    @pl.when(t == 0)
    def _():
        for g in range(G_S):
            ht_s[g][...] = h0_ref[0, g]
        conv_s[5:8, :] = cv0_ref[0]

    u = _modulate(x_ref[0], ada_ref[0], 0, 1).astype(BF16)
    dt = _softplus(_dot(u, w_ref[0, :, XS + CONV_S_DIM:]) + dtb_ref[0])
    xbc = _dot(u, w_ref[0, :, XS:XS + CONV_S_DIM])
    a_row = -jnp.exp(alog_ref[0])

    tril = _block_tril(Tt, L)
    full = _block_full(Tt, L)
    hi, mid, lo = _split3(dt * a_row)
    cs = _dot(tril, lo) + _dot(tril, mid) + _dot(tril, hi)
    csl = _dot(full, lo) + _dot(full, mid) + _dot(full, hi)
    xbc = _silu(_causal_conv(conv_s, xbc, cw_ref, cb_ref, Tt))
    x = xbc[:, :XS]
    bm = xbc[:, XS:XS + BC_S]
    x_s[...] = x
    b_s[...] = bm.astype(BF16)
    c_s[...] = xbc[:, XS + BC_S:].astype(BF16)
    wsrc = jnp.exp(csl - cs) * dt
    dec = jnp.exp(csl)
    cs_rows = _transpose_f32(cs)[0:NH_S, :]
    dt_rows = _transpose_f32(dt)[0:NH_S, :]
    for g in range(G_S):
        bt_s[g] = _transpose_f32(bm[:, g * N_S:(g + 1) * N_S]).astype(BF16)
    zg_s[...] = _silu(_dot(u, w_ref[0, :, 0:XS]))
    lane_t = lax.broadcasted_iota(jnp.int32, (Tt, LANES), 1)
    for pp in range(n_pairs):
        pc = slice(pp * LANES, (pp + 1) * LANES)
        e2_s[:, pc] = _pair_cols(cs, 2 * pp, lane_t)
        xw_s[:, pc] = (x[:, pc] * _pair_cols(wsrc, 2 * pp, lane_t)).astype(BF16)
    lane1 = lax.broadcasted_iota(jnp.int32, (1, LANES), 1)
    for c in range(nc):
        for pp in range(n_pairs):
            r0 = c * L
            dec_s[c * n_pairs + pp:c * n_pairs + pp + 1, :] = _pair_cols(dec[r0:r0 + 1, :], 2 * pp, lane1)

    rr = lax.broadcasted_iota(jnp.int32, (L, LANES), 0)
    lane_l = lax.broadcasted_iota(jnp.int32, (L, LANES), 1)
    causal2 = (lane_l % P_S) <= rr
    col = lax.broadcasted_iota(jnp.int32, (N_S, Tt), 1)
    zero_b = jnp.zeros((), BF16)
    for c in range(nc):
        rows = slice(c * L, (c + 1) * L)
        for g in range(G_S):
            gs = slice(g * N_S, (g + 1) * N_S)
            bg = b_s[rows, gs]
            cb2 = _dot_nt(c_s[rows, gs], jnp.concatenate([bg, bg], axis=0))
            btm = jnp.where((col // L) == c, bt_s[g], zero_b) if nc > 1 else bt_s[g]
            un_s[c * G_S + g] = _dot(btm, xw_s[:, g * 2 * LANES:(g + 1) * 2 * LANES])
            for p2 in range(2):
                pp = g * 2 + p2
                hd = 2 * pp
                pc = slice(pp * LANES, (pp + 1) * LANES)
                csr2 = jnp.concatenate([cs_rows[hd:hd + 1, rows], cs_rows[hd + 1:hd + 2, rows]], axis=1)
                dtr2 = jnp.concatenate([dt_rows[hd:hd + 1, rows], dt_rows[hd + 1:hd + 2, rows]], axis=1)
                seg2 = jnp.exp(jnp.where(causal2, e2_s[rows, pc] - csr2, NEG_INF))
                m2 = (cb2 * seg2 * dtr2).astype(BF16)
                xp = x_s[rows, pc].astype(BF16)
                xbd = jnp.concatenate([jnp.where(lane_l < P_S, xp, zero_b), jnp.where(lane_l >= P_S, xp, zero_b)], axis=0)
                y_s[rows, pc] = _dot(m2, xbd)

    for c in range(nc):
        rows = slice(c * L, (c + 1) * L)
        for g in range(G_S):
            gc = slice(g * 2 * LANES, (g + 1) * 2 * LANES)
            yi = _dot(c_s[rows, g * N_S:(g + 1) * N_S], ht_s[g][...].astype(BF16))
            y_s[rows, gc] += jnp.exp(e2_s[rows, gc]) * yi
        for g in range(G_S):
            for p2 in range(2):
                pp = g * 2 + p2
                hc = slice(p2 * LANES, (p2 + 1) * LANES)
                ht_s[g][:, hc] = (dec_s[c * n_pairs + pp:c * n_pairs + pp + 1, :] * ht_s[g][:, hc]
                                  + un_s[c * G_S + g][:, hc])

    y = (y_s[...] + d_ref[0] * x_s[...]) * zg_s[...]
    out_ref[0] = (_group_norm(y, G_S, False) * ng_ref[0]).astype(out_ref.dtype)

    @pl.when(t == pl.num_programs(1) - 1)
    def _():
        for g in range(G_S):
            h_out[0, g] = ht_s[g][...]
        cv_out[0] = conv_s[5:8, :]


def _ssd_call(l, x, ada, w, cw, cb, dtb, alog, dfull, ng, h0t, cv0, Tt):
    B, T, D = x.shape
    L = min(CHUNK, T)
    wcols = w.shape[2]
    bmap = lambda b, t: (b, 0, 0)
    lmap = lambda b, t: (l, 0, 0)
    hshape = (G_S, N_S, HPG_S * P_S)
    assert L == P_S and 2 * L == LANES and T % Tt == 0 and Tt % L == 0
    return pl.pallas_call(
        functools.partial(_ssd_kernel, Tt=Tt, L=L),
        grid=(B, T // Tt),
        in_specs=[pl.BlockSpec((1, Tt, D), lambda b, t: (b, t, 0)),
                  pl.BlockSpec((1, 6, D), bmap),
                  pl.BlockSpec((1, D, wcols), lmap),
                  pl.BlockSpec((1, CONV_W, CONV_S_DIM), lmap),
                  pl.BlockSpec((1, 1, CONV_S_DIM), lmap),
                  pl.BlockSpec((1, 1, LANES), lmap),
                  pl.BlockSpec((1, 1, LANES), lmap),
                  pl.BlockSpec((1, 1, XS), lmap),
                  pl.BlockSpec((1, 1, XS), lmap),
                  pl.BlockSpec((1,) + hshape, lambda b, t: (b, 0, 0, 0)),
                  pl.BlockSpec((1, CONV_W - 1, CONV_S_DIM), bmap)],
        out_specs=[pl.BlockSpec((1, Tt, XS), lambda b, t: (b, t, 0)),
                   pl.BlockSpec((1,) + hshape, lambda b, t: (b, 0, 0, 0)),
                   pl.BlockSpec((1, CONV_W - 1, CONV_S_DIM), bmap)],
        out_shape=[jax.ShapeDtypeStruct((B, T, XS), BF16),
                   jax.ShapeDtypeStruct((B,) + hshape, F32),
                   jax.ShapeDtypeStruct((B, CONV_W - 1, CONV_S_DIM), F32)],
        scratch_shapes=[pltpu.VMEM((8 + Tt, CONV_S_DIM), F32),
                        pltpu.VMEM((Tt, XS), F32),
                        pltpu.VMEM((Tt, XS), F32),
                        pltpu.VMEM((Tt, XS), BF16),
                        pltpu.VMEM((Tt, BC_S), BF16),
                        pltpu.VMEM((Tt, BC_S), BF16),
                        pltpu.VMEM((G_S, N_S, Tt), BF16),
                        pltpu.VMEM((Tt, XS), F32),
                        pltpu.VMEM((Tt // L * G_S, N_S, HPG_S * P_S), F32),
                        pltpu.VMEM((Tt // L * (NH_S // 2), LANES), F32),
                        pltpu.VMEM((Tt, XS), F32)] + [pltpu.VMEM(hshape[1:], F32)] * G_S,
        compiler_params=_cparams(2),
        name="ssd",
    )(x, ada, w, cw, cb, dtb, alog, dfull, ng, h0t, cv0)


def _gla_kernel(x_ref, ada_ref, w_ref, wlr_ref, blr_ref, ng_ref, s0_ref,
                out_ref, s_out,
                sg_s, qg_s, kg_s, v_s, kdt_s, egt_s, un_s, o_s, *st_s, Tt, L):
    t = pl.program_id(1)
    nc = Tt // L

    @pl.when(t == 0)
    def _():
        for h in range(NH_G):
            st_s[h][...] = s0_ref[0, h]

    u = _modulate(x_ref[0], ada_ref[0], 0, 1).astype(BF16)
    lr = _dot(u, w_ref[0, :, 2 * QK_G + 2 * VG:]).astype(BF16)
    lg = _log_sigmoid(_dot(lr, wlr_ref[0]) + blr_ref[0]) / TAU_G
    q = _dot(u, w_ref[0, :, 0:QK_G])
    k = _dot(u, w_ref[0, :, QK_G:2 * QK_G]) * DK_G ** -0.5

    tril = _block_tril(Tt, L)
    hi, mid, lo = _split3(lg)
    G = _dot(tril, lo) + _dot(tril, mid) + _dot(tril, hi)
    v_s[...] = _dot(u, w_ref[0, :, 2 * QK_G:2 * QK_G + VG]).astype(BF16)
    if nc > 1:
        Gl = _chunk_last(G, L)
    else:
        full = _block_full(Tt, L)
        Gl = _dot(full, lo) + _dot(full, mid) + _dot(full, hi)
    qg_s[...] = (q * jnp.exp(G)).astype(BF16)
    kg_s[...] = (k * jnp.exp(-G)).astype(BF16)
    kd = k * jnp.exp(Gl - G)
    eg = jnp.exp(Gl)
    sg_s[...] = _silu(_dot(u, w_ref[0, :, 2 * QK_G + VG:2 * QK_G + 2 * VG]))
    for h in range(NH_G):
        ks = slice(h * DK_G, (h + 1) * DK_G)
        kdt_s[h] = _transpose_f32(kd[:, ks]).astype(BF16)
        egt_s[h] = _transpose_f32(eg[:, ks])
    col = lax.broadcasted_iota(jnp.int32, (DK_G, Tt), 1)
    for c in range(nc):
        for h in range(NH_G):
            vs = slice(h * DV_G, (h + 1) * DV_G)
            kdm = jnp.where((col // L) == c, kdt_s[h], jnp.zeros((), BF16)) if nc > 1 else kdt_s[h]
            un_s[c * NH_G + h] = _dot(kdm, v_s[:, vs])

    rr = lax.broadcasted_iota(jnp.int32, (L, L), 0)
    cc = lax.broadcasted_iota(jnp.int32, (L, L), 1)
    causal = cc <= rr
    for c in range(nc):
        rows = slice(c * L, (c + 1) * L)
        att = []
        for h in range(NH_G):
            ks = slice(h * DK_G, (h + 1) * DK_G)
            att.append(_dot_nt(qg_s[rows, ks], kg_s[rows, ks]))
        for h in range(NH_G):
            ks = slice(h * DK_G, (h + 1) * DK_G)
            vs = slice(h * DV_G, (h + 1) * DV_G)
            o_s[rows, vs] = _dot(qg_s[rows, ks], st_s[h][...].astype(BF16))
        for h in range(NH_G):
            vs = slice(h * DV_G, (h + 1) * DV_G)
            o_s[rows, vs] += _dot(jnp.where(causal, att[h], 0.0).astype(BF16), v_s[rows, vs])
        for h in range(NH_G):
            st_s[h][...] = egt_s[h][:, c * L:c * L + 1] * st_s[h][...] + un_s[c * NH_G + h]

    out_ref[0] = (_group_norm(o_s[...], NH_G, False) * ng_ref[0] * sg_s[...]).astype(out_ref.dtype)

    @pl.when(t == pl.num_programs(1) - 1)
    def _():
        for h in range(NH_G):
            s_out[0, h] = st_s[h][...]


def _gla_call(l, x, ada, w, wlr, blr, ng, s0, Tt):
    B, T, D = x.shape
    L = min(CHUNK, T)
    wcols = w.shape[2]
    bmap = lambda b, t: (b, 0, 0)
    lmap = lambda b, t: (l, 0, 0)
    sshape = (NH_G, DK_G, DV_G)
    return pl.pallas_call(
        functools.partial(_gla_kernel, Tt=Tt, L=L),
        grid=(B, T // Tt),
        in_specs=[pl.BlockSpec((1, Tt, D), lambda b, t: (b, t, 0)),
                  pl.BlockSpec((1, 6, D), bmap),
                  pl.BlockSpec((1, D, wcols), lmap),
                  pl.BlockSpec((1, LANES, QK_G), lmap),
                  pl.BlockSpec((1, 1, QK_G), lmap),
                  pl.BlockSpec((1, 1, VG), lmap),
                  pl.BlockSpec((1,) + sshape, lambda b, t: (b, 0, 0, 0))],
        out_specs=[pl.BlockSpec((1, Tt, VG), lambda b, t: (b, t, 0)),
                   pl.BlockSpec((1,) + sshape, lambda b, t: (b, 0, 0, 0))],
        out_shape=[jax.ShapeDtypeStruct((B, T, VG), BF16),
                   jax.ShapeDtypeStruct((B,) + sshape, F32)],
        scratch_shapes=[pltpu.VMEM((Tt, VG), F32),
                        pltpu.VMEM((Tt, QK_G), BF16),
                        pltpu.VMEM((Tt, QK_G), BF16),
                        pltpu.VMEM((Tt, VG), BF16),
                        pltpu.VMEM((NH_G, DK_G, Tt), BF16),
                        pltpu.VMEM((NH_G, DK_G, Tt), F32),
                        pltpu.VMEM((Tt // L * NH_G, DK_G, DV_G), F32),
                        pltpu.VMEM((Tt, VG), F32)] + [pltpu.VMEM(sshape[1:], F32)] * NH_G,
        compiler_params=_cparams(2),
        name="gla",
    )(x, ada, w, wlr, blr, ng, s0)


def _layer_norm(x, g, b):
    mu = jnp.mean(x, axis=1, keepdims=True)
    xc = x - mu
    var = jnp.mean(xc * xc, axis=1, keepdims=True)
    return xc * lax.rsqrt(var + LN_EPS) * g + b


def _merge_kernel(x_ref, ada_ref, hm_ref, ys_ref, og_ref, wg_ref, bb_ref, wb_ref, wo_ref, lng_ref, lnb_ref,
                  wrt_ref, brt_ref,
                  x1_ref, u2_ref, eid_ref, wts_ref, *, alpha):
    x = x_ref[0]
    ada = ada_ref[0]
    D = x.shape[1]
    u = _modulate(x, ada, 0, 1).astype(BF16)
    gate = jax.nn.sigmoid(_dot(u, wg_ref[0]) + bb_ref[0])
    merged = (gate[:, 0:D] * _dot(hm_ref[0], wb_ref[0, 0])
              + gate[:, D:2 * D] * _dot(ys_ref[0], wb_ref[0, 1])
              + gate[:, 2 * D:3 * D] * _dot(og_ref[0], wb_ref[0, 2]))
    y = _dot(merged.astype(BF16), wo_ref[0])
    x1 = _layer_norm(alpha * x + ada[2:3] * y, lng_ref[0, 0:1], lnb_ref[0, 0:1])
    x1_ref[0] = x1
    u2 = _modulate(x1, ada, 3, 4).astype(BF16)
    bits = lax.bitcast_convert_type(u2.astype(F32), jnp.uint32)
    u2_ref[0] = (bits[:, :D // 2] >> 16) | bits[:, D // 2:]

    logits = _dot(u2, wrt_ref[0]) + brt_ref[0]
    lane = lax.broadcasted_iota(jnp.int32, logits.shape, 1)
    lane_f = lane.astype(F32)
    big = float(LANES)
    is_g = (lane >= N_EXPERTS) & (lane < N_EXPERTS + N_GROUPS)
    gmax = jnp.max(jnp.where(is_g, logits, NEG_INF), axis=1, keepdims=True)
    gsum = jnp.sum(jnp.where(is_g, jnp.exp(logits - gmax), 0.0), axis=1, keepdims=True)
    pg_top = 1.0 / gsum
    g_lane = jnp.min(jnp.where(is_g & (logits == gmax), lane_f, big), axis=1, keepdims=True)
    g_idx = g_lane.astype(jnp.int32) - N_EXPERTS
    in_grp = (lane < N_EXPERTS) & ((lane // EXP_PER_GROUP) == g_idx)
    el = jnp.where(in_grp, logits, NEG_INF)
    v1 = jnp.max(el, axis=1, keepdims=True)
    i1 = jnp.min(jnp.where(in_grp & (el == v1), lane_f, big), axis=1, keepdims=True)
    rest = in_grp & (lane_f != i1)
    el2 = jnp.where(rest, logits, NEG_INF)
    v2 = jnp.max(el2, axis=1, keepdims=True)
    i2 = jnp.min(jnp.where(rest & (el2 == v2), lane_f, big), axis=1, keepdims=True)
    e = jnp.exp(v2 - v1)
    w1 = pg_top / (1.0 + e)
    w2 = pg_top * e / (1.0 + e)
    eid_ref[0] = jnp.where(lane == 0, i1, jnp.where(lane == 1, i2, 0.0)).astype(jnp.int32)
    wts_ref[0] = jnp.where(lane == 0, w1, jnp.where(lane == 1, w2, 0.0))


def _merge_call(l, x, ada, hm, ys, og, wg, bb, wb, wo, lng, lnb, wrt, brt, tm, alpha):
    B, T, D = x.shape
    bmap = lambda b, t: (b, 0, 0)
    lmap = lambda b, t: (l, 0, 0)
    tmap = lambda b, t: (b, t, 0)
    return pl.pallas_call(
        functools.partial(_merge_kernel, alpha=alpha),
        grid=(B, T // tm),
        in_specs=[pl.BlockSpec((1, tm, D), tmap),
                  pl.BlockSpec((1, 6, D), bmap),
                  pl.BlockSpec((1, tm, D), tmap),
                  pl.BlockSpec((1, tm, D), tmap),
                  pl.BlockSpec((1, tm, D), tmap),
                  pl.BlockSpec((1, D, 3 * D), lmap),
                  pl.BlockSpec((1, 1, 3 * D), lmap),
                  pl.BlockSpec((1, 3, D, D), lambda b, t: (l, 0, 0, 0)),
                  pl.BlockSpec((1, D, D), lmap),
                  pl.BlockSpec((1, 2, D), lmap),
                  pl.BlockSpec((1, 2, D), lmap),
                  pl.BlockSpec((1, D, LANES), lmap),
                  pl.BlockSpec((1, 1, LANES), lmap)],
        out_specs=[pl.BlockSpec((1, tm, D), tmap),
                   pl.BlockSpec((1, tm, D // 2), tmap),
                   pl.BlockSpec((1, tm, LANES), tmap),
                   pl.BlockSpec((1, tm, LANES), tmap)],
        out_shape=[jax.ShapeDtypeStruct((B, T, D), F32),
                   jax.ShapeDtypeStruct((B, T, D // 2), jnp.uint32),
                   jax.ShapeDtypeStruct((B, T, LANES), jnp.int32),
                   jax.ShapeDtypeStruct((B, T, LANES), F32)],
        compiler_params=_cparams(2),
        name="merge",
    )(x, ada, hm, ys, og, wg, bb, wb, wo, lng, lnb, wrt, brt)


def _moe_kernel(te_ref, xs_ref, wg_ref, wu_ref, wd_ref, o_ref, wg_b, wu_b, wd_b):
    i = pl.program_id(0)
    prev = te_ref[jnp.maximum(i - 1, 0)]

    @pl.when((i == 0) | (te_ref[i] != prev))
    def _():
        wg_b[...] = wg_ref[0, 0].astype(BF16)
        wu_b[...] = wu_ref[0, 0].astype(BF16)
        wd_b[...] = wd_ref[0, 0].astype(BF16)

    w = xs_ref[...]
    lo = lax.bitcast_convert_type(w << 16, F32)
    hi = lax.bitcast_convert_type(w & jnp.uint32(0xFFFF0000), F32)
    xs = jnp.concatenate([lo, hi], axis=1).astype(BF16)
    hg = _dot(xs, wg_b[...])
    hu = _dot(xs, wu_b[...])
    o_ref[...] = _dot((_silu(hg) * hu).astype(BF16), wd_b[...])


def _moe_call(l, tile_expert, xs, weg, weu, wed):
    R = xs.shape[0]
    D, dexp = weg.shape[2], weg.shape[3]
    tm = MOE_TILE
    return pl.pallas_call(
        _moe_kernel,
        grid_spec=pltpu.PrefetchScalarGridSpec(
            num_scalar_prefetch=1,
            grid=(R // tm,),
            in_specs=[pl.BlockSpec((tm, D // 2), lambda i, te: (i, 0)),
                      pl.BlockSpec((1, 1, D, dexp), lambda i, te: (l, te[i], 0, 0)),
                      pl.BlockSpec((1, 1, D, dexp), lambda i, te: (l, te[i], 0, 0)),
                      pl.BlockSpec((1, 1, dexp, D), lambda i, te: (l, te[i], 0, 0))],
            out_specs=pl.BlockSpec((tm, D), lambda i, te: (i, 0)),
            scratch_shapes=[pltpu.VMEM((D, dexp), BF16), pltpu.VMEM((D, dexp), BF16), pltpu.VMEM((dexp, D), BF16)]),
        out_shape=jax.ShapeDtypeStruct((R, D), F32),
        compiler_params=_cparams(1),
        name="moe",
    )(tile_expert, xs, weg, weu, wed)


def _ln2_kernel(x1_ref, ada_ref, y0_ref, y1_ref, wts_ref, lng_ref, lnb_ref, o_ref, *, alpha):
    ada = ada_ref[0]
    wts = wts_ref[0]
    moe = y0_ref[0] * wts[:, 0:1] + y1_ref[0] * wts[:, 1:2]
    o_ref[0] = _layer_norm(alpha * x1_ref[0] + ada[5:6] * moe, lng_ref[0, 1:2], lnb_ref[0, 1:2])


def _ln2_call(l, x1, ada, y0, y1, wts, lng, lnb, tm, alpha):
    B, T, D = x1.shape
    bmap = lambda b, t: (b, 0, 0)
    lmap = lambda b, t: (l, 0, 0)
    tmap = lambda b, t: (b, t, 0)
    return pl.pallas_call(
        functools.partial(_ln2_kernel, alpha=alpha),
        grid=(B, T // tm),
        in_specs=[pl.BlockSpec((1, tm, D), tmap),
                  pl.BlockSpec((1, 6, D), bmap),
                  pl.BlockSpec((1, tm, D), tmap),
                  pl.BlockSpec((1, tm, D), tmap),
                  pl.BlockSpec((1, tm, LANES), tmap),
                  pl.BlockSpec((1, 2, D), lmap),
                  pl.BlockSpec((1, 2, D), lmap)],
        out_specs=pl.BlockSpec((1, tm, D), tmap),
        out_shape=jax.ShapeDtypeStruct((B, T, D), F32),
        compiler_params=_cparams(2),
        name="ln2",
    )(x1, ada, y0, y1, wts, lng, lnb)


def _take_rows(a, idx):
    return a.at[idx].get(mode="promise_in_bounds")


def _lookup(table, idx):
    sel = idx[:, None] == jnp.arange(table.shape[0], dtype=idx.dtype)[None, :]
    return jnp.sum(jnp.where(sel, table[None, :], 0), axis=1)


def _route(eid, n_tok):
    tm = MOE_TILE
    flat = eid.reshape(-1)
    n_asg = flat.shape[0]
    n_rows = (-(-n_asg // tm) + N_EXPERTS) * tm
    order = jnp.argsort(flat).astype(jnp.int32)
    inv = jnp.argsort(order).astype(jnp.int32)
    experts = jnp.arange(N_EXPERTS, dtype=jnp.int32)
    sizes = jnp.sum((flat[:, None] == experts[None, :]).astype(jnp.int32), axis=0)
    off = jnp.cumsum(sizes) - sizes
    psz = ((sizes + tm - 1) // tm) * tm
    pend = jnp.cumsum(psz)
    poff = pend - psz
    dest = _lookup(poff - off, flat) + inv
    tile_start = jnp.arange(n_rows // tm, dtype=jnp.int32) * tm
    tile_expert = jnp.minimum(jnp.sum((pend[None, :] <= tile_start[:, None]).astype(jnp.int32), axis=1),
                              N_EXPERTS - 1)
    within = jnp.arange(n_rows, dtype=jnp.int32) - jnp.repeat(poff[tile_expert], tm)
    valid = within < jnp.repeat(sizes[tile_expert], tm)
    pos = jnp.clip(jnp.repeat(off[tile_expert], tm) + within, 0, n_asg - 1)
    row_tok = jnp.where(valid, _take_rows(order, pos) // TOP_K, 0)
    return row_tok, dest.reshape(n_tok, TOP_K), tile_expert


def _pad_lanes(a, width=LANES):
    return jnp.pad(a, [(0, 0)] * (a.ndim - 1) + [(0, width - a.shape[-1])])


def _time_block(T):
    return min(T, 256)


def _merge_rows(T):
    return min(T, 512)


def _ln2_rows(T):
    return min(T, 1024)


def _trunk(x, ada_all, states, P, depth, alpha):
    B, T, D = x.shape
    Tt = _time_block(T)
    c_m, n_m, m_m, cv_m, h_s, cv_s, s_g = states
    new = [[] for _ in range(7)]
    for l in range(depth):
        ada = ada_all[l].reshape(B, 6, D)
        c0t = jnp.swapaxes(c_m[l], -1, -2)
        m0 = _pad_lanes(m_m[l])[:, None, :]
        h0t = (h_s[l].reshape(B, G_S, HPG_S, P_S, N_S).transpose(0, 1, 4, 2, 3)
               .reshape(B, G_S, N_S, HPG_S * P_S))
        hm, c_t, n_n, m_n, cvm_n = _mlstm_call(l, x, ada, P["w_mlstm"], P["mlstm_conv_w"], P["mlstm_conv_b"],
                                               P["mlstm_bif"], P["mlstm_norm_g"], c0t, n_m[l], m0, cv_m[l], Tt)
        ys, h_t, cvs_n = _ssd_call(l, x, ada, P["w_ssd"], P["ssd_conv_w"], P["ssd_conv_b"], P["ssd_dtb"],
                                   P["ssd_alog"], P["ssd_dfull"], P["ssd_norm_g"], h0t, cv_s[l], Tt)
        og, s_n = _gla_call(l, x, ada, P["w_gla"], P["gla_w_lr"], P["gla_b_lr"], P["gla_norm_g"], s_g[l], Tt)
        x1, u2, eid, wts = _merge_call(l, x, ada, hm, ys, og, P["w_gate"], P["b_branch"], P["w_branch"],
                                       P["w_out"], P["ln_g"], P["ln_b"], P["w_rt"], P["b_rt"], _merge_rows(T), alpha)
        row_tok, dest, tile_expert = _route(eid[:, :, :TOP_K], B * T)
        xs = _take_rows(u2.reshape(B * T, D // 2), row_tok)
        ye = _moe_call(l, tile_expert, xs, P["w_e_gate"], P["w_e_up"], P["w_e_down"])
        y0 = _take_rows(ye, dest[:, 0]).reshape(B, T, D)
        y1 = _take_rows(ye, dest[:, 1]).reshape(B, T, D)
        x = _ln2_call(l, x1, ada, y0, y1, wts, P["ln_g"], P["ln_b"], _ln2_rows(T), alpha)

        new[0].append(jnp.swapaxes(c_t, -1, -2))
        new[1].append(n_n)
        new[2].append(m_n[:, 0, :NH_M])
        new[3].append(cvm_n)
        new[4].append(h_t.reshape(B, G_S, N_S, HPG_S, P_S).transpose(0, 1, 3, 4, 2).reshape(B, NH_S, P_S, N_S))
        new[5].append(cvs_n)
        new[6].append(s_n)
    return x, tuple(jnp.stack(lst) for lst in new)


def kernel(x_prompt, x_sample, state_mlstm_c, state_mlstm_n, state_mlstm_m, state_mlstm_conv, state_ssd, state_ssd_conv, state_gla, c_prompt, c_sample, w_ada, b_ada, w_in, mlstm_b_i, mlstm_b_f, mlstm_conv_w, mlstm_conv_b, mlstm_norm_g, ssd_conv_w, ssd_conv_b, ssd_dt_bias, ssd_a_log, ssd_d, ssd_norm_g, gla_w_lr, gla_b_lr, gla_norm_g, b_branch, w_branch, w_out, ln_g, ln_b, w_grp, b_grp, w_router, b_router, w_e_gate, w_e_up, w_e_down):
    depth, D, _ = w_in.shape
    alpha = (2 * depth) ** 0.25
    nbp = x_prompt.shape[0]

    edges = np.concatenate([[0], np.cumsum(COL_SIZES)])
    col = {n: w_in[:, :, int(edges[i]):int(edges[i + 1])] for i, n in enumerate(COL_NAMES)}
    w_gate = w_in[:, :, int(edges[-1]):].astype(BF16)
    cat = lambda parts: jnp.concatenate(parts, axis=-1).astype(BF16)
    row = lambda a: a[:, None, :]
    P = {
        "w_mlstm": cat([col["qk_m"], col["v_m"], col["o_m"], _pad_lanes(jnp.concatenate([col["i_m"], col["f_m"]], -1))]),
        "w_ssd": cat([col["z_s"], col["xbc_s"], _pad_lanes(col["dt_s"])]),
        "w_gla": cat([col["q_g"], col["k_g"], col["v_g"], col["g_g"], _pad_lanes(col["lr_g"])]),
        "w_gate": w_gate,
        "mlstm_conv_w": mlstm_conv_w, "mlstm_conv_b": row(mlstm_conv_b),
        "mlstm_bif": row(_pad_lanes(jnp.concatenate([mlstm_b_i, mlstm_b_f], -1))),
        "mlstm_norm_g": row(mlstm_norm_g),
        "ssd_conv_w": ssd_conv_w, "ssd_conv_b": row(ssd_conv_b),
        "ssd_dtb": row(_pad_lanes(ssd_dt_bias)), "ssd_alog": row(_pad_lanes(ssd_a_log)),
        "ssd_dfull": row(jnp.repeat(ssd_d, P_S, axis=-1)), "ssd_norm_g": row(ssd_norm_g),
        "gla_w_lr": jnp.pad(gla_w_lr, ((0, 0), (0, LANES - R_G), (0, 0))).astype(BF16),
        "gla_b_lr": row(gla_b_lr), "gla_norm_g": row(gla_norm_g),
        "b_branch": row(b_branch), "w_branch": w_branch.astype(BF16), "w_out": w_out.astype(BF16),
        "ln_g": ln_g, "ln_b": ln_b,
        "w_rt": _pad_lanes(jnp.concatenate([w_router, w_grp], -1)).astype(BF16),
        "b_rt": row(_pad_lanes(jnp.concatenate([b_router, b_grp], -1))),
        "w_e_gate": w_e_gate, "w_e_up": w_e_up, "w_e_down": w_e_down,
    }

    ada_all = _ada_call(jnp.concatenate([c_prompt, c_sample], axis=0), w_ada, b_ada)

    sample_states = (state_mlstm_c, state_mlstm_n, state_mlstm_m, state_mlstm_conv,
                     state_ssd, state_ssd_conv, state_gla)
    prompt_states = tuple(jnp.zeros((s.shape[0], nbp) + s.shape[2:], x_prompt.dtype) for s in sample_states)

    y_prompt, new_p = _trunk(x_prompt, ada_all[:, :nbp], prompt_states, P, depth, alpha)
    y_sample, new_s = _trunk(x_sample, ada_all[:, nbp:], sample_states, P, depth, alpha)
    return (y_prompt, y_sample) + new_p + new_s
```

```python
import functools

import jax
import jax.numpy as jnp
import numpy as np
from jax import lax
from jax.experimental import pallas as pl
from jax.experimental.pallas import tpu as pltpu

F32 = jnp.float32
BF16 = jnp.bfloat16
NEG_INF = float("-inf")

CHUNK = 64
CONV_W = 4
NH_M, DQK_M, DV_M = 4, 128, 256
QK_M, VM = NH_M * DQK_M, NH_M * DV_M
NH_S, P_S, N_S, G_S = 16, 64, 128, 4
HPG_S = NH_S // G_S
XS, BC_S = NH_S * P_S, G_S * N_S
CONV_S_DIM = XS + 2 * BC_S
NH_G, DK_G, DV_G = 4, 128, 256
QK_G, VG = NH_G * DK_G, NH_G * DV_G
R_G = 16
TAU_G = 16.0
N_GROUPS, EXP_PER_GROUP, TOP_K = 4, 8, 2
N_EXPERTS = N_GROUPS * EXP_PER_GROUP
LN_EPS = 1e-5
LANES = 128

COL_SIZES = (2 * QK_M, VM, VM, NH_M, NH_M, XS, CONV_S_DIM, NH_S, QK_G, QK_G, VG, VG, R_G)
COL_NAMES = ("qk_m", "v_m", "o_m", "i_m", "f_m", "z_s", "xbc_s", "dt_s", "q_g", "k_g", "v_g", "g_g", "lr_g")

VMEM_LIMIT = 56 * 1024 * 1024
MOE_TILE = 512


def _cparams(n_axes):
    return pltpu.CompilerParams(dimension_semantics=("arbitrary",) * n_axes, vmem_limit_bytes=VMEM_LIMIT)


def _dot(a, b):
    return jnp.dot(a, b, preferred_element_type=F32)


def _dot_nt(a, b):
    return lax.dot_general(a, b, (((1,), (1,)), ((), ())), preferred_element_type=F32)


def _split3(x):
    hi = x.astype(BF16)
    r = x - hi.astype(F32)
    mid = r.astype(BF16)
    lo = (r - mid.astype(F32)).astype(BF16)
    return hi, mid, lo


def _eye(n, m):
    r = lax.broadcasted_iota(jnp.int32, (n, m), 0)
    c = lax.broadcasted_iota(jnp.int32, (n, m), 1)
    return jnp.where(r == c, 1.0, 0.0).astype(BF16)


def _transpose_rows(x, n):
    e = _eye(n, x.shape[1])
    hi, mid, lo = _split3(x)
    return _dot_nt(e, lo) + _dot_nt(e, mid) + _dot_nt(e, hi)


def _block_tril(Tt, L):
    r = lax.broadcasted_iota(jnp.int32, (Tt, Tt), 0)
    c = lax.broadcasted_iota(jnp.int32, (Tt, Tt), 1)
    return jnp.where(((r // L) == (c // L)) & (c <= r), 1.0, 0.0).astype(BF16)


def _block_full(Tt, L):
    r = lax.broadcasted_iota(jnp.int32, (Tt, Tt), 0)
    c = lax.broadcasted_iota(jnp.int32, (Tt, Tt), 1)
    return jnp.where((r // L) == (c // L), 1.0, 0.0).astype(BF16)


def _chunk_last(x, L):
    n = x.shape[0] // L
    return jnp.concatenate([jnp.broadcast_to(x[(c + 1) * L - 1:(c + 1) * L, :], (L, x.shape[1])) for c in range(n)], axis=0)


def _transpose_f32(x):
    if x.shape[0] % LANES == 0:
        return x.T
    return _transpose_rows(x, x.shape[1])


def _log_sigmoid(x):
    return jnp.minimum(x, 0.0) - jnp.log1p(jnp.exp(-jnp.abs(x)))


def _softplus(x):
    return jnp.maximum(x, 0.0) + jnp.log1p(jnp.exp(-jnp.abs(x)))


def _silu(x):
    return x * jax.nn.sigmoid(x)


def _modulate(x, ada, shift_row, scale_row):
    return x * (1.0 + ada[scale_row:scale_row + 1]) + ada[shift_row:shift_row + 1]


def _causal_conv(buf, x, w_ref, b_ref, Tt):
    buf[8:8 + Tt, :] = x
    y = buf[5:5 + Tt, :] * w_ref[0, 0:1, :]
    y = y + buf[6:6 + Tt, :] * w_ref[0, 1:2, :]
    y = y + buf[7:7 + Tt, :] * w_ref[0, 2:3, :]
    y = y + x * w_ref[0, 3:4, :]
    y = y + b_ref[0]
    buf[5:8, :] = buf[5 + Tt:8 + Tt, :]
    return y


def _group_norm(x, n_groups, center):
    w = x.shape[1] // n_groups
    outs = []
    for g in range(n_groups):
        xg = x[:, g * w:(g + 1) * w]
        if center:
            xg = xg - jnp.mean(xg, axis=1, keepdims=True)
        outs.append(xg * lax.rsqrt(jnp.mean(xg * xg, axis=1, keepdims=True) + LN_EPS))
    return jnp.concatenate(outs, axis=1)


def _ada_kernel(c_ref, w_ref, b_ref, o_ref):
    c = _silu(c_ref[...]).astype(BF16)
    o_ref[0] = _dot(c, w_ref[0].astype(BF16)) + b_ref[0]


def _ada_call(c_all, w_ada, b_ada):
    depth, d, n6 = w_ada.shape
    nb = c_all.shape[0]
    tn = 1536
    return pl.pallas_call(
        _ada_kernel,
        grid=(depth, n6 // tn),
        in_specs=[pl.BlockSpec((nb, d), lambda l, j: (0, 0)),
                  pl.BlockSpec((1, d, tn), lambda l, j: (l, 0, j)),
                  pl.BlockSpec((1, 1, tn), lambda l, j: (l, 0, j))],
        out_specs=pl.BlockSpec((1, nb, tn), lambda l, j: (l, 0, j)),
        out_shape=jax.ShapeDtypeStruct((depth, nb, n6), F32),
        compiler_params=_cparams(2),
        name="ada",
    )(c_all, w_ada, b_ada.reshape(depth, 1, n6))


def _mlstm_kernel(x_ref, ada_ref, w_ref, cw_ref, cb_ref, bif_ref, ng_ref, c0_ref, n0_ref, m0_ref, cv0_ref,
                  out_ref, c_out, n_out, m_out, cv_out,
                  conv_s, q_s, qb_s, kb_s, v_s, o_s, kwt_s, un_s, nu_s, p_s, rs_s, b_s, ml_s, n_s, m_s, *ct_s, Tt, L):
    t = pl.program_id(1)
    nc = Tt // L

    @pl.when(t == 0)
    def _():
        for h in range(NH_M):
            ct_s[h][...] = c0_ref[0, h]
        n_s[...] = n0_ref[0]
        m_s[...] = m0_ref[0]
        conv_s[5:8, :] = cv0_ref[0]

    u = _modulate(x_ref[0], ada_ref[0], 0, 1).astype(BF16)
    g = _dot(u, w_ref[0, :, 2 * QK_M + 2 * VM:]) + bif_ref[0]
    qk = _dot(u, w_ref[0, :, 0:2 * QK_M])
    lane = lax.broadcasted_iota(jnp.int32, g.shape, 1)
    g = jnp.where(lane < NH_M, g, _log_sigmoid(g))

    tril = _block_tril(Tt, L)
    hi, mid, lo = _split3(g)
    cs = _dot(tril, lo) + _dot(tril, mid) + _dot(tril, hi)
    v_s[...] = _dot(u, w_ref[0, :, 2 * QK_M:2 * QK_M + VM]).astype(BF16)
    qk = _silu(_causal_conv(conv_s, qk, cw_ref, cb_ref, Tt))
    q = qk[:, :QK_M]
    k = qk[:, QK_M:] * DQK_M ** -0.5
    q_s[...] = q
    qb_s[...] = q.astype(BF16)
    kb_s[...] = k.astype(BF16)
    o_s[...] = _dot(u, w_ref[0, :, 2 * QK_M + VM:2 * QK_M + 2 * VM])
    b0 = pltpu.roll(cs, LANES - NH_M, axis=1)
    r = g - b0
    row = lax.broadcasted_iota(jnp.int32, g.shape, 0) % L
    a = r
    sh = 1
    while sh < L:
        a = jnp.maximum(a, jnp.where(row >= sh, pltpu.roll(a, sh, axis=0), NEG_INF))
        sh *= 2
    al = a
    sh = 1
    while sh < L:
        al = jnp.maximum(al, jnp.where(row + sh < L, pltpu.roll(al, Tt - sh, axis=0), NEG_INF))
        sh *= 2
    b_s[...] = b0
    ml_s[...] = b0 + a
    wsrc = jnp.exp(r - al)
    r_rows = _transpose_f32(r)[0:8, :]
    col = lax.broadcasted_iota(jnp.int32, (DQK_M, Tt), 1)
    rr = lax.broadcasted_iota(jnp.int32, (L, L), 0)
    cc = lax.broadcasted_iota(jnp.int32, (L, L), 1)
    causal = cc <= rr
    for h in range(NH_M):
        ks = slice(h * DQK_M, (h + 1) * DQK_M)
        kw = wsrc[:, h:h + 1] * k[:, ks]
        kwt_s[h] = _transpose_f32(kw).astype(BF16)
        for c in range(nc):
            nu_s[c * NH_M + h:c * NH_M + h + 1, :] = jnp.sum(kw[c * L:(c + 1) * L], axis=0, keepdims=True)
    for c in range(nc):
        for h in range(NH_M):
            vs = slice(h * DV_M, (h + 1) * DV_M)
            kwm = jnp.where((col // L) == c, kwt_s[h], jnp.zeros((), BF16)) if nc > 1 else kwt_s[h]
            un_s[c * NH_M + h] = _dot(kwm, v_s[:, vs])
    for c in range(nc):
        rows = slice(c * L, (c + 1) * L)
        sl = []
        for h in range(NH_M):
            ks = slice(h * DQK_M, (h + 1) * DQK_M)
            dloc = jnp.exp(jnp.where(causal, r_rows[h:h + 1, rows] - a[rows, h:h + 1], NEG_INF))
            sl.append(_dot_nt(qb_s[rows, ks], kb_s[rows, ks]) * dloc)
        for h in range(NH_M):
            vs = slice(h * DV_M, (h + 1) * DV_M)
            p_s[rows, vs] = _dot(sl[h].astype(BF16), v_s[rows, vs])
            rs_s[rows, h:h + 1] = jnp.sum(sl[h], axis=1, keepdims=True)

    lane_l = lax.broadcasted_iota(jnp.int32, (L, LANES), 1)
    for c in range(nc):
        rows = slice(c * L, (c + 1) * L)
        m_prev = m_s[...]
        bc = b_s[rows, :]
        mlc = ml_s[rows, :]
        mt = jnp.maximum(bc + m_prev, mlc)
        corr = jnp.exp(mlc - mt)
        w_inter = jnp.exp(bc + m_prev - mt)
        inter = []
        qn = jnp.zeros((L, LANES), F32)
        for h in range(NH_M):
            ks = slice(h * DQK_M, (h + 1) * DQK_M)
            inter.append(_dot(qb_s[rows, ks], ct_s[h][...].astype(BF16)))
            qn_h = jnp.sum(q_s[rows, ks] * n_s[h:h + 1, :], axis=1, keepdims=True)
            qn = jnp.where(lane_l == h, qn_h, qn)
        den = corr * rs_s[rows, :] + w_inter * qn
        inv = 1.0 / jnp.maximum(jnp.abs(den), jnp.exp(-mt))
        ca = corr * inv
        cb2 = w_inter * inv
        for h in range(NH_M):
            vs = slice(h * DV_M, (h + 1) * DV_M)
            p_s[rows, vs] = ca[:, h:h + 1] * p_s[rows, vs] + cb2[:, h:h + 1] * inter[h]
        m_last = mt[L - 1:L, :]
        c_l = jnp.exp(mlc[L - 1:L, :] - m_last)
        decay = jnp.exp(bc[L - 1:L, :] + m_prev - m_last)
        for h in range(NH_M):
            ct_s[h][...] = decay[:, h:h + 1] * ct_s[h][...] + c_l[:, h:h + 1] * un_s[c * NH_M + h]
            n_s[h:h + 1, :] = decay[:, h:h + 1] * n_s[h:h + 1, :] + c_l[:, h:h + 1] * nu_s[c * NH_M + h:c * NH_M + h + 1, :]
        m_s[...] = m_last

    hn = _group_norm(p_s[...], NH_M, True) * ng_ref[0] * jax.nn.sigmoid(o_s[...])
    out_ref[0] = hn.astype(out_ref.dtype)

    @pl.when(t == pl.num_programs(1) - 1)
    def _():
        for h in range(NH_M):
            c_out[0, h] = ct_s[h][...]
        n_out[0] = n_s[...]
        m_out[0] = m_s[...]
        cv_out[0] = conv_s[5:8, :]


def _mlstm_call(l, x, ada, w, cw, cb, bif, ng, c0t, n0, m0, cv0, Tt):
    B, T, D = x.shape
    L = min(CHUNK, T)
    wcols = w.shape[2]
    bmap = lambda b, t: (b, 0, 0)
    lmap = lambda b, t: (l, 0, 0)
    return pl.pallas_call(
        functools.partial(_mlstm_kernel, Tt=Tt, L=L),
        grid=(B, T // Tt),
        in_specs=[pl.BlockSpec((1, Tt, D), lambda b, t: (b, t, 0)),
                  pl.BlockSpec((1, 6, D), bmap),
                  pl.BlockSpec((1, D, wcols), lmap),
                  pl.BlockSpec((1, CONV_W, 2 * QK_M), lmap),
                  pl.BlockSpec((1, 1, 2 * QK_M), lmap),
                  pl.BlockSpec((1, 1, LANES), lmap),
                  pl.BlockSpec((1, 1, VM), lmap),
                  pl.BlockSpec((1, NH_M, DQK_M, DV_M), lambda b, t: (b, 0, 0, 0)),
                  pl.BlockSpec((1, NH_M, DQK_M), bmap),
                  pl.BlockSpec((1, 1, LANES), bmap),
                  pl.BlockSpec((1, CONV_W - 1, 2 * QK_M), bmap)],
        out_specs=[pl.BlockSpec((1, Tt, VM), lambda b, t: (b, t, 0)),
                   pl.BlockSpec((1, NH_M, DQK_M, DV_M), lambda b, t: (b, 0, 0, 0)),
                   pl.BlockSpec((1, NH_M, DQK_M), bmap),
                   pl.BlockSpec((1, 1, LANES), bmap),
                   pl.BlockSpec((1, CONV_W - 1, 2 * QK_M), bmap)],
        out_shape=[jax.ShapeDtypeStruct((B, T, VM), BF16),
                   jax.ShapeDtypeStruct((B, NH_M, DQK_M, DV_M), F32),
                   jax.ShapeDtypeStruct((B, NH_M, DQK_M), F32),
                   jax.ShapeDtypeStruct((B, 1, LANES), F32),
                   jax.ShapeDtypeStruct((B, CONV_W - 1, 2 * QK_M), F32)],
        scratch_shapes=[pltpu.VMEM((8 + Tt, 2 * QK_M), F32),
                        pltpu.VMEM((Tt, QK_M), F32),
                        pltpu.VMEM((Tt, QK_M), BF16),
                        pltpu.VMEM((Tt, QK_M), BF16),
                        pltpu.VMEM((Tt, VM), BF16),
                        pltpu.VMEM((Tt, VM), F32),
                        pltpu.VMEM((NH_M, DQK_M, Tt), BF16),
                        pltpu.VMEM((Tt // L * NH_M, DQK_M, DV_M), F32),
                        pltpu.VMEM((Tt // L * NH_M, DQK_M), F32),
                        pltpu.VMEM((Tt, VM), F32),
                        pltpu.VMEM((Tt, LANES), F32),
                        pltpu.VMEM((Tt, LANES), F32),
                        pltpu.VMEM((Tt, LANES), F32),
                        pltpu.VMEM((NH_M, DQK_M), F32),
                        pltpu.VMEM((1, LANES), F32)] + [pltpu.VMEM((DQK_M, DV_M), F32)] * NH_M,
        compiler_params=_cparams(2),
        name="mlstm",
    )(x, ada, w, cw, cb, bif, ng, c0t, n0, m0, cv0)


def _pair_cols(x, hd, lane):
    return jnp.where(lane < P_S, x[:, hd:hd + 1], x[:, hd + 1:hd + 2])


def _ssd_kernel(x_ref, ada_ref, w_ref, cw_ref, cb_ref, dtb_ref, alog_ref, d_ref, ng_ref, h0_ref, cv0_ref,
                out_ref, h_out, cv_out,
                conv_s, zg_s, x_s, xw_s, b_s, c_s, bt_s, e2_s, un_s, dec_s, y_s, *ht_s, Tt, L):
    t = pl.program_id(1)
    nc = Tt // L
    n_pairs = NH_S // 2

    @pl.when(t == 0)
    def _():
        for g in range(G_S):
            ht_s[g][...] = h0_ref[0, g]
        conv_s[5:8, :] = cv0_ref[0]

    u = _modulate(x_ref[0], ada_ref[0], 0, 1).astype(BF16)
    dt = _softplus(_dot(u, w_ref[0, :, XS + CONV_S_DIM:]) + dtb_ref[0])
    xbc = _dot(u, w_ref[0, :, XS:XS + CONV_S_DIM])
    a_row = -jnp.exp(alog_ref[0])

    tril = _block_tril(Tt, L)
    full = _block_full(Tt, L)
    hi, mid, lo = _split3(dt * a_row)
    cs = _dot(tril, lo) + _dot(tril, mid) + _dot(tril, hi)
    csl = _dot(full, lo) + _dot(full, mid) + _dot(full, hi)
    xbc = _silu(_causal_conv(conv_s, xbc, cw_ref, cb_ref, Tt))
    x = xbc[:, :XS]
    bm = xbc[:, XS:XS + BC_S]
    x_s[...] = x
    b_s[...] = bm.astype(BF16)
    c_s[...] = xbc[:, XS + BC_S:].astype(BF16)
    wsrc = jnp.exp(csl - cs) * dt
    dec = jnp.exp(csl)
    cs_rows = _transpose_f32(cs)[0:NH_S, :]
    dt_rows = _transpose_f32(dt)[0:NH_S, :]
    for g in range(G_S):
        bt_s[g] = _transpose_f32(bm[:, g * N_S:(g + 1) * N_S]).astype(BF16)
    zg_s[...] = _silu(_dot(u, w_ref[0, :, 0:XS]))
    lane_t = lax.broadcasted_iota(jnp.int32, (Tt, LANES), 1)
    for pp in range(n_pairs):
        pc = slice(pp * LANES, (pp + 1) * LANES)
        e2_s[:, pc] = _pair_cols(cs, 2 * pp, lane_t)
        xw_s[:, pc] = (x[:, pc] * _pair_cols(wsrc, 2 * pp, lane_t)).astype(BF16)
    lane1 = lax.broadcasted_iota(jnp.int32, (1, LANES), 1)
    for c in range(nc):
        for pp in range(n_pairs):
            r0 = c * L
            dec_s[c * n_pairs + pp:c * n_pairs + pp + 1, :] = _pair_cols(dec[r0:r0 + 1, :], 2 * pp, lane1)

    rr = lax.broadcasted_iota(jnp.int32, (L, LANES), 0)
    lane_l = lax.broadcasted_iota(jnp.int32, (L, LANES), 1)
    causal2 = (lane_l % P_S) <= rr
    col = lax.broadcasted_iota(jnp.int32, (N_S, Tt), 1)
    zero_b = jnp.zeros((), BF16)
    for c in range(nc):
        rows = slice(c * L, (c + 1) * L)
        for g in range(G_S):
            gs = slice(g * N_S, (g + 1) * N_S)
            bg = b_s[rows, gs]
            cb2 = _dot_nt(c_s[rows, gs], jnp.concatenate([bg, bg], axis=0))
            btm = jnp.where((col // L) == c, bt_s[g], zero_b) if nc > 1 else bt_s[g]
            un_s[c * G_S + g] = _dot(btm, xw_s[:, g * 2 * LANES:(g + 1) * 2 * LANES])
            for p2 in range(2):
                pp = g * 2 + p2
                hd = 2 * pp
                pc = slice(pp * LANES, (pp + 1) * LANES)
                csr2 = jnp.concatenate([cs_rows[hd:hd + 1, rows], cs_rows[hd + 1:hd + 2, rows]], axis=1)
                dtr2 = jnp.concatenate([dt_rows[hd:hd + 1, rows], dt_rows[hd + 1:hd + 2, rows]], axis=1)
                seg2 = jnp.exp(jnp.where(causal2, e2_s[rows, pc] - csr2, NEG_INF))
                m2 = (cb2 * seg2 * dtr2).astype(BF16)
                xp = x_s[rows, pc].astype(BF16)
                xbd = jnp.concatenate([jnp.where(lane_l < P_S, xp, zero_b), jnp.where(lane_l >= P_S, xp, zero_b)], axis=0)
                y_s[rows, pc] = _dot(m2, xbd)

    for c in range(nc):
        rows = slice(c * L, (c + 1) * L)
        for g in range(G_S):
            gc = slice(g * 2 * LANES, (g + 1) * 2 * LANES)
            yi = _dot(c_s[rows, g * N_S:(g + 1) * N_S], ht_s[g][...].astype(BF16))
            y_s[rows, gc] += jnp.exp(e2_s[rows, gc]) * yi
        for g in range(G_S):
            for p2 in range(2):
                pp = g * 2 + p2
                hc = slice(p2 * LANES, (p2 + 1) * LANES)
                ht_s[g][:, hc] = (dec_s[c * n_pairs + pp:c * n_pairs + pp + 1, :] * ht_s[g][:, hc]
                                  + un_s[c * G_S + g][:, hc])

    y = (y_s[...] + d_ref[0] * x_s[...]) * zg_s[...]
    out_ref[0] = (_group_norm(y, G_S, False) * ng_ref[0]).astype(out_ref.dtype)

    @pl.when(t == pl.num_programs(1) - 1)
    def _():
        for g in range(G_S):
            h_out[0, g] = ht_s[g][...]
        cv_out[0] = conv_s[5:8, :]


def _ssd_call(l, x, ada, w, cw, cb, dtb, alog, dfull, ng, h0t, cv0, Tt):
    B, T, D = x.shape
    L = min(CHUNK, T)
    wcols = w.shape[2]
    bmap = lambda b, t: (b, 0, 0)
    lmap = lambda b, t: (l, 0, 0)
    hshape = (G_S, N_S, HPG_S * P_S)
    assert L == P_S and 2 * L == LANES and T % Tt == 0 and Tt % L == 0
    return pl.pallas_call(
        functools.partial(_ssd_kernel, Tt=Tt, L=L),
        grid=(B, T // Tt),
        in_specs=[pl.BlockSpec((1, Tt, D), lambda b, t: (b, t, 0)),
                  pl.BlockSpec((1, 6, D), bmap),
                  pl.BlockSpec((1, D, wcols), lmap),
                  pl.BlockSpec((1, CONV_W, CONV_S_DIM), lmap),
                  pl.BlockSpec((1, 1, CONV_S_DIM), lmap),
                  pl.BlockSpec((1, 1, LANES), lmap),
                  pl.BlockSpec((1, 1, LANES), lmap),
                  pl.BlockSpec((1, 1, XS), lmap),
                  pl.BlockSpec((1, 1, XS), lmap),
                  pl.BlockSpec((1,) + hshape, lambda b, t: (b, 0, 0, 0)),
                  pl.BlockSpec((1, CONV_W - 1, CONV_S_DIM), bmap)],
        out_specs=[pl.BlockSpec((1, Tt, XS), lambda b, t: (b, t, 0)),
                   pl.BlockSpec((1,) + hshape, lambda b, t: (b, 0, 0, 0)),
                   pl.BlockSpec((1, CONV_W - 1, CONV_S_DIM), bmap)],
        out_shape=[jax.ShapeDtypeStruct((B, T, XS), BF16),
                   jax.ShapeDtypeStruct((B,) + hshape, F32),
                   jax.ShapeDtypeStruct((B, CONV_W - 1, CONV_S_DIM), F32)],
        scratch_shapes=[pltpu.VMEM((8 + Tt, CONV_S_DIM), F32),
                        pltpu.VMEM((Tt, XS), F32),
                        pltpu.VMEM((Tt, XS), F32),
                        pltpu.VMEM((Tt, XS), BF16),
                        pltpu.VMEM((Tt, BC_S), BF16),
                        pltpu.VMEM((Tt, BC_S), BF16),
                        pltpu.VMEM((G_S, N_S, Tt), BF16),
                        pltpu.VMEM((Tt, XS), F32),
                        pltpu.VMEM((Tt // L * G_S, N_S, HPG_S * P_S), F32),
                        pltpu.VMEM((Tt // L * (NH_S // 2), LANES), F32),
                        pltpu.VMEM((Tt, XS), F32)] + [pltpu.VMEM(hshape[1:], F32)] * G_S,
        compiler_params=_cparams(2),
        name="ssd",
    )(x, ada, w, cw, cb, dtb, alog, dfull, ng, h0t, cv0)


def _gla_kernel(x_ref, ada_ref, w_ref, wlr_ref, blr_ref, ng_ref, s0_ref,
                out_ref, s_out,
                sg_s, qg_s, kg_s, v_s, kdt_s, egt_s, un_s, o_s, *st_s, Tt, L):
    t = pl.program_id(1)
    nc = Tt // L

    @pl.when(t == 0)
    def _():
        for h in range(NH_G):
            st_s[h][...] = s0_ref[0, h]

    u = _modulate(x_ref[0], ada_ref[0], 0, 1).astype(BF16)
    lr = _dot(u, w_ref[0, :, 2 * QK_G + 2 * VG:]).astype(BF16)
    lg = _log_sigmoid(_dot(lr, wlr_ref[0]) + blr_ref[0]) / TAU_G
    q = _dot(u, w_ref[0, :, 0:QK_G])
    k = _dot(u, w_ref[0, :, QK_G:2 * QK_G]) * DK_G ** -0.5

    tril = _block_tril(Tt, L)
    hi, mid, lo = _split3(lg)
    G = _dot(tril, lo) + _dot(tril, mid) + _dot(tril, hi)
    v_s[...] = _dot(u, w_ref[0, :, 2 * QK_G:2 * QK_G + VG]).astype(BF16)
    if nc > 1:
        Gl = _chunk_last(G, L)
    else:
        full = _block_full(Tt, L)
        Gl = _dot(full, lo) + _dot(full, mid) + _dot(full, hi)
    qg_s[...] = (q * jnp.exp(G)).astype(BF16)
    kg_s[...] = (k * jnp.exp(-G)).astype(BF16)
    kd = k * jnp.exp(Gl - G)
    eg = jnp.exp(Gl)
    sg_s[...] = _silu(_dot(u, w_ref[0, :, 2 * QK_G + VG:2 * QK_G + 2 * VG]))
    for h in range(NH_G):
        ks = slice(h * DK_G, (h + 1) * DK_G)
        kdt_s[h] = _transpose_f32(kd[:, ks]).astype(BF16)
        egt_s[h] = _transpose_f32(eg[:, ks])
    col = lax.broadcasted_iota(jnp.int32, (DK_G, Tt), 1)
    for c in range(nc):
        for h in range(NH_G):
            vs = slice(h * DV_G, (h + 1) * DV_G)
            kdm = jnp.where((col // L) == c, kdt_s[h], jnp.zeros((), BF16)) if nc > 1 else kdt_s[h]
            un_s[c * NH_G + h] = _dot(kdm, v_s[:, vs])

    rr = lax.broadcasted_iota(jnp.int32, (L, L), 0)
    cc = lax.broadcasted_iota(jnp.int32, (L, L), 1)
    causal = cc <= rr
    for c in range(nc):
        rows = slice(c * L, (c + 1) * L)
        att = []
        for h in range(NH_G):
            ks = slice(h * DK_G, (h + 1) * DK_G)
            att.append(_dot_nt(qg_s[rows, ks], kg_s[rows, ks]))
        for h in range(NH_G):
            ks = slice(h * DK_G, (h + 1) * DK_G)
            vs = slice(h * DV_G, (h + 1) * DV_G)
            o_s[rows, vs] = _dot(qg_s[rows, ks], st_s[h][...].astype(BF16))
        for h in range(NH_G):
            vs = slice(h * DV_G, (h + 1) * DV_G)
            o_s[rows, vs] += _dot(jnp.where(causal, att[h], 0.0).astype(BF16), v_s[rows, vs])
        for h in range(NH_G):
            st_s[h][...] = egt_s[h][:, c * L:c * L + 1] * st_s[h][...] + un_s[c * NH_G + h]

    out_ref[0] = (_group_norm(o_s[...], NH_G, False) * ng_ref[0] * sg_s[...]).astype(out_ref.dtype)

    @pl.when(t == pl.num_programs(1) - 1)
    def _():
        for h in range(NH_G):
            s_out[0, h] = st_s[h][...]


def _gla_call(l, x, ada, w, wlr, blr, ng, s0, Tt):
    B, T, D = x.shape
    L = min(CHUNK, T)
    wcols = w.shape[2]
    bmap = lambda b, t: (b, 0, 0)
    lmap = lambda b, t: (l, 0, 0)
    sshape = (NH_G, DK_G, DV_G)
    return pl.pallas_call(
        functools.partial(_gla_kernel, Tt=Tt, L=L),
        grid=(B, T // Tt),
        in_specs=[pl.BlockSpec((1, Tt, D), lambda b, t: (b, t, 0)),
                  pl.BlockSpec((1, 6, D), bmap),
                  pl.BlockSpec((1, D, wcols), lmap),
                  pl.BlockSpec((1, LANES, QK_G), lmap),
                  pl.BlockSpec((1, 1, QK_G), lmap),
                  pl.BlockSpec((1, 1, VG), lmap),
                  pl.BlockSpec((1,) + sshape, lambda b, t: (b, 0, 0, 0))],
        out_specs=[pl.BlockSpec((1, Tt, VG), lambda b, t: (b, t, 0)),
                   pl.BlockSpec((1,) + sshape, lambda b, t: (b, 0, 0, 0))],
        out_shape=[jax.ShapeDtypeStruct((B, T, VG), BF16),
                   jax.ShapeDtypeStruct((B,) + sshape, F32)],
        scratch_shapes=[pltpu.VMEM((Tt, VG), F32),
                        pltpu.VMEM((Tt, QK_G), BF16),
                        pltpu.VMEM((Tt, QK_G), BF16),
                        pltpu.VMEM((Tt, VG), BF16),
                        pltpu.VMEM((NH_G, DK_G, Tt), BF16),
                        pltpu.VMEM((NH_G, DK_G, Tt), F32),
                        pltpu.VMEM((Tt // L * NH_G, DK_G, DV_G), F32),
                        pltpu.VMEM((Tt, VG), F32)] + [pltpu.VMEM(sshape[1:], F32)] * NH_G,
        compiler_params=_cparams(2),
        name="gla",
    )(x, ada, w, wlr, blr, ng, s0)


def _layer_norm(x, g, b):
    mu = jnp.mean(x, axis=1, keepdims=True)
    xc = x - mu
    var = jnp.mean(xc * xc, axis=1, keepdims=True)
    return xc * lax.rsqrt(var + LN_EPS) * g + b


def _merge_kernel(x_ref, ada_ref, hm_ref, ys_ref, og_ref, wg_ref, bb_ref, wb_ref, wo_ref, lng_ref, lnb_ref,
                  wrt_ref, brt_ref,
                  x1_ref, u2_ref, eid_ref, wts_ref, cnt_ref, cnt_s, *, alpha):
    x = x_ref[0]
    ada = ada_ref[0]
    D = x.shape[1]
    u = _modulate(x, ada, 0, 1).astype(BF16)
    gate = jax.nn.sigmoid(_dot(u, wg_ref[0]) + bb_ref[0])
    merged = (gate[:, 0:D] * _dot(hm_ref[0], wb_ref[0, 0])
              + gate[:, D:2 * D] * _dot(ys_ref[0], wb_ref[0, 1])
              + gate[:, 2 * D:3 * D] * _dot(og_ref[0], wb_ref[0, 2]))
    y = _dot(merged.astype(BF16), wo_ref[0])
    x1 = _layer_norm(alpha * x + ada[2:3] * y, lng_ref[0, 0:1], lnb_ref[0, 0:1])
    x1_ref[0] = x1
    u2 = _modulate(x1, ada, 3, 4).astype(BF16)
    bits = lax.bitcast_convert_type(u2.astype(F32), jnp.uint32)
    u2_ref[0] = (bits[:, :D // 2] >> 16) | bits[:, D // 2:]

    logits = _dot(u2, wrt_ref[0]) + brt_ref[0]
    lane = lax.broadcasted_iota(jnp.int32, logits.shape, 1)
    lane_f = lane.astype(F32)
    big = float(LANES)
    is_g = (lane >= N_EXPERTS) & (lane < N_EXPERTS + N_GROUPS)
    gmax = jnp.max(jnp.where(is_g, logits, NEG_INF), axis=1, keepdims=True)
    gsum = jnp.sum(jnp.where(is_g, jnp.exp(logits - gmax), 0.0), axis=1, keepdims=True)
    pg_top = 1.0 / gsum
    g_lane = jnp.min(jnp.where(is_g & (logits == gmax), lane_f, big), axis=1, keepdims=True)
    g_idx = g_lane.astype(jnp.int32) - N_EXPERTS
    in_grp = (lane < N_EXPERTS) & ((lane // EXP_PER_GROUP) == g_idx)
    el = jnp.where(in_grp, logits, NEG_INF)
    v1 = jnp.max(el, axis=1, keepdims=True)
    i1 = jnp.min(jnp.where(in_grp & (el == v1), lane_f, big), axis=1, keepdims=True)
    rest = in_grp & (lane_f != i1)
    el2 = jnp.where(rest, logits, NEG_INF)
    v2 = jnp.max(el2, axis=1, keepdims=True)
    i2 = jnp.min(jnp.where(rest & (el2 == v2), lane_f, big), axis=1, keepdims=True)
    e = jnp.exp(v2 - v1)
    w1 = pg_top / (1.0 + e)
    w2 = pg_top * e / (1.0 + e)
    @pl.when((pl.program_id(0) == 0) & (pl.program_id(1) == 0))
    def _():
        cnt_s[...] = jnp.zeros_like(cnt_s)

    tm = logits.shape[0]
    hit1 = lane_f == i1
    hit2 = lane_f == i2
    onehot = jnp.where(hit1 | hit2, 1.0, 0.0)
    rr = lax.broadcasted_iota(jnp.int32, (tm, tm), 0)
    cc = lax.broadcasted_iota(jnp.int32, (tm, tm), 1)
    before = _dot(jnp.where(cc < rr, 1.0, 0.0).astype(BF16), onehot.astype(BF16)) + cnt_s[...]
    r1 = jnp.sum(jnp.where(hit1, before, 0.0), axis=1, keepdims=True)
    r2 = jnp.sum(jnp.where(hit2, before, 0.0), axis=1, keepdims=True)
    cnt_s[...] = cnt_s[...] + jnp.sum(onehot, axis=0, keepdims=True)
    cnt_ref[0] = cnt_s[...]
    eid_ref[0] = jnp.where(lane == 0, i1, jnp.where(lane == 1, i2, jnp.where(lane == 2, r1, jnp.where(lane == 3, r2, 0.0)))).astype(jnp.int32)
    wts_ref[0] = jnp.where(lane == 0, w1, jnp.where(lane == 1, w2, 0.0))


def _merge_call(l, x, ada, hm, ys, og, wg, bb, wb, wo, lng, lnb, wrt, brt, tm, alpha):
    B, T, D = x.shape
    bmap = lambda b, t: (b, 0, 0)
    lmap = lambda b, t: (l, 0, 0)
    tmap = lambda b, t: (b, t, 0)
    return pl.pallas_call(
        functools.partial(_merge_kernel, alpha=alpha),
        grid=(B, T // tm),
        in_specs=[pl.BlockSpec((1, tm, D), tmap),
                  pl.BlockSpec((1, 6, D), bmap),
                  pl.BlockSpec((1, tm, D), tmap),
                  pl.BlockSpec((1, tm, D), tmap),
                  pl.BlockSpec((1, tm, D), tmap),
                  pl.BlockSpec((1, D, 3 * D), lmap),
                  pl.BlockSpec((1, 1, 3 * D), lmap),
                  pl.BlockSpec((1, 3, D, D), lambda b, t: (l, 0, 0, 0)),
                  pl.BlockSpec((1, D, D), lmap),
                  pl.BlockSpec((1, 2, D), lmap),
                  pl.BlockSpec((1, 2, D), lmap),
                  pl.BlockSpec((1, D, LANES), lmap),
                  pl.BlockSpec((1, 1, LANES), lmap)],
        out_specs=[pl.BlockSpec((1, tm, D), tmap),
                   pl.BlockSpec((1, tm, D // 2), tmap),
                   pl.BlockSpec((1, tm, LANES), tmap),
                   pl.BlockSpec((1, tm, LANES), tmap),
                   pl.BlockSpec((1, 1, LANES), lambda b, t: (0, 0, 0))],
        out_shape=[jax.ShapeDtypeStruct((B, T, D), F32),
                   jax.ShapeDtypeStruct((B, T, D // 2), jnp.uint32),
                   jax.ShapeDtypeStruct((B, T, LANES), jnp.int32),
                   jax.ShapeDtypeStruct((B, T, LANES), F32),
                   jax.ShapeDtypeStruct((1, 1, LANES), F32)],
        scratch_shapes=[pltpu.VMEM((1, LANES), F32)],
        compiler_params=_cparams(2),
        name="merge",
    )(x, ada, hm, ys, og, wg, bb, wb, wo, lng, lnb, wrt, brt)


def _moe_kernel(te_ref, xs_ref, wg_ref, wu_ref, wd_ref, o_ref, wg_b, wu_b, wd_b):
    i = pl.program_id(0)
    prev = te_ref[jnp.maximum(i - 1, 0)]

    @pl.when((i == 0) | (te_ref[i] != prev))
    def _():
        wg_b[...] = wg_ref[0, 0].astype(BF16)
        wu_b[...] = wu_ref[0, 0].astype(BF16)
        wd_b[...] = wd_ref[0, 0].astype(BF16)

    w = xs_ref[...]
    lo = lax.bitcast_convert_type(w << 16, F32)
    hi = lax.bitcast_convert_type(w & jnp.uint32(0xFFFF0000), F32)
    xs = jnp.concatenate([lo, hi], axis=1).astype(BF16)
    hg = _dot(xs, wg_b[...])
    hu = _dot(xs, wu_b[...])
    o_ref[...] = _dot((_silu(hg) * hu).astype(BF16), wd_b[...])


def _moe_call(l, tile_expert, xs, weg, weu, wed):
    R = xs.shape[0]
    D, dexp = weg.shape[2], weg.shape[3]
    tm = MOE_TILE
    return pl.pallas_call(
        _moe_kernel,
        grid_spec=pltpu.PrefetchScalarGridSpec(
            num_scalar_prefetch=1,
            grid=(R // tm,),
            in_specs=[pl.BlockSpec((tm, D // 2), lambda i, te: (i, 0)),
                      pl.BlockSpec((1, 1, D, dexp), lambda i, te: (l, te[i], 0, 0)),
                      pl.BlockSpec((1, 1, D, dexp), lambda i, te: (l, te[i], 0, 0)),
                      pl.BlockSpec((1, 1, dexp, D), lambda i, te: (l, te[i], 0, 0))],
            out_specs=pl.BlockSpec((tm, D), lambda i, te: (i, 0)),
            scratch_shapes=[pltpu.VMEM((D, dexp), BF16), pltpu.VMEM((D, dexp), BF16), pltpu.VMEM((dexp, D), BF16)]),
        out_shape=jax.ShapeDtypeStruct((R, D), F32),
        compiler_params=_cparams(1),
        name="moe",
    )(tile_expert, xs, weg, weu, wed)


def _ln2_kernel(x1_ref, ada_ref, y0_ref, y1_ref, wts_ref, lng_ref, lnb_ref, o_ref, *, alpha):
    ada = ada_ref[0]
    wts = wts_ref[0]
    moe = y0_ref[0] * wts[:, 0:1] + y1_ref[0] * wts[:, 1:2]
    o_ref[0] = _layer_norm(alpha * x1_ref[0] + ada[5:6] * moe, lng_ref[0, 1:2], lnb_ref[0, 1:2])


def _ln2_call(l, x1, ada, y0, y1, wts, lng, lnb, tm, alpha):
    B, T, D = x1.shape
    bmap = lambda b, t: (b, 0, 0)
    lmap = lambda b, t: (l, 0, 0)
    tmap = lambda b, t: (b, t, 0)
    return pl.pallas_call(
        functools.partial(_ln2_kernel, alpha=alpha),
        grid=(B, T // tm),
        in_specs=[pl.BlockSpec((1, tm, D), tmap),
                  pl.BlockSpec((1, 6, D), bmap),
                  pl.BlockSpec((1, tm, D), tmap),
                  pl.BlockSpec((1, tm, D), tmap),
                  pl.BlockSpec((1, tm, LANES), tmap),
                  pl.BlockSpec((1, 2, D), lmap),
                  pl.BlockSpec((1, 2, D), lmap)],
        out_specs=pl.BlockSpec((1, tm, D), tmap),
        out_shape=jax.ShapeDtypeStruct((B, T, D), F32),
        compiler_params=_cparams(2),
        name="ln2",
    )(x1, ada, y0, y1, wts, lng, lnb)


def _take_rows(a, idx):
    return a.at[idx].get(mode="promise_in_bounds")


def _lookup(table, idx):
    sel = idx[:, None] == jnp.arange(table.shape[0], dtype=idx.dtype)[None, :]
    return jnp.sum(jnp.where(sel, table[None, :], 0), axis=1)


def _route(eid, rank, counts, n_tok):
    tm = MOE_TILE
    flat = eid.reshape(-1)
    n_asg = flat.shape[0]
    n_rows = (-(-n_asg // tm) + N_EXPERTS) * tm
    order = jnp.argsort(flat).astype(jnp.int32)
    sizes = counts.astype(jnp.int32)
    off = jnp.cumsum(sizes) - sizes
    psz = ((sizes + tm - 1) // tm) * tm
    pend = jnp.cumsum(psz)
    poff = pend - psz
    dest = _lookup(poff, flat) + rank.reshape(-1)
    tile_start = jnp.arange(n_rows // tm, dtype=jnp.int32) * tm
    tile_expert = jnp.minimum(jnp.sum((pend[None, :] <= tile_start[:, None]).astype(jnp.int32), axis=1),
                              N_EXPERTS - 1)
    within = jnp.arange(n_rows, dtype=jnp.int32) - jnp.repeat(poff[tile_expert], tm)
    valid = within < jnp.repeat(sizes[tile_expert], tm)
    pos = jnp.clip(jnp.repeat(off[tile_expert], tm) + within, 0, n_asg - 1)
    row_tok = jnp.where(valid, _take_rows(order, pos) // TOP_K, jnp.arange(n_rows, dtype=jnp.int32) % n_tok)
    return row_tok, dest.reshape(n_tok, TOP_K), tile_expert


def _pad_lanes(a, width=LANES):
    return jnp.pad(a, [(0, 0)] * (a.ndim - 1) + [(0, width - a.shape[-1])])


def _time_block(T):
    return min(T, 256)


def _merge_rows(T):
    return min(T, 512)


def _ln2_rows(T):
    return min(T, 1024)


def _trunk(x, ada_all, states, P, depth, alpha):
    B, T, D = x.shape
    Tt = _time_block(T)
    c_m, n_m, m_m, cv_m, h_s, cv_s, s_g = states
    new = [[] for _ in range(7)]
    for l in range(depth):
        ada = ada_all[l].reshape(B, 6, D)
        c0t = jnp.swapaxes(c_m[l], -1, -2)
        m0 = _pad_lanes(m_m[l])[:, None, :]
        h0t = (h_s[l].reshape(B, G_S, HPG_S, P_S, N_S).transpose(0, 1, 4, 2, 3)
               .reshape(B, G_S, N_S, HPG_S * P_S))
        hm, c_t, n_n, m_n, cvm_n = _mlstm_call(l, x, ada, P["w_mlstm"], P["mlstm_conv_w"], P["mlstm_conv_b"],
                                               P["mlstm_bif"], P["mlstm_norm_g"], c0t, n_m[l], m0, cv_m[l], Tt)
        ys, h_t, cvs_n = _ssd_call(l, x, ada, P["w_ssd"], P["ssd_conv_w"], P["ssd_conv_b"], P["ssd_dtb"],
                                   P["ssd_alog"], P["ssd_dfull"], P["ssd_norm_g"], h0t, cv_s[l], Tt)
        og, s_n = _gla_call(l, x, ada, P["w_gla"], P["gla_w_lr"], P["gla_b_lr"], P["gla_norm_g"], s_g[l], Tt)
        x1, u2, eid, wts, cnt = _merge_call(l, x, ada, hm, ys, og, P["w_gate"], P["b_branch"], P["w_branch"],
                                       P["w_out"], P["ln_g"], P["ln_b"], P["w_rt"], P["b_rt"], _merge_rows(T), alpha)
        row_tok, dest, tile_expert = _route(eid[:, :, :TOP_K], eid[:, :, TOP_K:2 * TOP_K], cnt[0, 0, :N_EXPERTS], B * T)
        xs = _take_rows(u2.reshape(B * T, D // 2), row_tok)
        ye = _moe_call(l, tile_expert, xs, P["w_e_gate"], P["w_e_up"], P["w_e_down"])
        y0 = _take_rows(ye, dest[:, 0]).reshape(B, T, D)
        y1 = _take_rows(ye, dest[:, 1]).reshape(B, T, D)
        x = _ln2_call(l, x1, ada, y0, y1, wts, P["ln_g"], P["ln_b"], _ln2_rows(T), alpha)

        new[0].append(jnp.swapaxes(c_t, -1, -2))
        new[1].append(n_n)
        new[2].append(m_n[:, 0, :NH_M])
        new[3].append(cvm_n)
        new[4].append(h_t.reshape(B, G_S, N_S, HPG_S, P_S).transpose(0, 1, 3, 4, 2).reshape(B, NH_S, P_S, N_S))
        new[5].append(cvs_n)
        new[6].append(s_n)
    return x, tuple(jnp.stack(lst) for lst in new)


def kernel(x_prompt, x_sample, state_mlstm_c, state_mlstm_n, state_mlstm_m, state_mlstm_conv, state_ssd, state_ssd_conv, state_gla, c_prompt, c_sample, w_ada, b_ada, w_in, mlstm_b_i, mlstm_b_f, mlstm_conv_w, mlstm_conv_b, mlstm_norm_g, ssd_conv_w, ssd_conv_b, ssd_dt_bias, ssd_a_log, ssd_d, ssd_norm_g, gla_w_lr, gla_b_lr, gla_norm_g, b_branch, w_branch, w_out, ln_g, ln_b, w_grp, b_grp, w_router, b_router, w_e_gate, w_e_up, w_e_down):
    depth, D, _ = w_in.shape
    alpha = (2 * depth) ** 0.25
    nbp = x_prompt.shape[0]

    edges = np.concatenate([[0], np.cumsum(COL_SIZES)])
    col = {n: w_in[:, :, int(edges[i]):int(edges[i + 1])] for i, n in enumerate(COL_NAMES)}
    w_gate = w_in[:, :, int(edges[-1]):].astype(BF16)
    cat = lambda parts: jnp.concatenate(parts, axis=-1).astype(BF16)
    row = lambda a: a[:, None, :]
    P = {
        "w_mlstm": cat([col["qk_m"], col["v_m"], col["o_m"], _pad_lanes(jnp.concatenate([col["i_m"], col["f_m"]], -1))]),
        "w_ssd": cat([col["z_s"], col["xbc_s"], _pad_lanes(col["dt_s"])]),
        "w_gla": cat([col["q_g"], col["k_g"], col["v_g"], col["g_g"], _pad_lanes(col["lr_g"])]),
        "w_gate": w_gate,
        "mlstm_conv_w": mlstm_conv_w, "mlstm_conv_b": row(mlstm_conv_b),
        "mlstm_bif": row(_pad_lanes(jnp.concatenate([mlstm_b_i, mlstm_b_f], -1))),
        "mlstm_norm_g": row(mlstm_norm_g),
        "ssd_conv_w": ssd_conv_w, "ssd_conv_b": row(ssd_conv_b),
        "ssd_dtb": row(_pad_lanes(ssd_dt_bias)), "ssd_alog": row(_pad_lanes(ssd_a_log)),
        "ssd_dfull": row(jnp.repeat(ssd_d, P_S, axis=-1)), "ssd_norm_g": row(ssd_norm_g),
        "gla_w_lr": jnp.pad(gla_w_lr, ((0, 0), (0, LANES - R_G), (0, 0))).astype(BF16),
        "gla_b_lr": row(gla_b_lr), "gla_norm_g": row(gla_norm_g),
        "b_branch": row(b_branch), "w_branch": w_branch.astype(BF16), "w_out": w_out.astype(BF16),
        "ln_g": ln_g, "ln_b": ln_b,
        "w_rt": _pad_lanes(jnp.concatenate([w_router, w_grp], -1)).astype(BF16),
        "b_rt": row(_pad_lanes(jnp.concatenate([b_router, b_grp], -1))),
        "w_e_gate": w_e_gate, "w_e_up": w_e_up, "w_e_down": w_e_down,
    }

    ada_all = _ada_call(jnp.concatenate([c_prompt, c_sample], axis=0), w_ada, b_ada)

    sample_states = (state_mlstm_c, state_mlstm_n, state_mlstm_m, state_mlstm_conv,
                     state_ssd, state_ssd_conv, state_gla)
    prompt_states = tuple(jnp.zeros((s.shape[0], nbp) + s.shape[2:], x_prompt.dtype) for s in sample_states)

    y_prompt, new_p = _trunk(x_prompt, ada_all[:, :nbp], prompt_states, P, depth, alpha)
    y_sample, new_s = _trunk(x_sample, ada_all[:, nbp:], sample_states, P, depth, alpha)
    return (y_prompt, y_sample) + new_p + new_s
```

```python
import functools

import jax
import jax.numpy as jnp
import numpy as np
from jax import lax
from jax.experimental import pallas as pl
from jax.experimental.pallas import tpu as pltpu

F32 = jnp.float32
BF16 = jnp.bfloat16
NEG_INF = float("-inf")

CHUNK = 64
CONV_W = 4
NH_M, DQK_M, DV_M = 4, 128, 256
QK_M, VM = NH_M * DQK_M, NH_M * DV_M
NH_S, P_S, N_S, G_S = 16, 64, 128, 4
HPG_S = NH_S // G_S
XS, BC_S = NH_S * P_S, G_S * N_S
CONV_S_DIM = XS + 2 * BC_S
NH_G, DK_G, DV_G = 4, 128, 256
QK_G, VG = NH_G * DK_G, NH_G * DV_G
R_G = 16
TAU_G = 16.0
N_GROUPS, EXP_PER_GROUP, TOP_K = 4, 8, 2
N_EXPERTS = N_GROUPS * EXP_PER_GROUP
LN_EPS = 1e-5
LANES = 128

COL_SIZES = (2 * QK_M, VM, VM, NH_M, NH_M, XS, CONV_S_DIM, NH_S, QK_G, QK_G, VG, VG, R_G)
COL_NAMES = ("qk_m", "v_m", "o_m", "i_m", "f_m", "z_s", "xbc_s", "dt_s", "q_g", "k_g", "v_g", "g_g", "lr_g")

VMEM_LIMIT = 56 * 1024 * 1024
MOE_TILE = 512
MOE_TILE_SMALL = 128


def _cparams(n_axes):
    return pltpu.CompilerParams(dimension_semantics=("arbitrary",) * n_axes, vmem_limit_bytes=VMEM_LIMIT)


def _dot(a, b):
    return jnp.dot(a, b, preferred_element_type=F32)


def _dot_nt(a, b):
    return lax.dot_general(a, b, (((1,), (1,)), ((), ())), preferred_element_type=F32)


def _split3(x):
    hi = x.astype(BF16)
    r = x - hi.astype(F32)
    mid = r.astype(BF16)
    lo = (r - mid.astype(F32)).astype(BF16)
    return hi, mid, lo


def _eye(n, m):
    r = lax.broadcasted_iota(jnp.int32, (n, m), 0)
    c = lax.broadcasted_iota(jnp.int32, (n, m), 1)
    return jnp.where(r == c, 1.0, 0.0).astype(BF16)


def _transpose_rows(x, n):
    e = _eye(n, x.shape[1])
    hi, mid, lo = _split3(x)
    return _dot_nt(e, lo) + _dot_nt(e, mid) + _dot_nt(e, hi)


def _block_tril(Tt, L):
    r = lax.broadcasted_iota(jnp.int32, (Tt, Tt), 0)
    c = lax.broadcasted_iota(jnp.int32, (Tt, Tt), 1)
    return jnp.where(((r // L) == (c // L)) & (c <= r), 1.0, 0.0).astype(BF16)


def _block_full(Tt, L):
    r = lax.broadcasted_iota(jnp.int32, (Tt, Tt), 0)
    c = lax.broadcasted_iota(jnp.int32, (Tt, Tt), 1)
    return jnp.where((r // L) == (c // L), 1.0, 0.0).astype(BF16)


def _chunk_last(x, L):
    n = x.shape[0] // L
    return jnp.concatenate([jnp.broadcast_to(x[(c + 1) * L - 1:(c + 1) * L, :], (L, x.shape[1])) for c in range(n)], axis=0)


def _transpose_f32(x):
    if x.shape[0] % LANES == 0:
        return x.T
    return _transpose_rows(x, x.shape[1])


def _log_sigmoid(x):
    return jnp.minimum(x, 0.0) - jnp.log1p(jnp.exp(-jnp.abs(x)))


def _softplus(x):
    return jnp.maximum(x, 0.0) + jnp.log1p(jnp.exp(-jnp.abs(x)))


def _silu(x):
    return x * jax.nn.sigmoid(x)


def _modulate(x, ada, shift_row, scale_row):
    return x * (1.0 + ada[scale_row:scale_row + 1]) + ada[shift_row:shift_row + 1]


def _causal_conv(buf, x, w_ref, b_ref, Tt):
    buf[8:8 + Tt, :] = x
    y = buf[5:5 + Tt, :] * w_ref[0, 0:1, :]
    y = y + buf[6:6 + Tt, :] * w_ref[0, 1:2, :]
    y = y + buf[7:7 + Tt, :] * w_ref[0, 2:3, :]
    y = y + x * w_ref[0, 3:4, :]
    y = y + b_ref[0]
    buf[5:8, :] = buf[5 + Tt:8 + Tt, :]
    return y


def _group_norm(x, n_groups, center):
    w = x.shape[1] // n_groups
    outs = []
    for g in range(n_groups):
        xg = x[:, g * w:(g + 1) * w]
        if center:
            xg = xg - jnp.mean(xg, axis=1, keepdims=True)
        outs.append(xg * lax.rsqrt(jnp.mean(xg * xg, axis=1, keepdims=True) + LN_EPS))
    return jnp.concatenate(outs, axis=1)


def _ada_kernel(c_ref, w_ref, b_ref, o_ref):
    c = _silu(c_ref[...]).astype(BF16)
    o_ref[0] = _dot(c, w_ref[0].astype(BF16)) + b_ref[0]


def _ada_call(c_all, w_ada, b_ada):
    depth, d, n6 = w_ada.shape
    nb = c_all.shape[0]
    tn = 1536
    return pl.pallas_call(
        _ada_kernel,
        grid=(depth, n6 // tn),
        in_specs=[pl.BlockSpec((nb, d), lambda l, j: (0, 0)),
                  pl.BlockSpec((1, d, tn), lambda l, j: (l, 0, j)),
                  pl.BlockSpec((1, 1, tn), lambda l, j: (l, 0, j))],
        out_specs=pl.BlockSpec((1, nb, tn), lambda l, j: (l, 0, j)),
        out_shape=jax.ShapeDtypeStruct((depth, nb, n6), F32),
        compiler_params=_cparams(2),
        name="ada",
    )(c_all, w_ada, b_ada.reshape(depth, 1, n6))


def _mlstm_kernel(x_ref, ada_ref, w_ref, cw_ref, cb_ref, bif_ref, ng_ref, c0_ref, n0_ref, m0_ref, cv0_ref,
                  out_ref, c_out, n_out, m_out, cv_out,
                  conv_s, q_s, qb_s, kb_s, v_s, o_s, kwt_s, un_s, nu_s, p_s, rs_s, b_s, ml_s, n_s, m_s, *ct_s, Tt, L):
    t = pl.program_id(1)
    nc = Tt // L

    @pl.when(t == 0)
    def _():
        for h in range(NH_M):
            ct_s[h][...] = c0_ref[0, h]
        n_s[...] = n0_ref[0]
        m_s[...] = m0_ref[0]
        conv_s[5:8, :] = cv0_ref[0]

    u = _modulate(x_ref[0], ada_ref[0], 0, 1).astype(BF16)
    g = _dot(u, w_ref[0, :, 2 * QK_M + 2 * VM:]) + bif_ref[0]
    qk = _dot(u, w_ref[0, :, 0:2 * QK_M])
    lane = lax.broadcasted_iota(jnp.int32, g.shape, 1)
    g = jnp.where(lane < NH_M, g, _log_sigmoid(g))

    tril = _block_tril(Tt, L)
    hi, mid, lo = _split3(g)
    cs = _dot(tril, lo) + _dot(tril, mid) + _dot(tril, hi)
    v_s[...] = _dot(u, w_ref[0, :, 2 * QK_M:2 * QK_M + VM]).astype(BF16)
    qk = _silu(_causal_conv(conv_s, qk, cw_ref, cb_ref, Tt))
    q = qk[:, :QK_M]
    k = qk[:, QK_M:] * DQK_M ** -0.5
    q_s[...] = q
    qb_s[...] = q.astype(BF16)
    kb_s[...] = k.astype(BF16)
    o_s[...] = _dot(u, w_ref[0, :, 2 * QK_M + VM:2 * QK_M + 2 * VM])
    b0 = pltpu.roll(cs, LANES - NH_M, axis=1)
    r = g - b0
    row = lax.broadcasted_iota(jnp.int32, g.shape, 0) % L
    a = r
    sh = 1
    while sh < L:
        a = jnp.maximum(a, jnp.where(row >= sh, pltpu.roll(a, sh, axis=0), NEG_INF))
        sh *= 2
    al = a
    sh = 1
    while sh < L:
        al = jnp.maximum(al, jnp.where(row + sh < L, pltpu.roll(al, Tt - sh, axis=0), NEG_INF))
        sh *= 2
    b_s[...] = b0
    ml_s[...] = b0 + a
    wsrc = jnp.exp(r - al)
    r_rows = _transpose_f32(r)[0:8, :]
    col = lax.broadcasted_iota(jnp.int32, (DQK_M, Tt), 1)
    rr = lax.broadcasted_iota(jnp.int32, (L, L), 0)
    cc = lax.broadcasted_iota(jnp.int32, (L, L), 1)
    causal = cc <= rr
    for h in range(NH_M):
        ks = slice(h * DQK_M, (h + 1) * DQK_M)
        kw = wsrc[:, h:h + 1] * k[:, ks]
        kwt_s[h] = _transpose_f32(kw).astype(BF16)
        for c in range(nc):
            nu_s[c * NH_M + h:c * NH_M + h + 1, :] = jnp.sum(kw[c * L:(c + 1) * L], axis=0, keepdims=True)
    for c in range(nc):
        for h in range(NH_M):
            vs = slice(h * DV_M, (h + 1) * DV_M)
            kwm = jnp.where((col // L) == c, kwt_s[h], jnp.zeros((), BF16)) if nc > 1 else kwt_s[h]
            un_s[c * NH_M + h] = _dot(kwm, v_s[:, vs])
    for c in range(nc):
        rows = slice(c * L, (c + 1) * L)
        sl = []
        for h in range(NH_M):
            ks = slice(h * DQK_M, (h + 1) * DQK_M)
            dloc = jnp.exp(jnp.where(causal, r_rows[h:h + 1, rows] - a[rows, h:h + 1], NEG_INF))
            sl.append(_dot_nt(qb_s[rows, ks], kb_s[rows, ks]) * dloc)
        for h in range(NH_M):
            vs = slice(h * DV_M, (h + 1) * DV_M)
            p_s[rows, vs] = _dot(sl[h].astype(BF16), v_s[rows, vs])
            rs_s[rows, h:h + 1] = jnp.sum(sl[h], axis=1, keepdims=True)

    lane_l = lax.broadcasted_iota(jnp.int32, (L, LANES), 1)
    for c in range(nc):
        rows = slice(c * L, (c + 1) * L)
        m_prev = m_s[...]
        bc = b_s[rows, :]
        mlc = ml_s[rows, :]
        mt = jnp.maximum(bc + m_prev, mlc)
        corr = jnp.exp(mlc - mt)
        w_inter = jnp.exp(bc + m_prev - mt)
        inter = []
        qn = jnp.zeros((L, LANES), F32)
        for h in range(NH_M):
            ks = slice(h * DQK_M, (h + 1) * DQK_M)
            inter.append(_dot(qb_s[rows, ks], ct_s[h][...].astype(BF16)))
            qn_h = jnp.sum(q_s[rows, ks] * n_s[h:h + 1, :], axis=1, keepdims=True)
            qn = jnp.where(lane_l == h, qn_h, qn)
        den = corr * rs_s[rows, :] + w_inter * qn
        inv = 1.0 / jnp.maximum(jnp.abs(den), jnp.exp(-mt))
        ca = corr * inv
        cb2 = w_inter * inv
        for h in range(NH_M):
            vs = slice(h * DV_M, (h + 1) * DV_M)
            p_s[rows, vs] = ca[:, h:h + 1] * p_s[rows, vs] + cb2[:, h:h + 1] * inter[h]
        m_last = mt[L - 1:L, :]
        c_l = jnp.exp(mlc[L - 1:L, :] - m_last)
        decay = jnp.exp(bc[L - 1:L, :] + m_prev - m_last)
        for h in range(NH_M):
            ct_s[h][...] = decay[:, h:h + 1] * ct_s[h][...] + c_l[:, h:h + 1] * un_s[c * NH_M + h]
            n_s[h:h + 1, :] = decay[:, h:h + 1] * n_s[h:h + 1, :] + c_l[:, h:h + 1] * nu_s[c * NH_M + h:c * NH_M + h + 1, :]
        m_s[...] = m_last

    hn = _group_norm(p_s[...], NH_M, True) * ng_ref[0] * jax.nn.sigmoid(o_s[...])
    out_ref[0] = hn.astype(out_ref.dtype)

    @pl.when(t == pl.num_programs(1) - 1)
    def _():
        for h in range(NH_M):
            c_out[0, h] = ct_s[h][...]
        n_out[0] = n_s[...]
        m_out[0] = m_s[...]
        cv_out[0] = conv_s[5:8, :]


def _mlstm_call(l, x, ada, w, cw, cb, bif, ng, c0t, n0, m0, cv0, Tt):
    B, T, D = x.shape
    L = min(CHUNK, T)
    wcols = w.shape[2]
    bmap = lambda b, t: (b, 0, 0)
    lmap = lambda b, t: (l, 0, 0)
    return pl.pallas_call(
        functools.partial(_mlstm_kernel, Tt=Tt, L=L),
        grid=(B, T // Tt),
        in_specs=[pl.BlockSpec((1, Tt, D), lambda b, t: (b, t, 0)),
                  pl.BlockSpec((1, 6, D), bmap),
                  pl.BlockSpec((1, D, wcols), lmap),
                  pl.BlockSpec((1, CONV_W, 2 * QK_M), lmap),
                  pl.BlockSpec((1, 1, 2 * QK_M), lmap),
                  pl.BlockSpec((1, 1, LANES), lmap),
                  pl.BlockSpec((1, 1, VM), lmap),
                  pl.BlockSpec((1, NH_M, DQK_M, DV_M), lambda b, t: (b, 0, 0, 0)),
                  pl.BlockSpec((1, NH_M, DQK_M), bmap),
                  pl.BlockSpec((1, 1, LANES), bmap),
                  pl.BlockSpec((1, CONV_W - 1, 2 * QK_M), bmap)],
        out_specs=[pl.BlockSpec((1, Tt, VM), lambda b, t: (b, t, 0)),
                   pl.BlockSpec((1, NH_M, DQK_M, DV_M), lambda b, t: (b, 0, 0, 0)),
                   pl.BlockSpec((1, NH_M, DQK_M), bmap),
                   pl.BlockSpec((1, 1, LANES), bmap),
                   pl.BlockSpec((1, CONV_W - 1, 2 * QK_M), bmap)],
        out_shape=[jax.ShapeDtypeStruct((B, T, VM), BF16),
                   jax.ShapeDtypeStruct((B, NH_M, DQK_M, DV_M), F32),
                   jax.ShapeDtypeStruct((B, NH_M, DQK_M), F32),
                   jax.ShapeDtypeStruct((B, 1, LANES), F32),
                   jax.ShapeDtypeStruct((B, CONV_W - 1, 2 * QK_M), F32)],
        scratch_shapes=[pltpu.VMEM((8 + Tt, 2 * QK_M), F32),
                        pltpu.VMEM((Tt, QK_M), F32),
                        pltpu.VMEM((Tt, QK_M), BF16),
                        pltpu.VMEM((Tt, QK_M), BF16),
                        pltpu.VMEM((Tt, VM), BF16),
                        pltpu.VMEM((Tt, VM), F32),
                        pltpu.VMEM((NH_M, DQK_M, Tt), BF16),
                        pltpu.VMEM((Tt // L * NH_M, DQK_M, DV_M), F32),
                        pltpu.VMEM((Tt // L * NH_M, DQK_M), F32),
                        pltpu.VMEM((Tt, VM), F32),
                        pltpu.VMEM((Tt, LANES), F32),
                        pltpu.VMEM((Tt, LANES), F32),
                        pltpu.VMEM((Tt, LANES), F32),
                        pltpu.VMEM((NH_M, DQK_M), F32),
                        pltpu.VMEM((1, LANES), F32)] + [pltpu.VMEM((DQK_M, DV_M), F32)] * NH_M,
        compiler_params=_cparams(2),
        name="mlstm",
    )(x, ada, w, cw, cb, bif, ng, c0t, n0, m0, cv0)


def _pair_cols(x, hd, lane):
    return jnp.where(lane < P_S, x[:, hd:hd + 1], x[:, hd + 1:hd + 2])


def _ssd_kernel(x_ref, ada_ref, w_ref, cw_ref, cb_ref, dtb_ref, alog_ref, d_ref, ng_ref, h0_ref, cv0_ref,
                out_ref, h_out, cv_out,
                conv_s, zg_s, x_s, xw_s, b_s, c_s, bt_s, e2_s, un_s, dec_s, y_s, *ht_s, Tt, L):
    t = pl.program_id(1)
    nc = Tt // L
    n_pairs = NH_S // 2

    @pl.when(t == 0)
    def _():
        for g in range(G_S):
            ht_s[g][...] = h0_ref[0, g]
        conv_s[5:8, :] = cv0_ref[0]

    u = _modulate(x_ref[0], ada_ref[0], 0, 1).astype(BF16)
    dt = _softplus(_dot(u, w_ref[0, :, XS + CONV_S_DIM:]) + dtb_ref[0])
    xbc = _dot(u, w_ref[0, :, XS:XS + CONV_S_DIM])
    a_row = -jnp.exp(alog_ref[0])

    tril = _block_tril(Tt, L)
    full = _block_full(Tt, L)
    hi, mid, lo = _split3(dt * a_row)
    cs = _dot(tril, lo) + _dot(tril, mid) + _dot(tril, hi)
    csl = _dot(full, lo) + _dot(full, mid) + _dot(full, hi)
    xbc = _silu(_causal_conv(conv_s, xbc, cw_ref, cb_ref, Tt))
    x = xbc[:, :XS]
    bm = xbc[:, XS:XS + BC_S]
    x_s[...] = x
    b_s[...] = bm.astype(BF16)
    c_s[...] = xbc[:, XS + BC_S:].astype(BF16)
    wsrc = jnp.exp(csl - cs) * dt
    dec = jnp.exp(csl)
    cs_rows = _transpose_f32(cs)[0:NH_S, :]
    dt_rows = _transpose_f32(dt)[0:NH_S, :]
    for g in range(G_S):
        bt_s[g] = _transpose_f32(bm[:, g * N_S:(g + 1) * N_S]).astype(BF16)
    zg_s[...] = _silu(_dot(u, w_ref[0, :, 0:XS]))
    lane_t = lax.broadcasted_iota(jnp.int32, (Tt, LANES), 1)
    for pp in range(n_pairs):
        pc = slice(pp * LANES, (pp + 1) * LANES)
        e2_s[:, pc] = _pair_cols(cs, 2 * pp, lane_t)
        xw_s[:, pc] = (x[:, pc] * _pair_cols(wsrc, 2 * pp, lane_t)).astype(BF16)
    lane1 = lax.broadcasted_iota(jnp.int32, (1, LANES), 1)
    for c in range(nc):
        for pp in range(n_pairs):
            r0 = c * L
            dec_s[c * n_pairs + pp:c * n_pairs + pp + 1, :] = _pair_cols(dec[r0:r0 + 1, :], 2 * pp, lane1)

    rr = lax.broadcasted_iota(jnp.int32, (L, LANES), 0)
    lane_l = lax.broadcasted_iota(jnp.int32, (L, LANES), 1)
    causal2 = (lane_l % P_S) <= rr
    col = lax.broadcasted_iota(jnp.int32, (N_S, Tt), 1)
    zero_b = jnp.zeros((), BF16)
    for c in range(nc):
        rows = slice(c * L, (c + 1) * L)
        for g in range(G_S):
            gs = slice(g * N_S, (g + 1) * N_S)
            bg = b_s[rows, gs]
            cb2 = _dot_nt(c_s[rows, gs], jnp.concatenate([bg, bg], axis=0))
            btm = jnp.where((col // L) == c, bt_s[g], zero_b) if nc > 1 else bt_s[g]
            un_s[c * G_S + g] = _dot(btm, xw_s[:, g * 2 * LANES:(g + 1) * 2 * LANES])
            for p2 in range(2):
                pp = g * 2 + p2
                hd = 2 * pp
                pc = slice(pp * LANES, (pp + 1) * LANES)
                csr2 = jnp.concatenate([cs_rows[hd:hd + 1, rows], cs_rows[hd + 1:hd + 2, rows]], axis=1)
                dtr2 = jnp.concatenate([dt_rows[hd:hd + 1, rows], dt_rows[hd + 1:hd + 2, rows]], axis=1)
                seg2 = jnp.exp(jnp.where(causal2, e2_s[rows, pc] - csr2, NEG_INF))
                m2 = (cb2 * seg2 * dtr2).astype(BF16)
                xp = x_s[rows, pc].astype(BF16)
                xbd = jnp.concatenate([jnp.where(lane_l < P_S, xp, zero_b), jnp.where(lane_l >= P_S, xp, zero_b)], axis=0)
                y_s[rows, pc] = _dot(m2, xbd)

    for c in range(nc):
        rows = slice(c * L, (c + 1) * L)
        for g in range(G_S):
            gc = slice(g * 2 * LANES, (g + 1) * 2 * LANES)
            yi = _dot(c_s[rows, g * N_S:(g + 1) * N_S], ht_s[g][...].astype(BF16))
            y_s[rows, gc] += jnp.exp(e2_s[rows, gc]) * yi
        for g in range(G_S):
            for p2 in range(2):
                pp = g * 2 + p2
                hc = slice(p2 * LANES, (p2 + 1) * LANES)
                ht_s[g][:, hc] = (dec_s[c * n_pairs + pp:c * n_pairs + pp + 1, :] * ht_s[g][:, hc]
                                  + un_s[c * G_S + g][:, hc])

    y = (y_s[...] + d_ref[0] * x_s[...]) * zg_s[...]
    out_ref[0] = (_group_norm(y, G_S, False) * ng_ref[0]).astype(out_ref.dtype)

    @pl.when(t == pl.num_programs(1) - 1)
    def _():
        for g in range(G_S):
            h_out[0, g] = ht_s[g][...]
        cv_out[0] = conv_s[5:8, :]


def _ssd_call(l, x, ada, w, cw, cb, dtb, alog, dfull, ng, h0t, cv0, Tt):
    B, T, D = x.shape
    L = min(CHUNK, T)
    wcols = w.shape[2]
    bmap = lambda b, t: (b, 0, 0)
    lmap = lambda b, t: (l, 0, 0)
    hshape = (G_S, N_S, HPG_S * P_S)
    assert L == P_S and 2 * L == LANES and T % Tt == 0 and Tt % L == 0
    return pl.pallas_call(
        functools.partial(_ssd_kernel, Tt=Tt, L=L),
        grid=(B, T // Tt),
        in_specs=[pl.BlockSpec((1, Tt, D), lambda b, t: (b, t, 0)),
                  pl.BlockSpec((1, 6, D), bmap),
                  pl.BlockSpec((1, D, wcols), lmap),
                  pl.BlockSpec((1, CONV_W, CONV_S_DIM), lmap),
                  pl.BlockSpec((1, 1, CONV_S_DIM), lmap),
                  pl.BlockSpec((1, 1, LANES), lmap),
                  pl.BlockSpec((1, 1, LANES), lmap),
                  pl.BlockSpec((1, 1, XS), lmap),
                  pl.BlockSpec((1, 1, XS), lmap),
                  pl.BlockSpec((1,) + hshape, lambda b, t: (b, 0, 0, 0)),
                  pl.BlockSpec((1, CONV_W - 1, CONV_S_DIM), bmap)],
        out_specs=[pl.BlockSpec((1, Tt, XS), lambda b, t: (b, t, 0)),
                   pl.BlockSpec((1,) + hshape, lambda b, t: (b, 0, 0, 0)),
                   pl.BlockSpec((1, CONV_W - 1, CONV_S_DIM), bmap)],
        out_shape=[jax.ShapeDtypeStruct((B, T, XS), BF16),
                   jax.ShapeDtypeStruct((B,) + hshape, F32),
                   jax.ShapeDtypeStruct((B, CONV_W - 1, CONV_S_DIM), F32)],
        scratch_shapes=[pltpu.VMEM((8 + Tt, CONV_S_DIM), F32),
                        pltpu.VMEM((Tt, XS), F32),
                        pltpu.VMEM((Tt, XS), F32),
                        pltpu.VMEM((Tt, XS), BF16),
                        pltpu.VMEM((Tt, BC_S), BF16),
                        pltpu.VMEM((Tt, BC_S), BF16),
                        pltpu.VMEM((G_S, N_S, Tt), BF16),
                        pltpu.VMEM((Tt, XS), F32),
                        pltpu.VMEM((Tt // L * G_S, N_S, HPG_S * P_S), F32),
                        pltpu.VMEM((Tt // L * (NH_S // 2), LANES), F32),
                        pltpu.VMEM((Tt, XS), F32)] + [pltpu.VMEM(hshape[1:], F32)] * G_S,
        compiler_params=_cparams(2),
        name="ssd",
    )(x, ada, w, cw, cb, dtb, alog, dfull, ng, h0t, cv0)


def _gla_kernel(x_ref, ada_ref, w_ref, wlr_ref, blr_ref, ng_ref, s0_ref,
                out_ref, s_out,
                sg_s, qg_s, kg_s, v_s, kdt_s, egt_s, un_s, o_s, *st_s, Tt, L):
    t = pl.program_id(1)
    nc = Tt // L

    @pl.when(t == 0)
    def _():
        for h in range(NH_G):
            st_s[h][...] = s0_ref[0, h]

    u = _modulate(x_ref[0], ada_ref[0], 0, 1).astype(BF16)
    lr = _dot(u, w_ref[0, :, 2 * QK_G + 2 * VG:]).astype(BF16)
    lg = _log_sigmoid(_dot(lr, wlr_ref[0]) + blr_ref[0]) / TAU_G
    q = _dot(u, w_ref[0, :, 0:QK_G])
    k = _dot(u, w_ref[0, :, QK_G:2 * QK_G]) * DK_G ** -0.5

    tril = _block_tril(Tt, L)
    hi, mid, lo = _split3(lg)
    G = _dot(tril, lo) + _dot(tril, mid) + _dot(tril, hi)
    v_s[...] = _dot(u, w_ref[0, :, 2 * QK_G:2 * QK_G + VG]).astype(BF16)
    if nc > 1:
        Gl = _chunk_last(G, L)
    else:
        full = _block_full(Tt, L)
        Gl = _dot(full, lo) + _dot(full, mid) + _dot(full, hi)
    qg_s[...] = (q * jnp.exp(G)).astype(BF16)
    kg_s[...] = (k * jnp.exp(-G)).astype(BF16)
    kd = k * jnp.exp(Gl - G)
    eg = jnp.exp(Gl)
    sg_s[...] = _silu(_dot(u, w_ref[0, :, 2 * QK_G + VG:2 * QK_G + 2 * VG]))
    for h in range(NH_G):
        ks = slice(h * DK_G, (h + 1) * DK_G)
        kdt_s[h] = _transpose_f32(kd[:, ks]).astype(BF16)
        egt_s[h] = _transpose_f32(eg[:, ks])
    col = lax.broadcasted_iota(jnp.int32, (DK_G, Tt), 1)
    for c in range(nc):
        for h in range(NH_G):
            vs = slice(h * DV_G, (h + 1) * DV_G)
            kdm = jnp.where((col // L) == c, kdt_s[h], jnp.zeros((), BF16)) if nc > 1 else kdt_s[h]
            un_s[c * NH_G + h] = _dot(kdm, v_s[:, vs])

    rr = lax.broadcasted_iota(jnp.int32, (L, L), 0)
    cc = lax.broadcasted_iota(jnp.int32, (L, L), 1)
    causal = cc <= rr
    for c in range(nc):
        rows = slice(c * L, (c + 1) * L)
        att = []
        for h in range(NH_G):
            ks = slice(h * DK_G, (h + 1) * DK_G)
            att.append(_dot_nt(qg_s[rows, ks], kg_s[rows, ks]))
        for h in range(NH_G):
            ks = slice(h * DK_G, (h + 1) * DK_G)
            vs = slice(h * DV_G, (h + 1) * DV_G)
            o_s[rows, vs] = _dot(qg_s[rows, ks], st_s[h][...].astype(BF16))
        for h in range(NH_G):
            vs = slice(h * DV_G, (h + 1) * DV_G)
            o_s[rows, vs] += _dot(jnp.where(causal, att[h], 0.0).astype(BF16), v_s[rows, vs])
        for h in range(NH_G):
            st_s[h][...] = egt_s[h][:, c * L:c * L + 1] * st_s[h][...] + un_s[c * NH_G + h]

    out_ref[0] = (_group_norm(o_s[...], NH_G, False) * ng_ref[0] * sg_s[...]).astype(out_ref.dtype)

    @pl.when(t == pl.num_programs(1) - 1)
    def _():
        for h in range(NH_G):
            s_out[0, h] = st_s[h][...]


def _gla_call(l, x, ada, w, wlr, blr, ng, s0, Tt):
    B, T, D = x.shape
    L = min(CHUNK, T)
    wcols = w.shape[2]
    bmap = lambda b, t: (b, 0, 0)
    lmap = lambda b, t: (l, 0, 0)
    sshape = (NH_G, DK_G, DV_G)
    return pl.pallas_call(
        functools.partial(_gla_kernel, Tt=Tt, L=L),
        grid=(B, T // Tt),
        in_specs=[pl.BlockSpec((1, Tt, D), lambda b, t: (b, t, 0)),
                  pl.BlockSpec((1, 6, D), bmap),
                  pl.BlockSpec((1, D, wcols), lmap),
                  pl.BlockSpec((1, LANES, QK_G), lmap),
                  pl.BlockSpec((1, 1, QK_G), lmap),
                  pl.BlockSpec((1, 1, VG), lmap),
                  pl.BlockSpec((1,) + sshape, lambda b, t: (b, 0, 0, 0))],
        out_specs=[pl.BlockSpec((1, Tt, VG), lambda b, t: (b, t, 0)),
                   pl.BlockSpec((1,) + sshape, lambda b, t: (b, 0, 0, 0))],
        out_shape=[jax.ShapeDtypeStruct((B, T, VG), BF16),
                   jax.ShapeDtypeStruct((B,) + sshape, F32)],
        scratch_shapes=[pltpu.VMEM((Tt, VG), F32),
                        pltpu.VMEM((Tt, QK_G), BF16),
                        pltpu.VMEM((Tt, QK_G), BF16),
                        pltpu.VMEM((Tt, VG), BF16),
                        pltpu.VMEM((NH_G, DK_G, Tt), BF16),
                        pltpu.VMEM((NH_G, DK_G, Tt), F32),
                        pltpu.VMEM((Tt // L * NH_G, DK_G, DV_G), F32),
                        pltpu.VMEM((Tt, VG), F32)] + [pltpu.VMEM(sshape[1:], F32)] * NH_G,
        compiler_params=_cparams(2),
        name="gla",
    )(x, ada, w, wlr, blr, ng, s0)


def _layer_norm(x, g, b):
    mu = jnp.mean(x, axis=1, keepdims=True)
    xc = x - mu
    var = jnp.mean(xc * xc, axis=1, keepdims=True)
    return xc * lax.rsqrt(var + LN_EPS) * g + b


def _merge_kernel(x_ref, ada_ref, hm_ref, ys_ref, og_ref, wg_ref, bb_ref, wb_ref, wo_ref, lng_ref, lnb_ref,
                  wrt_ref, brt_ref,
                  x1_ref, u2_ref, eid_ref, wts_ref, cnt_ref, cnt_s, *, alpha):
    x = x_ref[0]
    ada = ada_ref[0]
    D = x.shape[1]
    u = _modulate(x, ada, 0, 1).astype(BF16)
    gate = jax.nn.sigmoid(_dot(u, wg_ref[0]) + bb_ref[0])
    merged = (gate[:, 0:D] * _dot(hm_ref[0], wb_ref[0, 0])
              + gate[:, D:2 * D] * _dot(ys_ref[0], wb_ref[0, 1])
              + gate[:, 2 * D:3 * D] * _dot(og_ref[0], wb_ref[0, 2]))
    y = _dot(merged.astype(BF16), wo_ref[0])
    x1 = _layer_norm(alpha * x + ada[2:3] * y, lng_ref[0, 0:1], lnb_ref[0, 0:1])
    x1_ref[0] = x1
    u2 = _modulate(x1, ada, 3, 4).astype(BF16)
    bits = lax.bitcast_convert_type(u2.astype(F32), jnp.uint32)
    u2_ref[0] = (bits[:, :D // 2] >> 16) | bits[:, D // 2:]

    logits = _dot(u2, wrt_ref[0]) + brt_ref[0]
    lane = lax.broadcasted_iota(jnp.int32, logits.shape, 1)
    lane_f = lane.astype(F32)
    big = float(LANES)
    is_g = (lane >= N_EXPERTS) & (lane < N_EXPERTS + N_GROUPS)
    gmax = jnp.max(jnp.where(is_g, logits, NEG_INF), axis=1, keepdims=True)
    gsum = jnp.sum(jnp.where(is_g, jnp.exp(logits - gmax), 0.0), axis=1, keepdims=True)
    pg_top = 1.0 / gsum
    g_lane = jnp.min(jnp.where(is_g & (logits == gmax), lane_f, big), axis=1, keepdims=True)
    g_idx = g_lane.astype(jnp.int32) - N_EXPERTS
    in_grp = (lane < N_EXPERTS) & ((lane // EXP_PER_GROUP) == g_idx)
    el = jnp.where(in_grp, logits, NEG_INF)
    v1 = jnp.max(el, axis=1, keepdims=True)
    i1 = jnp.min(jnp.where(in_grp & (el == v1), lane_f, big), axis=1, keepdims=True)
    rest = in_grp & (lane_f != i1)
    el2 = jnp.where(rest, logits, NEG_INF)
    v2 = jnp.max(el2, axis=1, keepdims=True)
    i2 = jnp.min(jnp.where(rest & (el2 == v2), lane_f, big), axis=1, keepdims=True)
    e = jnp.exp(v2 - v1)
    w1 = pg_top / (1.0 + e)
    w2 = pg_top * e / (1.0 + e)
    @pl.when((pl.program_id(0) == 0) & (pl.program_id(1) == 0))
    def _():
        cnt_s[...] = jnp.zeros_like(cnt_s)

    tm = logits.shape[0]
    hit1 = lane_f == i1
    hit2 = lane_f == i2
    onehot = jnp.where(hit1 | hit2, 1.0, 0.0)
    rr = lax.broadcasted_iota(jnp.int32, (tm, tm), 0)
    cc = lax.broadcasted_iota(jnp.int32, (tm, tm), 1)
    before = _dot(jnp.where(cc < rr, 1.0, 0.0).astype(BF16), onehot.astype(BF16)) + cnt_s[...]
    r1 = jnp.sum(jnp.where(hit1, before, 0.0), axis=1, keepdims=True)
    r2 = jnp.sum(jnp.where(hit2, before, 0.0), axis=1, keepdims=True)
    cnt_s[...] = cnt_s[...] + jnp.sum(onehot, axis=0, keepdims=True)
    cnt_ref[0] = cnt_s[...]
    eid_ref[0] = jnp.where(lane == 0, i1, jnp.where(lane == 1, i2, jnp.where(lane == 2, r1, jnp.where(lane == 3, r2, 0.0)))).astype(jnp.int32)
    wts_ref[0] = jnp.where(lane == 0, w1, jnp.where(lane == 1, w2, 0.0))


def _merge_call(l, x, ada, hm, ys, og, wg, bb, wb, wo, lng, lnb, wrt, brt, tm, alpha):
    B, T, D = x.shape
    bmap = lambda b, t: (b, 0, 0)
    lmap = lambda b, t: (l, 0, 0)
    tmap = lambda b, t: (b, t, 0)
    return pl.pallas_call(
        functools.partial(_merge_kernel, alpha=alpha),
        grid=(B, T // tm),
        in_specs=[pl.BlockSpec((1, tm, D), tmap),
                  pl.BlockSpec((1, 6, D), bmap),
                  pl.BlockSpec((1, tm, D), tmap),
                  pl.BlockSpec((1, tm, D), tmap),
                  pl.BlockSpec((1, tm, D), tmap),
                  pl.BlockSpec((1, D, 3 * D), lmap),
                  pl.BlockSpec((1, 1, 3 * D), lmap),
                  pl.BlockSpec((1, 3, D, D), lambda b, t: (l, 0, 0, 0)),
                  pl.BlockSpec((1, D, D), lmap),
                  pl.BlockSpec((1, 2, D), lmap),
                  pl.BlockSpec((1, 2, D), lmap),
                  pl.BlockSpec((1, D, LANES), lmap),
                  pl.BlockSpec((1, 1, LANES), lmap)],
        out_specs=[pl.BlockSpec((1, tm, D), tmap),
                   pl.BlockSpec((1, tm, D // 2), tmap),
                   pl.BlockSpec((1, tm, LANES), tmap),
                   pl.BlockSpec((1, tm, LANES), tmap),
                   pl.BlockSpec((1, 1, LANES), lambda b, t: (0, 0, 0))],
        out_shape=[jax.ShapeDtypeStruct((B, T, D), F32),
                   jax.ShapeDtypeStruct((B, T, D // 2), jnp.uint32),
                   jax.ShapeDtypeStruct((B, T, LANES), jnp.int32),
                   jax.ShapeDtypeStruct((B, T, LANES), F32),
                   jax.ShapeDtypeStruct((1, 1, LANES), F32)],
        scratch_shapes=[pltpu.VMEM((1, LANES), F32)],
        compiler_params=_cparams(2),
        name="merge",
    )(x, ada, hm, ys, og, wg, bb, wb, wo, lng, lnb, wrt, brt)


def _moe_kernel(te_ref, xs_ref, wg_ref, wu_ref, wd_ref, o_ref, wg_b, wu_b, wd_b):
    i = pl.program_id(0)
    prev = te_ref[jnp.maximum(i - 1, 0)]

    @pl.when((i == 0) | (te_ref[i] != prev))
    def _():
        wg_b[...] = wg_ref[0, 0].astype(BF16)
        wu_b[...] = wu_ref[0, 0].astype(BF16)
        wd_b[...] = wd_ref[0, 0].astype(BF16)

    w = xs_ref[...]
    lo = lax.bitcast_convert_type(w << 16, F32)
    hi = lax.bitcast_convert_type(w & jnp.uint32(0xFFFF0000), F32)
    xs = jnp.concatenate([lo, hi], axis=1).astype(BF16)
    hg = _dot(xs, wg_b[...])
    hu = _dot(xs, wu_b[...])
    o_ref[...] = _dot((_silu(hg) * hu).astype(BF16), wd_b[...])


def _moe_call(l, tile_expert, xs, weg, weu, wed, tm):
    R = xs.shape[0]
    D, dexp = weg.shape[2], weg.shape[3]
    return pl.pallas_call(
        _moe_kernel,
        grid_spec=pltpu.PrefetchScalarGridSpec(
            num_scalar_prefetch=1,
            grid=(R // tm,),
            in_specs=[pl.BlockSpec((tm, D // 2), lambda i, te: (i, 0)),
                      pl.BlockSpec((1, 1, D, dexp), lambda i, te: (l, te[i], 0, 0)),
                      pl.BlockSpec((1, 1, D, dexp), lambda i, te: (l, te[i], 0, 0)),
                      pl.BlockSpec((1, 1, dexp, D), lambda i, te: (l, te[i], 0, 0))],
            out_specs=pl.BlockSpec((tm, D), lambda i, te: (i, 0)),
            scratch_shapes=[pltpu.VMEM((D, dexp), BF16), pltpu.VMEM((D, dexp), BF16), pltpu.VMEM((dexp, D), BF16)]),
        out_shape=jax.ShapeDtypeStruct((R, D), F32),
        compiler_params=_cparams(1),
        name="moe",
    )(tile_expert, xs, weg, weu, wed)


def _ln2_kernel(x1_ref, ada_ref, y0_ref, y1_ref, wts_ref, lng_ref, lnb_ref, o_ref, *, alpha):
    ada = ada_ref[0]
    wts = wts_ref[0]
    moe = y0_ref[0] * wts[:, 0:1] + y1_ref[0] * wts[:, 1:2]
    o_ref[0] = _layer_norm(alpha * x1_ref[0] + ada[5:6] * moe, lng_ref[0, 1:2], lnb_ref[0, 1:2])


def _ln2_call(l, x1, ada, y0, y1, wts, lng, lnb, tm, alpha):
    B, T, D = x1.shape
    bmap = lambda b, t: (b, 0, 0)
    lmap = lambda b, t: (l, 0, 0)
    tmap = lambda b, t: (b, t, 0)
    return pl.pallas_call(
        functools.partial(_ln2_kernel, alpha=alpha),
        grid=(B, T // tm),
        in_specs=[pl.BlockSpec((1, tm, D), tmap),
                  pl.BlockSpec((1, 6, D), bmap),
                  pl.BlockSpec((1, tm, D), tmap),
                  pl.BlockSpec((1, tm, D), tmap),
                  pl.BlockSpec((1, tm, LANES), tmap),
                  pl.BlockSpec((1, 2, D), lmap),
                  pl.BlockSpec((1, 2, D), lmap)],
        out_specs=pl.BlockSpec((1, tm, D), tmap),
        out_shape=jax.ShapeDtypeStruct((B, T, D), F32),
        compiler_params=_cparams(2),
        name="ln2",
    )(x1, ada, y0, y1, wts, lng, lnb)


def _take_rows(a, idx):
    return a.at[idx].get(mode="promise_in_bounds")


def _lookup(table, idx):
    sel = idx[:, None] == jnp.arange(table.shape[0], dtype=idx.dtype)[None, :]
    return jnp.sum(jnp.where(sel, table[None, :], 0), axis=1)


def _moe_tile(n_asg):
    return MOE_TILE if n_asg >= 4 * MOE_TILE * N_EXPERTS else MOE_TILE_SMALL


def _route(eid, rank, counts, n_tok, tm):
    flat = eid.reshape(-1)
    n_asg = flat.shape[0]
    n_rows = (-(-n_asg // tm) + N_EXPERTS) * tm
    order = jnp.argsort(flat).astype(jnp.int32)
    sizes = counts.astype(jnp.int32)
    off = jnp.cumsum(sizes) - sizes
    psz = ((sizes + tm - 1) // tm) * tm
    pend = jnp.cumsum(psz)
    poff = pend - psz
    dest = _lookup(poff, flat) + rank.reshape(-1)
    tile_start = jnp.arange(n_rows // tm, dtype=jnp.int32) * tm
    tile_expert = jnp.minimum(jnp.sum((pend[None, :] <= tile_start[:, None]).astype(jnp.int32), axis=1),
                              N_EXPERTS - 1)
    within = jnp.arange(n_rows, dtype=jnp.int32) - jnp.repeat(poff[tile_expert], tm)
    valid = within < jnp.repeat(sizes[tile_expert], tm)
    pos = jnp.clip(jnp.repeat(off[tile_expert], tm) + within, 0, n_asg - 1)
    row_tok = jnp.where(valid, _take_rows(order, pos) // TOP_K, jnp.arange(n_rows, dtype=jnp.int32) % n_tok)
    return row_tok, dest.reshape(n_tok, TOP_K), tile_expert


def _pad_lanes(a, width=LANES):
    return jnp.pad(a, [(0, 0)] * (a.ndim - 1) + [(0, width - a.shape[-1])])


def _time_block(T):
    return min(T, 256)


def _merge_rows(T):
    return min(T, 512)


def _ln2_rows(T):
    return min(T, 1024)


def _trunk(x, ada_all, states, P, depth, alpha):
    B, T, D = x.shape
    Tt = _time_block(T)
    c_m, n_m, m_m, cv_m, h_s, cv_s, s_g = states
    new = [[] for _ in range(7)]
    for l in range(depth):
        ada = ada_all[l].reshape(B, 6, D)
        c0t = jnp.swapaxes(c_m[l], -1, -2)
        m0 = _pad_lanes(m_m[l])[:, None, :]
        h0t = (h_s[l].reshape(B, G_S, HPG_S, P_S, N_S).transpose(0, 1, 4, 2, 3)
               .reshape(B, G_S, N_S, HPG_S * P_S))
        hm, c_t, n_n, m_n, cvm_n = _mlstm_call(l, x, ada, P["w_mlstm"], P["mlstm_conv_w"], P["mlstm_conv_b"],
                                               P["mlstm_bif"], P["mlstm_norm_g"], c0t, n_m[l], m0, cv_m[l], Tt)
        ys, h_t, cvs_n = _ssd_call(l, x, ada, P["w_ssd"], P["ssd_conv_w"], P["ssd_conv_b"], P["ssd_dtb"],
                                   P["ssd_alog"], P["ssd_dfull"], P["ssd_norm_g"], h0t, cv_s[l], Tt)
        og, s_n = _gla_call(l, x, ada, P["w_gla"], P["gla_w_lr"], P["gla_b_lr"], P["gla_norm_g"], s_g[l], Tt)
        x1, u2, eid, wts, cnt = _merge_call(l, x, ada, hm, ys, og, P["w_gate"], P["b_branch"], P["w_branch"],
                                       P["w_out"], P["ln_g"], P["ln_b"], P["w_rt"], P["b_rt"], _merge_rows(T), alpha)
        moe_tm = _moe_tile(B * T * TOP_K)
        row_tok, dest, tile_expert = _route(eid[:, :, :TOP_K], eid[:, :, TOP_K:2 * TOP_K], cnt[0, 0, :N_EXPERTS], B * T, moe_tm)
        xs = _take_rows(u2.reshape(B * T, D // 2), row_tok)
        ye = _moe_call(l, tile_expert, xs, P["w_e_gate"], P["w_e_up"], P["w_e_down"], moe_tm)
        y0 = _take_rows(ye, dest[:, 0]).reshape(B, T, D)
        y1 = _take_rows(ye, dest[:, 1]).reshape(B, T, D)
        x = _ln2_call(l, x1, ada, y0, y1, wts, P["ln_g"], P["ln_b"], _ln2_rows(T), alpha)

        new[0].append(jnp.swapaxes(c_t, -1, -2))
        new[1].append(n_n)
        new[2].append(m_n[:, 0, :NH_M])
        new[3].append(cvm_n)
        new[4].append(h_t.reshape(B, G_S, N_S, HPG_S, P_S).transpose(0, 1, 3, 4, 2).reshape(B, NH_S, P_S, N_S))
        new[5].append(cvs_n)
        new[6].append(s_n)
    return x, tuple(jnp.stack(lst) for lst in new)


def kernel(x_prompt, x_sample, state_mlstm_c, state_mlstm_n, state_mlstm_m, state_mlstm_conv, state_ssd, state_ssd_conv, state_gla, c_prompt, c_sample, w_ada, b_ada, w_in, mlstm_b_i, mlstm_b_f, mlstm_conv_w, mlstm_conv_b, mlstm_norm_g, ssd_conv_w, ssd_conv_b, ssd_dt_bias, ssd_a_log, ssd_d, ssd_norm_g, gla_w_lr, gla_b_lr, gla_norm_g, b_branch, w_branch, w_out, ln_g, ln_b, w_grp, b_grp, w_router, b_router, w_e_gate, w_e_up, w_e_down):
    depth, D, _ = w_in.shape
    alpha = (2 * depth) ** 0.25
    nbp = x_prompt.shape[0]

    edges = np.concatenate([[0], np.cumsum(COL_SIZES)])
    col = {n: w_in[:, :, int(edges[i]):int(edges[i + 1])] for i, n in enumerate(COL_NAMES)}
    w_gate = w_in[:, :, int(edges[-1]):].astype(BF16)
    cat = lambda parts: jnp.concatenate(parts, axis=-1).astype(BF16)
    row = lambda a: a[:, None, :]
    P = {
        "w_mlstm": cat([col["qk_m"], col["v_m"], col["o_m"], _pad_lanes(jnp.concatenate([col["i_m"], col["f_m"]], -1))]),
        "w_ssd": cat([col["z_s"], col["xbc_s"], _pad_lanes(col["dt_s"])]),
        "w_gla": cat([col["q_g"], col["k_g"], col["v_g"], col["g_g"], _pad_lanes(col["lr_g"])]),
        "w_gate": w_gate,
        "mlstm_conv_w": mlstm_conv_w, "mlstm_conv_b": row(mlstm_conv_b),
        "mlstm_bif": row(_pad_lanes(jnp.concatenate([mlstm_b_i, mlstm_b_f], -1))),
        "mlstm_norm_g": row(mlstm_norm_g),
        "ssd_conv_w": ssd_conv_w, "ssd_conv_b": row(ssd_conv_b),
        "ssd_dtb": row(_pad_lanes(ssd_dt_bias)), "ssd_alog": row(_pad_lanes(ssd_a_log)),
        "ssd_dfull": row(jnp.repeat(ssd_d, P_S, axis=-1)), "ssd_norm_g": row(ssd_norm_g),
        "gla_w_lr": jnp.pad(gla_w_lr, ((0, 0), (0, LANES - R_G), (0, 0))).astype(BF16),
        "gla_b_lr": row(gla_b_lr), "gla_norm_g": row(gla_norm_g),
        "b_branch": row(b_branch), "w_branch": w_branch.astype(BF16), "w_out": w_out.astype(BF16),
        "ln_g": ln_g, "ln_b": ln_b,
        "w_rt": _pad_lanes(jnp.concatenate([w_router, w_grp], -1)).astype(BF16),
        "b_rt": row(_pad_lanes(jnp.concatenate([b_router, b_grp], -1))),
        "w_e_gate": w_e_gate, "w_e_up": w_e_up, "w_e_down": w_e_down,
    }

    ada_all = _ada_call(jnp.concatenate([c_prompt, c_sample], axis=0), w_ada, b_ada)

    sample_states = (state_mlstm_c, state_mlstm_n, state_mlstm_m, state_mlstm_conv,
                     state_ssd, state_ssd_conv, state_gla)
    prompt_states = tuple(jnp.zeros((s.shape[0], nbp) + s.shape[2:], x_prompt.dtype) for s in sample_states)

    y_prompt, new_p = _trunk(x_prompt, ada_all[:, :nbp], prompt_states, P, depth, alpha)
    y_sample, new_s = _trunk(x_sample, ada_all[:, nbp:], sample_states, P, depth, alpha)
    return (y_prompt, y_sample) + new_p + new_s
```

```python
import functools

import jax
import jax.numpy as jnp
import numpy as np
from jax import lax
from jax.experimental import pallas as pl
from jax.experimental.pallas import tpu as pltpu

F32 = jnp.float32
BF16 = jnp.bfloat16
NEG_INF = float("-inf")

CHUNK = 64
CONV_W = 4
NH_M, DQK_M, DV_M = 4, 128, 256
QK_M, VM = NH_M * DQK_M, NH_M * DV_M
NH_S, P_S, N_S, G_S = 16, 64, 128, 4
HPG_S = NH_S // G_S
XS, BC_S = NH_S * P_S, G_S * N_S
CONV_S_DIM = XS + 2 * BC_S
NH_G, DK_G, DV_G = 4, 128, 256
QK_G, VG = NH_G * DK_G, NH_G * DV_G
R_G = 16
TAU_G = 16.0
N_GROUPS, EXP_PER_GROUP, TOP_K = 4, 8, 2
N_EXPERTS = N_GROUPS * EXP_PER_GROUP
LN_EPS = 1e-5
LANES = 128

COL_SIZES = (2 * QK_M, VM, VM, NH_M, NH_M, XS, CONV_S_DIM, NH_S, QK_G, QK_G, VG, VG, R_G)
COL_NAMES = ("qk_m", "v_m", "o_m", "i_m", "f_m", "z_s", "xbc_s", "dt_s", "q_g", "k_g", "v_g", "g_g", "lr_g")

VMEM_LIMIT = 56 * 1024 * 1024
MOE_TILE = 512
MOE_TILE_SMALL = 128


def _cparams(n_axes):
    return pltpu.CompilerParams(dimension_semantics=("arbitrary",) * n_axes, vmem_limit_bytes=VMEM_LIMIT)


def _dot(a, b):
    return jnp.dot(a, b, preferred_element_type=F32)


def _dot_nt(a, b):
    return lax.dot_general(a, b, (((1,), (1,)), ((), ())), preferred_element_type=F32)


def _split3(x):
    hi = x.astype(BF16)
    r = x - hi.astype(F32)
    mid = r.astype(BF16)
    lo = (r - mid.astype(F32)).astype(BF16)
    return hi, mid, lo


def _eye(n, m):
    r = lax.broadcasted_iota(jnp.int32, (n, m), 0)
    c = lax.broadcasted_iota(jnp.int32, (n, m), 1)
    return jnp.where(r == c, 1.0, 0.0).astype(BF16)


def _transpose_rows(x, n):
    e = _eye(n, x.shape[1])
    hi, mid, lo = _split3(x)
    return _dot_nt(e, lo) + _dot_nt(e, mid) + _dot_nt(e, hi)


def _block_tril(Tt, L):
    r = lax.broadcasted_iota(jnp.int32, (Tt, Tt), 0)
    c = lax.broadcasted_iota(jnp.int32, (Tt, Tt), 1)
    return jnp.where(((r // L) == (c // L)) & (c <= r), 1.0, 0.0).astype(BF16)


def _block_full(Tt, L):
    r = lax.broadcasted_iota(jnp.int32, (Tt, Tt), 0)
    c = lax.broadcasted_iota(jnp.int32, (Tt, Tt), 1)
    return jnp.where((r // L) == (c // L), 1.0, 0.0).astype(BF16)


def _chunk_last(x, L):
    n = x.shape[0] // L
    return jnp.concatenate([jnp.broadcast_to(x[(c + 1) * L - 1:(c + 1) * L, :], (L, x.shape[1])) for c in range(n)], axis=0)


def _transpose_f32(x):
    if x.shape[0] % LANES == 0:
        return x.T
    return _transpose_rows(x, x.shape[1])


def _log_sigmoid(x):
    return jnp.minimum(x, 0.0) - jnp.log1p(jnp.exp(-jnp.abs(x)))


def _softplus(x):
    return jnp.maximum(x, 0.0) + jnp.log1p(jnp.exp(-jnp.abs(x)))


def _silu(x):
    return x * jax.nn.sigmoid(x)


def _modulate(x, ada, shift_row, scale_row):
    return x * (1.0 + ada[scale_row:scale_row + 1]) + ada[shift_row:shift_row + 1]


def _causal_conv(buf, x, w_ref, b_ref, Tt):
    buf[8:8 + Tt, :] = x
    y = buf[5:5 + Tt, :] * w_ref[0, 0:1, :]
    y = y + buf[6:6 + Tt, :] * w_ref[0, 1:2, :]
    y = y + buf[7:7 + Tt, :] * w_ref[0, 2:3, :]
    y = y + x * w_ref[0, 3:4, :]
    y = y + b_ref[0]
    buf[5:8, :] = buf[5 + Tt:8 + Tt, :]
    return y


def _group_norm(x, n_groups, center):
    w = x.shape[1] // n_groups
    outs = []
    for g in range(n_groups):
        xg = x[:, g * w:(g + 1) * w]
        if center:
            xg = xg - jnp.mean(xg, axis=1, keepdims=True)
        outs.append(xg * lax.rsqrt(jnp.mean(xg * xg, axis=1, keepdims=True) + LN_EPS))
    return jnp.concatenate(outs, axis=1)


def _ada_kernel(c_ref, w_ref, b_ref, o_ref):
    c = _silu(c_ref[...]).astype(BF16)
    o_ref[0] = _dot(c, w_ref[0].astype(BF16)) + b_ref[0]


def _ada_call(c_all, w_ada, b_ada):
    depth, d, n6 = w_ada.shape
    nb = c_all.shape[0]
    tn = 1536
    return pl.pallas_call(
        _ada_kernel,
        grid=(depth, n6 // tn),
        in_specs=[pl.BlockSpec((nb, d), lambda l, j: (0, 0)),
                  pl.BlockSpec((1, d, tn), lambda l, j: (l, 0, j)),
                  pl.BlockSpec((1, 1, tn), lambda l, j: (l, 0, j))],
        out_specs=pl.BlockSpec((1, nb, tn), lambda l, j: (l, 0, j)),
        out_shape=jax.ShapeDtypeStruct((depth, nb, n6), F32),
        compiler_params=_cparams(2),
        name="ada",
    )(c_all, w_ada, b_ada.reshape(depth, 1, n6))


def _mlstm_kernel(x_ref, ada_ref, w_ref, cw_ref, cb_ref, bif_ref, ng_ref, c0_ref, n0_ref, m0_ref, cv0_ref,
                  out_ref, c_out, n_out, m_out, cv_out,
                  conv_s, q_s, qb_s, kb_s, v_s, o_s, kwt_s, un_s, nu_s, p_s, rs_s, b_s, ml_s, n_s, m_s, *ct_s, Tt, L):
    t = pl.program_id(1)
    nc = Tt // L

    @pl.when(t == 0)
    def _():
        for h in range(NH_M):
            ct_s[h][...] = c0_ref[0, h]
        n_s[...] = n0_ref[0]
        m_s[...] = m0_ref[0]
        conv_s[5:8, :] = cv0_ref[0]

    u = _modulate(x_ref[0], ada_ref[0], 0, 1).astype(BF16)
    g = _dot(u, w_ref[0, :, 2 * QK_M + 2 * VM:]) + bif_ref[0]
    qk = _dot(u, w_ref[0, :, 0:2 * QK_M])
    lane = lax.broadcasted_iota(jnp.int32, g.shape, 1)
    g = jnp.where(lane < NH_M, g, _log_sigmoid(g))

    tril = _block_tril(Tt, L)
    hi, mid, lo = _split3(g)
    cs = _dot(tril, lo) + _dot(tril, mid) + _dot(tril, hi)
    v_s[...] = _dot(u, w_ref[0, :, 2 * QK_M:2 * QK_M + VM]).astype(BF16)
    qk = _silu(_causal_conv(conv_s, qk, cw_ref, cb_ref, Tt))
    q = qk[:, :QK_M]
    k = qk[:, QK_M:] * DQK_M ** -0.5
    q_s[...] = q
    qb_s[...] = q.astype(BF16)
    kb_s[...] = k.astype(BF16)
    o_s[...] = _dot(u, w_ref[0, :, 2 * QK_M + VM:2 * QK_M + 2 * VM])
    b0 = pltpu.roll(cs, LANES - NH_M, axis=1)
    r = g - b0
    row = lax.broadcasted_iota(jnp.int32, g.shape, 0) % L
    a = r
    sh = 1
    while sh < L:
        a = jnp.maximum(a, jnp.where(row >= sh, pltpu.roll(a, sh, axis=0), NEG_INF))
        sh *= 2
    al = a
    sh = 1
    while sh < L:
        al = jnp.maximum(al, jnp.where(row + sh < L, pltpu.roll(al, Tt - sh, axis=0), NEG_INF))
        sh *= 2
    b_s[...] = b0
    ml_s[...] = b0 + a
    wsrc = jnp.exp(r - al)
    r_rows = _transpose_f32(r)[0:8, :]
    col = lax.broadcasted_iota(jnp.int32, (DQK_M, Tt), 1)
    rr = lax.broadcasted_iota(jnp.int32, (L, L), 0)
    cc = lax.broadcasted_iota(jnp.int32, (L, L), 1)
    causal = cc <= rr
    for h in range(NH_M):
        ks = slice(h * DQK_M, (h + 1) * DQK_M)
        kw = wsrc[:, h:h + 1] * k[:, ks]
        kwt_s[h] = _transpose_f32(kw).astype(BF16)
        for c in range(nc):
            nu_s[c * NH_M + h:c * NH_M + h + 1, :] = jnp.sum(kw[c * L:(c + 1) * L], axis=0, keepdims=True)
    for c in range(nc):
        for h in range(NH_M):
            vs = slice(h * DV_M, (h + 1) * DV_M)
            kwm = jnp.where((col // L) == c, kwt_s[h], jnp.zeros((), BF16)) if nc > 1 else kwt_s[h]
            un_s[c * NH_M + h] = _dot(kwm, v_s[:, vs])
    for c in range(nc):
        rows = slice(c * L, (c + 1) * L)
        sl = []
        for h in range(NH_M):
            ks = slice(h * DQK_M, (h + 1) * DQK_M)
            dloc = jnp.exp(jnp.where(causal, r_rows[h:h + 1, rows] - a[rows, h:h + 1], NEG_INF))
            sl.append(_dot_nt(qb_s[rows, ks], kb_s[rows, ks]) * dloc)
        for h in range(NH_M):
            vs = slice(h * DV_M, (h + 1) * DV_M)
            p_s[rows, vs] = _dot(sl[h].astype(BF16), v_s[rows, vs])
            rs_s[rows, h:h + 1] = jnp.sum(sl[h], axis=1, keepdims=True)

    lane_l = lax.broadcasted_iota(jnp.int32, (L, LANES), 1)
    for c in range(nc):
        rows = slice(c * L, (c + 1) * L)
        m_prev = m_s[...]
        bc = b_s[rows, :]
        mlc = ml_s[rows, :]
        mt = jnp.maximum(bc + m_prev, mlc)
        corr = jnp.exp(mlc - mt)
        w_inter = jnp.exp(bc + m_prev - mt)
        inter = []
        qn = jnp.zeros((L, LANES), F32)
        for h in range(NH_M):
            ks = slice(h * DQK_M, (h + 1) * DQK_M)
            inter.append(_dot(qb_s[rows, ks], ct_s[h][...].astype(BF16)))
            qn_h = jnp.sum(q_s[rows, ks] * n_s[h:h + 1, :], axis=1, keepdims=True)
            qn = jnp.where(lane_l == h, qn_h, qn)
        den = corr * rs_s[rows, :] + w_inter * qn
        inv = 1.0 / jnp.maximum(jnp.abs(den), jnp.exp(-mt))
        ca = corr * inv
        cb2 = w_inter * inv
        for h in range(NH_M):
            vs = slice(h * DV_M, (h + 1) * DV_M)
            p_s[rows, vs] = ca[:, h:h + 1] * p_s[rows, vs] + cb2[:, h:h + 1] * inter[h]
        m_last = mt[L - 1:L, :]
        c_l = jnp.exp(mlc[L - 1:L, :] - m_last)
        decay = jnp.exp(bc[L - 1:L, :] + m_prev - m_last)
        for h in range(NH_M):
            ct_s[h][...] = decay[:, h:h + 1] * ct_s[h][...] + c_l[:, h:h + 1] * un_s[c * NH_M + h]
            n_s[h:h + 1, :] = decay[:, h:h + 1] * n_s[h:h + 1, :] + c_l[:, h:h + 1] * nu_s[c * NH_M + h:c * NH_M + h + 1, :]
        m_s[...] = m_last

    hn = _group_norm(p_s[...], NH_M, True) * ng_ref[0] * jax.nn.sigmoid(o_s[...])
    out_ref[0] = hn.astype(out_ref.dtype)

    @pl.when(t == pl.num_programs(1) - 1)
    def _():
        for h in range(NH_M):
            c_out[0, h] = ct_s[h][...]
        n_out[0] = n_s[...]
        m_out[0] = m_s[...]
        cv_out[0] = conv_s[5:8, :]


def _mlstm_call(l, x, ada, w, cw, cb, bif, ng, c0t, n0, m0, cv0, Tt):
    B, T, D = x.shape
    L = min(CHUNK, T)
    wcols = w.shape[2]
    bmap = lambda b, t: (b, 0, 0)
    lmap = lambda b, t: (l, 0, 0)
    return pl.pallas_call(
        functools.partial(_mlstm_kernel, Tt=Tt, L=L),
        grid=(B, T // Tt),
        in_specs=[pl.BlockSpec((1, Tt, D), lambda b, t: (b, t, 0)),
                  pl.BlockSpec((1, 6, D), bmap),
                  pl.BlockSpec((1, D, wcols), lmap),
                  pl.BlockSpec((1, CONV_W, 2 * QK_M), lmap),
                  pl.BlockSpec((1, 1, 2 * QK_M), lmap),
                  pl.BlockSpec((1, 1, LANES), lmap),
                  pl.BlockSpec((1, 1, VM), lmap),
                  pl.BlockSpec((1, NH_M, DQK_M, DV_M), lambda b, t: (b, 0, 0, 0)),
                  pl.BlockSpec((1, NH_M, DQK_M), bmap),
                  pl.BlockSpec((1, 1, LANES), bmap),
                  pl.BlockSpec((1, CONV_W - 1, 2 * QK_M), bmap)],
        out_specs=[pl.BlockSpec((1, Tt, VM), lambda b, t: (b, t, 0)),
                   pl.BlockSpec((1, NH_M, DQK_M, DV_M), lambda b, t: (b, 0, 0, 0)),
                   pl.BlockSpec((1, NH_M, DQK_M), bmap),
                   pl.BlockSpec((1, 1, LANES), bmap),
                   pl.BlockSpec((1, CONV_W - 1, 2 * QK_M), bmap)],
        out_shape=[jax.ShapeDtypeStruct((B, T, VM), BF16),
                   jax.ShapeDtypeStruct((B, NH_M, DQK_M, DV_M), F32),
                   jax.ShapeDtypeStruct((B, NH_M, DQK_M), F32),
                   jax.ShapeDtypeStruct((B, 1, LANES), F32),
                   jax.ShapeDtypeStruct((B, CONV_W - 1, 2 * QK_M), F32)],
        scratch_shapes=[pltpu.VMEM((8 + Tt, 2 * QK_M), F32),
                        pltpu.VMEM((Tt, QK_M), F32),
                        pltpu.VMEM((Tt, QK_M), BF16),
                        pltpu.VMEM((Tt, QK_M), BF16),
                        pltpu.VMEM((Tt, VM), BF16),
                        pltpu.VMEM((Tt, VM), F32),
                        pltpu.VMEM((NH_M, DQK_M, Tt), BF16),
                        pltpu.VMEM((Tt // L * NH_M, DQK_M, DV_M), F32),
                        pltpu.VMEM((Tt // L * NH_M, DQK_M), F32),
                        pltpu.VMEM((Tt, VM), F32),
                        pltpu.VMEM((Tt, LANES), F32),
                        pltpu.VMEM((Tt, LANES), F32),
                        pltpu.VMEM((Tt, LANES), F32),
                        pltpu.VMEM((NH_M, DQK_M), F32),
                        pltpu.VMEM((1, LANES), F32)] + [pltpu.VMEM((DQK_M, DV_M), F32)] * NH_M,
        compiler_params=_cparams(2),
        name="mlstm",
    )(x, ada, w, cw, cb, bif, ng, c0t, n0, m0, cv0)


def _pair_cols(x, hd, lane):
    return jnp.where(lane < P_S, x[:, hd:hd + 1], x[:, hd + 1:hd + 2])


def _ssd_kernel(x_ref, ada_ref, w_ref, cw_ref, cb_ref, dtb_ref, alog_ref, d_ref, ng_ref, h0_ref, cv0_ref,
                out_ref, h_out, cv_out,
                conv_s, zg_s, x_s, xw_s, b_s, c_s, bt_s, e2_s, un_s, dec_s, y_s, *ht_s, Tt, L):
    t = pl.program_id(1)
    nc = Tt // L
    n_pairs = NH_S // 2

    @pl.when(t == 0)
    def _():
        for g in range(G_S):
            ht_s[g][...] = h0_ref[0, g]
        conv_s[5:8, :] = cv0_ref[0]

    u = _modulate(x_ref[0], ada_ref[0], 0, 1).astype(BF16)
    dt = _softplus(_dot(u, w_ref[0, :, XS + CONV_S_DIM:]) + dtb_ref[0])
    xbc = _dot(u, w_ref[0, :, XS:XS + CONV_S_DIM])
    a_row = -jnp.exp(alog_ref[0])

    tril = _block_tril(Tt, L)
    full = _block_full(Tt, L)
    hi, mid, lo = _split3(dt * a_row)
    cs = _dot(tril, lo) + _dot(tril, mid) + _dot(tril, hi)
    csl = _dot(full, lo) + _dot(full, mid) + _dot(full, hi)
    xbc = _silu(_causal_conv(conv_s, xbc, cw_ref, cb_ref, Tt))
    x = xbc[:, :XS]
    bm = xbc[:, XS:XS + BC_S]
    x_s[...] = x
    b_s[...] = bm.astype(BF16)
    c_s[...] = xbc[:, XS + BC_S:].astype(BF16)
    wsrc = jnp.exp(csl - cs) * dt
    dec = jnp.exp(csl)
    cs_rows = _transpose_f32(cs)[0:NH_S, :]
    dt_rows = _transpose_f32(dt)[0:NH_S, :]
    for g in range(G_S):
        bt_s[g] = _transpose_f32(bm[:, g * N_S:(g + 1) * N_S]).astype(BF16)
    zg_s[...] = _silu(_dot(u, w_ref[0, :, 0:XS]))
    lane_t = lax.broadcasted_iota(jnp.int32, (Tt, LANES), 1)
    for pp in range(n_pairs):
        pc = slice(pp * LANES, (pp + 1) * LANES)
        e2_s[:, pc] = _pair_cols(cs, 2 * pp, lane_t)
        xw_s[:, pc] = (x[:, pc] * _pair_cols(wsrc, 2 * pp, lane_t)).astype(BF16)
    lane1 = lax.broadcasted_iota(jnp.int32, (1, LANES), 1)
    for c in range(nc):
        for pp in range(n_pairs):
            r0 = c * L
            dec_s[c * n_pairs + pp:c * n_pairs + pp + 1, :] = _pair_cols(dec[r0:r0 + 1, :], 2 * pp, lane1)

    rr = lax.broadcasted_iota(jnp.int32, (L, LANES), 0)
    lane_l = lax.broadcasted_iota(jnp.int32, (L, LANES), 1)
    causal2 = (lane_l % P_S) <= rr
    col = lax.broadcasted_iota(jnp.int32, (N_S, Tt), 1)
    zero_b = jnp.zeros((), BF16)
    for c in range(nc):
        rows = slice(c * L, (c + 1) * L)
        for g in range(G_S):
            gs = slice(g * N_S, (g + 1) * N_S)
            bg = b_s[rows, gs]
            cb2 = _dot_nt(c_s[rows, gs], jnp.concatenate([bg, bg], axis=0))
            btm = jnp.where((col // L) == c, bt_s[g], zero_b) if nc > 1 else bt_s[g]
            un_s[c * G_S + g] = _dot(btm, xw_s[:, g * 2 * LANES:(g + 1) * 2 * LANES])
            for p2 in range(2):
                pp = g * 2 + p2
                hd = 2 * pp
                pc = slice(pp * LANES, (pp + 1) * LANES)
                csr2 = jnp.concatenate([cs_rows[hd:hd + 1, rows], cs_rows[hd + 1:hd + 2, rows]], axis=1)
                dtr2 = jnp.concatenate([dt_rows[hd:hd + 1, rows], dt_rows[hd + 1:hd + 2, rows]], axis=1)
                seg2 = jnp.exp(jnp.where(causal2, e2_s[rows, pc] - csr2, NEG_INF))
                m2 = (cb2 * seg2 * dtr2).astype(BF16)
                xp = x_s[rows, pc].astype(BF16)
                xbd = jnp.concatenate([jnp.where(lane_l < P_S, xp, zero_b), jnp.where(lane_l >= P_S, xp, zero_b)], axis=0)
                y_s[rows, pc] = _dot(m2, xbd)

    for c in range(nc):
        rows = slice(c * L, (c + 1) * L)
        for g in range(G_S):
            gc = slice(g * 2 * LANES, (g + 1) * 2 * LANES)
            yi = _dot(c_s[rows, g * N_S:(g + 1) * N_S], ht_s[g][...].astype(BF16))
            y_s[rows, gc] += jnp.exp(e2_s[rows, gc]) * yi
        for g in range(G_S):
            for p2 in range(2):
                pp = g * 2 + p2
                hc = slice(p2 * LANES, (p2 + 1) * LANES)
                ht_s[g][:, hc] = (dec_s[c * n_pairs + pp:c * n_pairs + pp + 1, :] * ht_s[g][:, hc]
                                  + un_s[c * G_S + g][:, hc])

    y = (y_s[...] + d_ref[0] * x_s[...]) * zg_s[...]
    out_ref[0] = (_group_norm(y, G_S, False) * ng_ref[0]).astype(out_ref.dtype)

    @pl.when(t == pl.num_programs(1) - 1)
    def _():
        for g in range(G_S):
            h_out[0, g] = ht_s[g][...]
        cv_out[0] = conv_s[5:8, :]


def _ssd_call(l, x, ada, w, cw, cb, dtb, alog, dfull, ng, h0t, cv0, Tt):
    B, T, D = x.shape
    L = min(CHUNK, T)
    wcols = w.shape[2]
    bmap = lambda b, t: (b, 0, 0)
    lmap = lambda b, t: (l, 0, 0)
    hshape = (G_S, N_S, HPG_S * P_S)
    assert L == P_S and 2 * L == LANES and T % Tt == 0 and Tt % L == 0
    return pl.pallas_call(
        functools.partial(_ssd_kernel, Tt=Tt, L=L),
        grid=(B, T // Tt),
        in_specs=[pl.BlockSpec((1, Tt, D), lambda b, t: (b, t, 0)),
                  pl.BlockSpec((1, 6, D), bmap),
                  pl.BlockSpec((1, D, wcols), lmap),
                  pl.BlockSpec((1, CONV_W, CONV_S_DIM), lmap),
                  pl.BlockSpec((1, 1, CONV_S_DIM), lmap),
                  pl.BlockSpec((1, 1, LANES), lmap),
                  pl.BlockSpec((1, 1, LANES), lmap),
                  pl.BlockSpec((1, 1, XS), lmap),
                  pl.BlockSpec((1, 1, XS), lmap),
                  pl.BlockSpec((1,) + hshape, lambda b, t: (b, 0, 0, 0)),
                  pl.BlockSpec((1, CONV_W - 1, CONV_S_DIM), bmap)],
        out_specs=[pl.BlockSpec((1, Tt, XS), lambda b, t: (b, t, 0)),
                   pl.BlockSpec((1,) + hshape, lambda b, t: (b, 0, 0, 0)),
                   pl.BlockSpec((1, CONV_W - 1, CONV_S_DIM), bmap)],
        out_shape=[jax.ShapeDtypeStruct((B, T, XS), BF16),
                   jax.ShapeDtypeStruct((B,) + hshape, F32),
                   jax.ShapeDtypeStruct((B, CONV_W - 1, CONV_S_DIM), F32)],
        scratch_shapes=[pltpu.VMEM((8 + Tt, CONV_S_DIM), F32),
                        pltpu.VMEM((Tt, XS), F32),
                        pltpu.VMEM((Tt, XS), F32),
                        pltpu.VMEM((Tt, XS), BF16),
                        pltpu.VMEM((Tt, BC_S), BF16),
                        pltpu.VMEM((Tt, BC_S), BF16),
                        pltpu.VMEM((G_S, N_S, Tt), BF16),
                        pltpu.VMEM((Tt, XS), F32),
                        pltpu.VMEM((Tt // L * G_S, N_S, HPG_S * P_S), F32),
                        pltpu.VMEM((Tt // L * (NH_S // 2), LANES), F32),
                        pltpu.VMEM((Tt, XS), F32)] + [pltpu.VMEM(hshape[1:], F32)] * G_S,
        compiler_params=_cparams(2),
        name="ssd",
    )(x, ada, w, cw, cb, dtb, alog, dfull, ng, h0t, cv0)


def _gla_kernel(x_ref, ada_ref, w_ref, wlr_ref, blr_ref, ng_ref, s0_ref,
                out_ref, s_out,
                sg_s, qg_s, kg_s, v_s, kdt_s, egt_s, un_s, o_s, *st_s, Tt, L):
    t = pl.program_id(1)
    nc = Tt // L

    @pl.when(t == 0)
    def _():
        for h in range(NH_G):
            st_s[h][...] = s0_ref[0, h]

    u = _modulate(x_ref[0], ada_ref[0], 0, 1).astype(BF16)
    lr = _dot(u, w_ref[0, :, 2 * QK_G + 2 * VG:]).astype(BF16)
    lg = _log_sigmoid(_dot(lr, wlr_ref[0]) + blr_ref[0]) / TAU_G
    q = _dot(u, w_ref[0, :, 0:QK_G])
    k = _dot(u, w_ref[0, :, QK_G:2 * QK_G]) * DK_G ** -0.5

    tril = _block_tril(Tt, L)
    hi, mid, lo = _split3(lg)
    G = _dot(tril, lo) + _dot(tril, mid) + _dot(tril, hi)
    v_s[...] = _dot(u, w_ref[0, :, 2 * QK_G:2 * QK_G + VG]).astype(BF16)
    if nc > 1:
        Gl = _chunk_last(G, L)
    else:
        full = _block_full(Tt, L)
        Gl = _dot(full, lo) + _dot(full, mid) + _dot(full, hi)
    qg_s[...] = (q * jnp.exp(G)).astype(BF16)
    kg_s[...] = (k * jnp.exp(-G)).astype(BF16)
    kd = k * jnp.exp(Gl - G)
    eg = jnp.exp(Gl)
    sg_s[...] = _silu(_dot(u, w_ref[0, :, 2 * QK_G + VG:2 * QK_G + 2 * VG]))
    for h in range(NH_G):
        ks = slice(h * DK_G, (h + 1) * DK_G)
        kdt_s[h] = _transpose_f32(kd[:, ks]).astype(BF16)
        egt_s[h] = _transpose_f32(eg[:, ks])
    col = lax.broadcasted_iota(jnp.int32, (DK_G, Tt), 1)
    for c in range(nc):
        for h in range(NH_G):
            vs = slice(h * DV_G, (h + 1) * DV_G)
            kdm = jnp.where((col // L) == c, kdt_s[h], jnp.zeros((), BF16)) if nc > 1 else kdt_s[h]
            un_s[c * NH_G + h] = _dot(kdm, v_s[:, vs])

    rr = lax.broadcasted_iota(jnp.int32, (L, L), 0)
    cc = lax.broadcasted_iota(jnp.int32, (L, L), 1)
    causal = cc <= rr
    for c in range(nc):
        rows = slice(c * L, (c + 1) * L)
        att = []
        for h in range(NH_G):
            ks = slice(h * DK_G, (h + 1) * DK_G)
            att.append(_dot_nt(qg_s[rows, ks], kg_s[rows, ks]))
        for h in range(NH_G):
            ks = slice(h * DK_G, (h + 1) * DK_G)
            vs = slice(h * DV_G, (h + 1) * DV_G)
            o_s[rows, vs] = _dot(qg_s[rows, ks], st_s[h][...].astype(BF16))
        for h in range(NH_G):
            vs = slice(h * DV_G, (h + 1) * DV_G)
            o_s[rows, vs] += _dot(jnp.where(causal, att[h], 0.0).astype(BF16), v_s[rows, vs])
        for h in range(NH_G):
            st_s[h][...] = egt_s[h][:, c * L:c * L + 1] * st_s[h][...] + un_s[c * NH_G + h]

    out_ref[0] = (_group_norm(o_s[...], NH_G, False) * ng_ref[0] * sg_s[...]).astype(out_ref.dtype)

    @pl.when(t == pl.num_programs(1) - 1)
    def _():
        for h in range(NH_G):
            s_out[0, h] = st_s[h][...]


def _gla_call(l, x, ada, w, wlr, blr, ng, s0, Tt):
    B, T, D = x.shape
    L = min(CHUNK, T)
    wcols = w.shape[2]
    bmap = lambda b, t: (b, 0, 0)
    lmap = lambda b, t: (l, 0, 0)
    sshape = (NH_G, DK_G, DV_G)
    return pl.pallas_call(
        functools.partial(_gla_kernel, Tt=Tt, L=L),
        grid=(B, T // Tt),
        in_specs=[pl.BlockSpec((1, Tt, D), lambda b, t: (b, t, 0)),
                  pl.BlockSpec((1, 6, D), bmap),
                  pl.BlockSpec((1, D, wcols), lmap),
                  pl.BlockSpec((1, LANES, QK_G), lmap),
                  pl.BlockSpec((1, 1, QK_G), lmap),
                  pl.BlockSpec((1, 1, VG), lmap),
                  pl.BlockSpec((1,) + sshape, lambda b, t: (b, 0, 0, 0))],
        out_specs=[pl.BlockSpec((1, Tt, VG), lambda b, t: (b, t, 0)),
                   pl.BlockSpec((1,) + sshape, lambda b, t: (b, 0, 0, 0))],
        out_shape=[jax.ShapeDtypeStruct((B, T, VG), BF16),
                   jax.ShapeDtypeStruct((B,) + sshape, F32)],
        scratch_shapes=[pltpu.VMEM((Tt, VG), F32),
                        pltpu.VMEM((Tt, QK_G), BF16),
                        pltpu.VMEM((Tt, QK_G), BF16),
                        pltpu.VMEM((Tt, VG), BF16),
                        pltpu.VMEM((NH_G, DK_G, Tt), BF16),
                        pltpu.VMEM((NH_G, DK_G, Tt), F32),
                        pltpu.VMEM((Tt // L * NH_G, DK_G, DV_G), F32),
                        pltpu.VMEM((Tt, VG), F32)] + [pltpu.VMEM(sshape[1:], F32)] * NH_G,
        compiler_params=_cparams(2),
        name="gla",
    )(x, ada, w, wlr, blr, ng, s0)


def _layer_norm(x, g, b):
    mu = jnp.mean(x, axis=1, keepdims=True)
    xc = x - mu
    var = jnp.mean(xc * xc, axis=1, keepdims=True)
    return xc * lax.rsqrt(var + LN_EPS) * g + b


def _merge_kernel(x_ref, ada_ref, hm_ref, ys_ref, og_ref, wg_ref, bb_ref, wb_ref, wo_ref, lng_ref, lnb_ref,
                  wrt_ref, brt_ref,
                  x1_ref, u2_ref, eid_ref, wts_ref, cnt_ref, cnt_s, *, alpha):
    x = x_ref[0]
    ada = ada_ref[0]
    D = x.shape[1]
    u = _modulate(x, ada, 0, 1).astype(BF16)
    gate = jax.nn.sigmoid(_dot(u, wg_ref[0]) + bb_ref[0])
    merged = (gate[:, 0:D] * _dot(hm_ref[0], wb_ref[0, 0])
              + gate[:, D:2 * D] * _dot(ys_ref[0], wb_ref[0, 1])
              + gate[:, 2 * D:3 * D] * _dot(og_ref[0], wb_ref[0, 2]))
    y = _dot(merged.astype(BF16), wo_ref[0])
    x1 = _layer_norm(alpha * x + ada[2:3] * y, lng_ref[0, 0:1], lnb_ref[0, 0:1])
    x1_ref[0] = x1
    u2 = _modulate(x1, ada, 3, 4).astype(BF16)
    bits = lax.bitcast_convert_type(u2.astype(F32), jnp.uint32)
    u2_ref[0] = (bits[:, :D // 2] >> 16) | bits[:, D // 2:]

    logits = _dot(u2, wrt_ref[0]) + brt_ref[0]
    lane = lax.broadcasted_iota(jnp.int32, logits.shape, 1)
    lane_f = lane.astype(F32)
    big = float(LANES)
    is_g = (lane >= N_EXPERTS) & (lane < N_EXPERTS + N_GROUPS)
    gmax = jnp.max(jnp.where(is_g, logits, NEG_INF), axis=1, keepdims=True)
    gsum = jnp.sum(jnp.where(is_g, jnp.exp(logits - gmax), 0.0), axis=1, keepdims=True)
    pg_top = 1.0 / gsum
    g_lane = jnp.min(jnp.where(is_g & (logits == gmax), lane_f, big), axis=1, keepdims=True)
    g_idx = g_lane.astype(jnp.int32) - N_EXPERTS
    in_grp = (lane < N_EXPERTS) & ((lane // EXP_PER_GROUP) == g_idx)
    el = jnp.where(in_grp, logits, NEG_INF)
    v1 = jnp.max(el, axis=1, keepdims=True)
    i1 = jnp.min(jnp.where(in_grp & (el == v1), lane_f, big), axis=1, keepdims=True)
    rest = in_grp & (lane_f != i1)
    el2 = jnp.where(rest, logits, NEG_INF)
    v2 = jnp.max(el2, axis=1, keepdims=True)
    i2 = jnp.min(jnp.where(rest & (el2 == v2), lane_f, big), axis=1, keepdims=True)
    e = jnp.exp(v2 - v1)
    w1 = pg_top / (1.0 + e)
    w2 = pg_top * e / (1.0 + e)
    @pl.when((pl.program_id(0) == 0) & (pl.program_id(1) == 0))
    def _():
        cnt_s[...] = jnp.zeros_like(cnt_s)

    tm = logits.shape[0]
    hit1 = lane_f == i1
    hit2 = lane_f == i2
    onehot = jnp.where(hit1 | hit2, 1.0, 0.0)
    rr = lax.broadcasted_iota(jnp.int32, (tm, tm), 0)
    cc = lax.broadcasted_iota(jnp.int32, (tm, tm), 1)
    before = _dot(jnp.where(cc < rr, 1.0, 0.0).astype(BF16), onehot.astype(BF16)) + cnt_s[...]
    r1 = jnp.sum(jnp.where(hit1, before, 0.0), axis=1, keepdims=True)
    r2 = jnp.sum(jnp.where(hit2, before, 0.0), axis=1, keepdims=True)
    cnt_s[...] = cnt_s[...] + jnp.sum(onehot, axis=0, keepdims=True)
    cnt_ref[0] = cnt_s[...]
    eid_ref[0] = jnp.where(lane == 0, i1, jnp.where(lane == 1, i2, jnp.where(lane == 2, r1, jnp.where(lane == 3, r2, 0.0)))).astype(jnp.int32)
    wts_ref[0] = jnp.where(lane == 0, w1, jnp.where(lane == 1, w2, 0.0))


def _merge_call(l, x, ada, hm, ys, og, wg, bb, wb, wo, lng, lnb, wrt, brt, tm, alpha):
    B, T, D = x.shape
    bmap = lambda b, t: (b, 0, 0)
    lmap = lambda b, t: (l, 0, 0)
    tmap = lambda b, t: (b, t, 0)
    return pl.pallas_call(
        functools.partial(_merge_kernel, alpha=alpha),
        grid=(B, T // tm),
        in_specs=[pl.BlockSpec((1, tm, D), tmap),
                  pl.BlockSpec((1, 6, D), bmap),
                  pl.BlockSpec((1, tm, D), tmap),
                  pl.BlockSpec((1, tm, D), tmap),
                  pl.BlockSpec((1, tm, D), tmap),
                  pl.BlockSpec((1, D, 3 * D), lmap),
                  pl.BlockSpec((1, 1, 3 * D), lmap),
                  pl.BlockSpec((1, 3, D, D), lambda b, t: (l, 0, 0, 0)),
                  pl.BlockSpec((1, D, D), lmap),
                  pl.BlockSpec((1, 2, D), lmap),
                  pl.BlockSpec((1, 2, D), lmap),
                  pl.BlockSpec((1, D, LANES), lmap),
                  pl.BlockSpec((1, 1, LANES), lmap)],
        out_specs=[pl.BlockSpec((1, tm, D), tmap),
                   pl.BlockSpec((1, tm, D // 2), tmap),
                   pl.BlockSpec((1, tm, LANES), tmap),
                   pl.BlockSpec((1, tm, LANES), tmap),
                   pl.BlockSpec((1, 1, LANES), lambda b, t: (0, 0, 0))],
        out_shape=[jax.ShapeDtypeStruct((B, T, D), F32),
                   jax.ShapeDtypeStruct((B, T, D // 2), jnp.uint32),
                   jax.ShapeDtypeStruct((B, T, LANES), jnp.int32),
                   jax.ShapeDtypeStruct((B, T, LANES), F32),
                   jax.ShapeDtypeStruct((1, 1, LANES), F32)],
        scratch_shapes=[pltpu.VMEM((1, LANES), F32)],
        compiler_params=_cparams(2),
        name="merge",
    )(x, ada, hm, ys, og, wg, bb, wb, wo, lng, lnb, wrt, brt)


def _moe_kernel(te_ref, xs_ref, wg_ref, wu_ref, wd_ref, o_ref, wg_b, wu_b, wd_b):
    i = pl.program_id(0)
    prev = te_ref[jnp.maximum(i - 1, 0)]

    @pl.when((i == 0) | (te_ref[i] != prev))
    def _():
        wg_b[...] = wg_ref[0, 0].astype(BF16)
        wu_b[...] = wu_ref[0, 0].astype(BF16)
        wd_b[...] = wd_ref[0, 0].astype(BF16)

    w = xs_ref[...]
    lo = lax.bitcast_convert_type(w << 16, F32)
    hi = lax.bitcast_convert_type(w & jnp.uint32(0xFFFF0000), F32)
    xs = jnp.concatenate([lo, hi], axis=1).astype(BF16)
    hg = _dot(xs, wg_b[...])
    hu = _dot(xs, wu_b[...])
    o_ref[...] = _dot((_silu(hg) * hu).astype(BF16), wd_b[...])


def _moe_call(l, tile_expert, xs, weg, weu, wed, tm):
    R = xs.shape[0]
    D, dexp = weg.shape[2], weg.shape[3]
    return pl.pallas_call(
        _moe_kernel,
        grid_spec=pltpu.PrefetchScalarGridSpec(
            num_scalar_prefetch=1,
            grid=(R // tm,),
            in_specs=[pl.BlockSpec((tm, D // 2), lambda i, te: (i, 0)),
                      pl.BlockSpec((1, 1, D, dexp), lambda i, te: (l, te[i], 0, 0)),
                      pl.BlockSpec((1, 1, D, dexp), lambda i, te: (l, te[i], 0, 0)),
                      pl.BlockSpec((1, 1, dexp, D), lambda i, te: (l, te[i], 0, 0))],
            out_specs=pl.BlockSpec((tm, D), lambda i, te: (i, 0)),
            scratch_shapes=[pltpu.VMEM((D, dexp), BF16), pltpu.VMEM((D, dexp), BF16), pltpu.VMEM((dexp, D), BF16)]),
        out_shape=jax.ShapeDtypeStruct((R, D), F32),
        compiler_params=_cparams(1),
        name="moe",
    )(tile_expert, xs, weg, weu, wed)


def _ln2_kernel(x1_ref, ada_ref, y0_ref, y1_ref, wts_ref, lng_ref, lnb_ref, o_ref, *, alpha):
    ada = ada_ref[0]
    wts = wts_ref[0]
    moe = y0_ref[0] * wts[:, 0:1] + y1_ref[0] * wts[:, 1:2]
    o_ref[0] = _layer_norm(alpha * x1_ref[0] + ada[5:6] * moe, lng_ref[0, 1:2], lnb_ref[0, 1:2])


def _ln2_call(l, x1, ada, y0, y1, wts, lng, lnb, tm, alpha):
    B, T, D = x1.shape
    bmap = lambda b, t: (b, 0, 0)
    lmap = lambda b, t: (l, 0, 0)
    tmap = lambda b, t: (b, t, 0)
    return pl.pallas_call(
        functools.partial(_ln2_kernel, alpha=alpha),
        grid=(B, T // tm),
        in_specs=[pl.BlockSpec((1, tm, D), tmap),
                  pl.BlockSpec((1, 6, D), bmap),
                  pl.BlockSpec((1, tm, D), tmap),
                  pl.BlockSpec((1, tm, D), tmap),
                  pl.BlockSpec((1, tm, LANES), tmap),
                  pl.BlockSpec((1, 2, D), lmap),
                  pl.BlockSpec((1, 2, D), lmap)],
        out_specs=pl.BlockSpec((1, tm, D), tmap),
        out_shape=jax.ShapeDtypeStruct((B, T, D), F32),
        compiler_params=_cparams(2),
        name="ln2",
    )(x1, ada, y0, y1, wts, lng, lnb)


def _take_rows(a, idx):
    return a.at[idx].get(mode="promise_in_bounds")


def _lookup(table, idx):
    sel = idx[:, None] == jnp.arange(table.shape[0], dtype=idx.dtype)[None, :]
    return jnp.sum(jnp.where(sel, table[None, :], 0), axis=1)


def _moe_tile(n_asg):
    return MOE_TILE if n_asg >= 4 * MOE_TILE * N_EXPERTS else MOE_TILE_SMALL


def _route(eid, rank, counts, n_tok, tm):
    flat = eid.reshape(-1)
    n_asg = flat.shape[0]
    n_rows = (-(-n_asg // tm) + N_EXPERTS) * tm
    sizes = counts.astype(jnp.int32)
    psz = ((sizes + tm - 1) // tm) * tm
    pend = jnp.cumsum(psz)
    poff = pend - psz
    dest = _lookup(poff, flat) + rank.reshape(-1)
    fill_end = jnp.cumsum(psz - sizes)
    fill = jnp.arange(n_rows - n_asg, dtype=jnp.int32)
    fill_key = jnp.sum((fill_end[None, :] <= fill[:, None]).astype(jnp.int32), axis=1)
    order = jnp.argsort(jnp.concatenate([flat, fill_key])).astype(jnp.int32)
    row_tok = jnp.where(order < n_asg, order // TOP_K, jnp.arange(n_rows, dtype=jnp.int32) % n_tok)
    tile_start = jnp.arange(n_rows // tm, dtype=jnp.int32) * tm
    tile_expert = jnp.minimum(jnp.sum((pend[None, :] <= tile_start[:, None]).astype(jnp.int32), axis=1),
                              N_EXPERTS - 1)
    return row_tok, dest.reshape(n_tok, TOP_K), tile_expert


def _pad_lanes(a, width=LANES):
    return jnp.pad(a, [(0, 0)] * (a.ndim - 1) + [(0, width - a.shape[-1])])


def _time_block(T):
    return min(T, 256)


def _merge_rows(T):
    return min(T, 512)


def _ln2_rows(T):
    return min(T, 1024)


def _trunk(x, ada_all, states, P, depth, alpha):
    B, T, D = x.shape
    Tt = _time_block(T)
    c_m, n_m, m_m, cv_m, h_s, cv_s, s_g = states
    new = [[] for _ in range(7)]
    for l in range(depth):
        ada = ada_all[l].reshape(B, 6, D)
        c0t = jnp.swapaxes(c_m[l], -1, -2)
        m0 = _pad_lanes(m_m[l])[:, None, :]
        h0t = (h_s[l].reshape(B, G_S, HPG_S, P_S, N_S).transpose(0, 1, 4, 2, 3)
               .reshape(B, G_S, N_S, HPG_S * P_S))
        hm, c_t, n_n, m_n, cvm_n = _mlstm_call(l, x, ada, P["w_mlstm"], P["mlstm_conv_w"], P["mlstm_conv_b"],
                                               P["mlstm_bif"], P["mlstm_norm_g"], c0t, n_m[l], m0, cv_m[l], Tt)
        ys, h_t, cvs_n = _ssd_call(l, x, ada, P["w_ssd"], P["ssd_conv_w"], P["ssd_conv_b"], P["ssd_dtb"],
                                   P["ssd_alog"], P["ssd_dfull"], P["ssd_norm_g"], h0t, cv_s[l], Tt)
        og, s_n = _gla_call(l, x, ada, P["w_gla"], P["gla_w_lr"], P["gla_b_lr"], P["gla_norm_g"], s_g[l], Tt)
        x1, u2, eid, wts, cnt = _merge_call(l, x, ada, hm, ys, og, P["w_gate"], P["b_branch"], P["w_branch"],
                                       P["w_out"], P["ln_g"], P["ln_b"], P["w_rt"], P["b_rt"], _merge_rows(T), alpha)
        moe_tm = _moe_tile(B * T * TOP_K)
        row_tok, dest, tile_expert = _route(eid[:, :, :TOP_K], eid[:, :, TOP_K:2 * TOP_K], cnt[0, 0, :N_EXPERTS], B * T, moe_tm)
        xs = _take_rows(u2.reshape(B * T, D // 2), row_tok)
        ye = _moe_call(l, tile_expert, xs, P["w_e_gate"], P["w_e_up"], P["w_e_down"], moe_tm)
        y0 = _take_rows(ye, dest[:, 0]).reshape(B, T, D)
        y1 = _take_rows(ye, dest[:, 1]).reshape(B, T, D)
        x = _ln2_call(l, x1, ada, y0, y1, wts, P["ln_g"], P["ln_b"], _ln2_rows(T), alpha)

        new[0].append(jnp.swapaxes(c_t, -1, -2))
        new[1].append(n_n)
        new[2].append(m_n[:, 0, :NH_M])
        new[3].append(cvm_n)
        new[4].append(h_t.reshape(B, G_S, N_S, HPG_S, P_S).transpose(0, 1, 3, 4, 2).reshape(B, NH_S, P_S, N_S))
        new[5].append(cvs_n)
        new[6].append(s_n)
    return x, tuple(jnp.stack(lst) for lst in new)


def kernel(x_prompt, x_sample, state_mlstm_c, state_mlstm_n, state_mlstm_m, state_mlstm_conv, state_ssd, state_ssd_conv, state_gla, c_prompt, c_sample, w_ada, b_ada, w_in, mlstm_b_i, mlstm_b_f, mlstm_conv_w, mlstm_conv_b, mlstm_norm_g, ssd_conv_w, ssd_conv_b, ssd_dt_bias, ssd_a_log, ssd_d, ssd_norm_g, gla_w_lr, gla_b_lr, gla_norm_g, b_branch, w_branch, w_out, ln_g, ln_b, w_grp, b_grp, w_router, b_router, w_e_gate, w_e_up, w_e_down):
    depth, D, _ = w_in.shape
    alpha = (2 * depth) ** 0.25
    nbp = x_prompt.shape[0]

    edges = np.concatenate([[0], np.cumsum(COL_SIZES)])
    col = {n: w_in[:, :, int(edges[i]):int(edges[i + 1])] for i, n in enumerate(COL_NAMES)}
    w_gate = w_in[:, :, int(edges[-1]):].astype(BF16)
    cat = lambda parts: jnp.concatenate(parts, axis=-1).astype(BF16)
    row = lambda a: a[:, None, :]
    P = {
        "w_mlstm": cat([col["qk_m"], col["v_m"], col["o_m"], _pad_lanes(jnp.concatenate([col["i_m"], col["f_m"]], -1))]),
        "w_ssd": cat([col["z_s"], col["xbc_s"], _pad_lanes(col["dt_s"])]),
        "w_gla": cat([col["q_g"], col["k_g"], col["v_g"], col["g_g"], _pad_lanes(col["lr_g"])]),
        "w_gate": w_gate,
        "mlstm_conv_w": mlstm_conv_w, "mlstm_conv_b": row(mlstm_conv_b),
        "mlstm_bif": row(_pad_lanes(jnp.concatenate([mlstm_b_i, mlstm_b_f], -1))),
        "mlstm_norm_g": row(mlstm_norm_g),
        "ssd_conv_w": ssd_conv_w, "ssd_conv_b": row(ssd_conv_b),
        "ssd_dtb": row(_pad_lanes(ssd_dt_bias)), "ssd_alog": row(_pad_lanes(ssd_a_log)),
        "ssd_dfull": row(jnp.repeat(ssd_d, P_S, axis=-1)), "ssd_norm_g": row(ssd_norm_g),
        "gla_w_lr": jnp.pad(gla_w_lr, ((0, 0), (0, LANES - R_G), (0, 0))).astype(BF16),
        "gla_b_lr": row(gla_b_lr), "gla_norm_g": row(gla_norm_g),
        "b_branch": row(b_branch), "w_branch": w_branch.astype(BF16), "w_out": w_out.astype(BF16),
        "ln_g": ln_g, "ln_b": ln_b,
        "w_rt": _pad_lanes(jnp.concatenate([w_router, w_grp], -1)).astype(BF16),
        "b_rt": row(_pad_lanes(jnp.concatenate([b_router, b_grp], -1))),
        "w_e_gate": w_e_gate, "w_e_up": w_e_up, "w_e_down": w_e_down,
    }

    ada_all = _ada_call(jnp.concatenate([c_prompt, c_sample], axis=0), w_ada, b_ada)

    sample_states = (state_mlstm_c, state_mlstm_n, state_mlstm_m, state_mlstm_conv,
                     state_ssd, state_ssd_conv, state_gla)
    prompt_states = tuple(jnp.zeros((s.shape[0], nbp) + s.shape[2:], x_prompt.dtype) for s in sample_states)

    y_prompt, new_p = _trunk(x_prompt, ada_all[:, :nbp], prompt_states, P, depth, alpha)
    y_sample, new_s = _trunk(x_sample, ada_all[:, nbp:], sample_states, P, depth, alpha)
    return (y_prompt, y_sample) + new_p + new_s
```

```python
import functools

import jax
import jax.numpy as jnp
import numpy as np
from jax import lax
from jax.experimental import pallas as pl
from jax.experimental.pallas import tpu as pltpu

F32 = jnp.float32
BF16 = jnp.bfloat16
NEG_INF = float("-inf")

CHUNK = 64
CONV_W = 4
NH_M, DQK_M, DV_M = 4, 128, 256
QK_M, VM = NH_M * DQK_M, NH_M * DV_M
NH_S, P_S, N_S, G_S = 16, 64, 128, 4
HPG_S = NH_S // G_S
XS, BC_S = NH_S * P_S, G_S * N_S
CONV_S_DIM = XS + 2 * BC_S
NH_G, DK_G, DV_G = 4, 128, 256
QK_G, VG = NH_G * DK_G, NH_G * DV_G
R_G = 16
TAU_G = 16.0
N_GROUPS, EXP_PER_GROUP, TOP_K = 4, 8, 2
N_EXPERTS = N_GROUPS * EXP_PER_GROUP
LN_EPS = 1e-5
LANES = 128

COL_SIZES = (2 * QK_M, VM, VM, NH_M, NH_M, XS, CONV_S_DIM, NH_S, QK_G, QK_G, VG, VG, R_G)
COL_NAMES = ("qk_m", "v_m", "o_m", "i_m", "f_m", "z_s", "xbc_s", "dt_s", "q_g", "k_g", "v_g", "g_g", "lr_g")

VMEM_LIMIT = 56 * 1024 * 1024
MOE_TILE = 512
MOE_TILE_SMALL = 128


def _cparams(n_axes):
    return pltpu.CompilerParams(dimension_semantics=("arbitrary",) * n_axes, vmem_limit_bytes=VMEM_LIMIT)


def _dot(a, b):
    return jnp.dot(a, b, preferred_element_type=F32)


def _dot_nt(a, b):
    return lax.dot_general(a, b, (((1,), (1,)), ((), ())), preferred_element_type=F32)


def _split3(x):
    hi = x.astype(BF16)
    r = x - hi.astype(F32)
    mid = r.astype(BF16)
    lo = (r - mid.astype(F32)).astype(BF16)
    return hi, mid, lo


def _eye(n, m):
    r = lax.broadcasted_iota(jnp.int32, (n, m), 0)
    c = lax.broadcasted_iota(jnp.int32, (n, m), 1)
    return jnp.where(r == c, 1.0, 0.0).astype(BF16)


def _transpose_rows(x, n):
    e = _eye(n, x.shape[1])
    hi, mid, lo = _split3(x)
    return _dot_nt(e, lo) + _dot_nt(e, mid) + _dot_nt(e, hi)


def _block_tril(Tt, L):
    r = lax.broadcasted_iota(jnp.int32, (Tt, Tt), 0)
    c = lax.broadcasted_iota(jnp.int32, (Tt, Tt), 1)
    return jnp.where(((r // L) == (c // L)) & (c <= r), 1.0, 0.0).astype(BF16)


def _block_full(Tt, L):
    r = lax.broadcasted_iota(jnp.int32, (Tt, Tt), 0)
    c = lax.broadcasted_iota(jnp.int32, (Tt, Tt), 1)
    return jnp.where((r // L) == (c // L), 1.0, 0.0).astype(BF16)


def _chunk_last(x, L):
    n = x.shape[0] // L
    return jnp.concatenate([jnp.broadcast_to(x[(c + 1) * L - 1:(c + 1) * L, :], (L, x.shape[1])) for c in range(n)], axis=0)


def _transpose_f32(x):
    if x.shape[0] % LANES == 0:
        return x.T
    return _transpose_rows(x, x.shape[1])


def _log_sigmoid(x):
    return jnp.minimum(x, 0.0) - jnp.log1p(jnp.exp(-jnp.abs(x)))


def _softplus(x):
    return jnp.maximum(x, 0.0) + jnp.log1p(jnp.exp(-jnp.abs(x)))


def _silu(x):
    return x * jax.nn.sigmoid(x)


def _modulate(x, ada, shift_row, scale_row):
    return x * (1.0 + ada[scale_row:scale_row + 1]) + ada[shift_row:shift_row + 1]


def _causal_conv(buf, x, w_ref, b_ref, Tt):
    buf[8:8 + Tt, :] = x
    y = buf[5:5 + Tt, :] * w_ref[0, 0:1, :]
    y = y + buf[6:6 + Tt, :] * w_ref[0, 1:2, :]
    y = y + buf[7:7 + Tt, :] * w_ref[0, 2:3, :]
    y = y + x * w_ref[0, 3:4, :]
    y = y + b_ref[0]
    buf[5:8, :] = buf[5 + Tt:8 + Tt, :]
    return y


def _group_norm(x, n_groups, center):
    w = x.shape[1] // n_groups
    outs = []
    for g in range(n_groups):
        xg = x[:, g * w:(g + 1) * w]
        if center:
            xg = xg - jnp.mean(xg, axis=1, keepdims=True)
        outs.append(xg * lax.rsqrt(jnp.mean(xg * xg, axis=1, keepdims=True) + LN_EPS))
    return jnp.concatenate(outs, axis=1)


def _ada_kernel(c_ref, w_ref, b_ref, o_ref):
    c = _silu(c_ref[...]).astype(BF16)
    o_ref[0] = _dot(c, w_ref[0].astype(BF16)) + b_ref[0]


def _ada_call(c_all, w_ada, b_ada):
    depth, d, n6 = w_ada.shape
    nb = c_all.shape[0]
    tn = 1536
    return pl.pallas_call(
        _ada_kernel,
        grid=(depth, n6 // tn),
        in_specs=[pl.BlockSpec((nb, d), lambda l, j: (0, 0)),
                  pl.BlockSpec((1, d, tn), lambda l, j: (l, 0, j)),
                  pl.BlockSpec((1, 1, tn), lambda l, j: (l, 0, j))],
        out_specs=pl.BlockSpec((1, nb, tn), lambda l, j: (l, 0, j)),
        out_shape=jax.ShapeDtypeStruct((depth, nb, n6), F32),
        compiler_params=_cparams(2),
        name="ada",
    )(c_all, w_ada, b_ada.reshape(depth, 1, n6))


def _mlstm_kernel(x_ref, ada_ref, w_ref, cw_ref, cb_ref, bif_ref, ng_ref, c0_ref, n0_ref, m0_ref, cv0_ref,
                  out_ref, c_out, n_out, m_out, cv_out,
                  conv_s, q_s, qb_s, kb_s, v_s, o_s, kwt_s, un_s, nu_s, p_s, rs_s, b_s, ml_s, n_s, m_s, *ct_s, Tt, L):
    t = pl.program_id(1)
    nc = Tt // L

    @pl.when(t == 0)
    def _():
        for h in range(NH_M):
            ct_s[h][...] = c0_ref[0, h]
        n_s[...] = n0_ref[0]
        m_s[...] = m0_ref[0]
        conv_s[5:8, :] = cv0_ref[0]

    u = _modulate(x_ref[0], ada_ref[0], 0, 1).astype(BF16)
    g = _dot(u, w_ref[0, :, 2 * QK_M + 2 * VM:]) + bif_ref[0]
    qk = _dot(u, w_ref[0, :, 0:2 * QK_M])
    lane = lax.broadcasted_iota(jnp.int32, g.shape, 1)
    g = jnp.where(lane < NH_M, g, _log_sigmoid(g))

    tril = _block_tril(Tt, L)
    hi, mid, lo = _split3(g)
    cs = _dot(tril, lo) + _dot(tril, mid) + _dot(tril, hi)
    v_s[...] = _dot(u, w_ref[0, :, 2 * QK_M:2 * QK_M + VM]).astype(BF16)
    qk = _silu(_causal_conv(conv_s, qk, cw_ref, cb_ref, Tt))
    q = qk[:, :QK_M]
    k = qk[:, QK_M:] * DQK_M ** -0.5
    q_s[...] = q
    qb_s[...] = q.astype(BF16)
    kb_s[...] = k.astype(BF16)
    o_s[...] = _dot(u, w_ref[0, :, 2 * QK_M + VM:2 * QK_M + 2 * VM])
    b0 = pltpu.roll(cs, LANES - NH_M, axis=1)
    r = g - b0
    row = lax.broadcasted_iota(jnp.int32, g.shape, 0) % L
    a = r
    sh = 1
    while sh < L:
        a = jnp.maximum(a, jnp.where(row >= sh, pltpu.roll(a, sh, axis=0), NEG_INF))
        sh *= 2
    al = a
    sh = 1
    while sh < L:
        al = jnp.maximum(al, jnp.where(row + sh < L, pltpu.roll(al, Tt - sh, axis=0), NEG_INF))
        sh *= 2
    b_s[...] = b0
    ml_s[...] = b0 + a
    wsrc = jnp.exp(r - al)
    r_rows = _transpose_f32(r)[0:8, :]
    col = lax.broadcasted_iota(jnp.int32, (DQK_M, Tt), 1)
    rr = lax.broadcasted_iota(jnp.int32, (L, L), 0)
    cc = lax.broadcasted_iota(jnp.int32, (L, L), 1)
    causal = cc <= rr
    for h in range(NH_M):
        ks = slice(h * DQK_M, (h + 1) * DQK_M)
        kw = wsrc[:, h:h + 1] * k[:, ks]
        kwt_s[h] = _transpose_f32(kw).astype(BF16)
        for c in range(nc):
            nu_s[c * NH_M + h:c * NH_M + h + 1, :] = jnp.sum(kw[c * L:(c + 1) * L], axis=0, keepdims=True)
    for c in range(nc):
        for h in range(NH_M):
            vs = slice(h * DV_M, (h + 1) * DV_M)
            kwm = jnp.where((col // L) == c, kwt_s[h], jnp.zeros((), BF16)) if nc > 1 else kwt_s[h]
            un_s[c * NH_M + h] = _dot(kwm, v_s[:, vs])
    for c in range(nc):
        rows = slice(c * L, (c + 1) * L)
        sl = []
        for h in range(NH_M):
            ks = slice(h * DQK_M, (h + 1) * DQK_M)
            dloc = jnp.exp(jnp.where(causal, r_rows[h:h + 1, rows] - a[rows, h:h + 1], NEG_INF))
            sl.append(_dot_nt(qb_s[rows, ks], kb_s[rows, ks]) * dloc)
        for h in range(NH_M):
            vs = slice(h * DV_M, (h + 1) * DV_M)
            p_s[rows, vs] = _dot(sl[h].astype(BF16), v_s[rows, vs])
            rs_s[rows, h:h + 1] = jnp.sum(sl[h], axis=1, keepdims=True)

    lane_l = lax.broadcasted_iota(jnp.int32, (L, LANES), 1)
    for c in range(nc):
        rows = slice(c * L, (c + 1) * L)
        m_prev = m_s[...]
        bc = b_s[rows, :]
        mlc = ml_s[rows, :]
        mt = jnp.maximum(bc + m_prev, mlc)
        corr = jnp.exp(mlc - mt)
        w_inter = jnp.exp(bc + m_prev - mt)
        inter = []
        qn = jnp.zeros((L, LANES), F32)
        for h in range(NH_M):
            ks = slice(h * DQK_M, (h + 1) * DQK_M)
            inter.append(_dot(qb_s[rows, ks], ct_s[h][...].astype(BF16)))
            qn_h = jnp.sum(q_s[rows, ks] * n_s[h:h + 1, :], axis=1, keepdims=True)
            qn = jnp.where(lane_l == h, qn_h, qn)
        den = corr * rs_s[rows, :] + w_inter * qn
        inv = 1.0 / jnp.maximum(jnp.abs(den), jnp.exp(-mt))
        ca = corr * inv
        cb2 = w_inter * inv
        for h in range(NH_M):
            vs = slice(h * DV_M, (h + 1) * DV_M)
            p_s[rows, vs] = ca[:, h:h + 1] * p_s[rows, vs] + cb2[:, h:h + 1] * inter[h]
        m_last = mt[L - 1:L, :]
        c_l = jnp.exp(mlc[L - 1:L, :] - m_last)
        decay = jnp.exp(bc[L - 1:L, :] + m_prev - m_last)
        for h in range(NH_M):
            ct_s[h][...] = decay[:, h:h + 1] * ct_s[h][...] + c_l[:, h:h + 1] * un_s[c * NH_M + h]
            n_s[h:h + 1, :] = decay[:, h:h + 1] * n_s[h:h + 1, :] + c_l[:, h:h + 1] * nu_s[c * NH_M + h:c * NH_M + h + 1, :]
        m_s[...] = m_last

    hn = _group_norm(p_s[...], NH_M, True) * ng_ref[0] * jax.nn.sigmoid(o_s[...])
    out_ref[0] = hn.astype(out_ref.dtype)

    @pl.when(t == pl.num_programs(1) - 1)
    def _():
        for h in range(NH_M):
            c_out[0, h] = ct_s[h][...]
        n_out[0] = n_s[...]
        m_out[0] = m_s[...]
        cv_out[0] = conv_s[5:8, :]


def _mlstm_call(l, x, ada, w, cw, cb, bif, ng, c0t, n0, m0, cv0, Tt):
    B, T, D = x.shape
    L = min(CHUNK, T)
    wcols = w.shape[2]
    bmap = lambda b, t: (b, 0, 0)
    lmap = lambda b, t: (l, 0, 0)
    return pl.pallas_call(
        functools.partial(_mlstm_kernel, Tt=Tt, L=L),
        grid=(B, T // Tt),
        in_specs=[pl.BlockSpec((1, Tt, D), lambda b, t: (b, t, 0)),
                  pl.BlockSpec((1, 6, D), bmap),
                  pl.BlockSpec((1, D, wcols), lmap),
                  pl.BlockSpec((1, CONV_W, 2 * QK_M), lmap),
                  pl.BlockSpec((1, 1, 2 * QK_M), lmap),
                  pl.BlockSpec((1, 1, LANES), lmap),
                  pl.BlockSpec((1, 1, VM), lmap),
                  pl.BlockSpec((1, NH_M, DQK_M, DV_M), lambda b, t: (b, 0, 0, 0)),
                  pl.BlockSpec((1, NH_M, DQK_M), bmap),
                  pl.BlockSpec((1, 1, LANES), bmap),
                  pl.BlockSpec((1, CONV_W - 1, 2 * QK_M), bmap)],
        out_specs=[pl.BlockSpec((1, Tt, VM), lambda b, t: (b, t, 0)),
                   pl.BlockSpec((1, NH_M, DQK_M, DV_M), lambda b, t: (b, 0, 0, 0)),
                   pl.BlockSpec((1, NH_M, DQK_M), bmap),
                   pl.BlockSpec((1, 1, LANES), bmap),
                   pl.BlockSpec((1, CONV_W - 1, 2 * QK_M), bmap)],
        out_shape=[jax.ShapeDtypeStruct((B, T, VM), BF16),
                   jax.ShapeDtypeStruct((B, NH_M, DQK_M, DV_M), F32),
                   jax.ShapeDtypeStruct((B, NH_M, DQK_M), F32),
                   jax.ShapeDtypeStruct((B, 1, LANES), F32),
                   jax.ShapeDtypeStruct((B, CONV_W - 1, 2 * QK_M), F32)],
        scratch_shapes=[pltpu.VMEM((8 + Tt, 2 * QK_M), F32),
                        pltpu.VMEM((Tt, QK_M), F32),
                        pltpu.VMEM((Tt, QK_M), BF16),
                        pltpu.VMEM((Tt, QK_M), BF16),
                        pltpu.VMEM((Tt, VM), BF16),
                        pltpu.VMEM((Tt, VM), F32),
                        pltpu.VMEM((NH_M, DQK_M, Tt), BF16),
                        pltpu.VMEM((Tt // L * NH_M, DQK_M, DV_M), F32),
                        pltpu.VMEM((Tt // L * NH_M, DQK_M), F32),
                        pltpu.VMEM((Tt, VM), F32),
                        pltpu.VMEM((Tt, LANES), F32),
                        pltpu.VMEM((Tt, LANES), F32),
                        pltpu.VMEM((Tt, LANES), F32),
                        pltpu.VMEM((NH_M, DQK_M), F32),
                        pltpu.VMEM((1, LANES), F32)] + [pltpu.VMEM((DQK_M, DV_M), F32)] * NH_M,
        compiler_params=_cparams(2),
        name="mlstm",
    )(x, ada, w, cw, cb, bif, ng, c0t, n0, m0, cv0)


def _pair_cols(x, hd, lane):
    return jnp.where(lane < P_S, x[:, hd:hd + 1], x[:, hd + 1:hd + 2])


def _ssd_kernel(x_ref, ada_ref, w_ref, cw_ref, cb_ref, dtb_ref, alog_ref, d_ref, ng_ref, h0_ref, cv0_ref,
                out_ref, h_out, cv_out,
                conv_s, zg_s, x_s, xw_s, b_s, c_s, bt_s, e2_s, un_s, dec_s, y_s, *ht_s, Tt, L):
    t = pl.program_id(1)
    nc = Tt // L
    n_pairs = NH_S // 2

    @pl.when(t == 0)
    def _():
        for g in range(G_S):
            ht_s[g][...] = h0_ref[0, g]
        conv_s[5:8, :] = cv0_ref[0]

    u = _modulate(x_ref[0], ada_ref[0], 0, 1).astype(BF16)
    dt = _softplus(_dot(u, w_ref[0, :, XS + CONV_S_DIM:]) + dtb_ref[0])
    xbc = _dot(u, w_ref[0, :, XS:XS + CONV_S_DIM])
    a_row = -jnp.exp(alog_ref[0])

    tril = _block_tril(Tt, L)
    full = _block_full(Tt, L)
    hi, mid, lo = _split3(dt * a_row)
    cs = _dot(tril, lo) + _dot(tril, mid) + _dot(tril, hi)
    csl = _dot(full, lo) + _dot(full, mid) + _dot(full, hi)
    xbc = _silu(_causal_conv(conv_s, xbc, cw_ref, cb_ref, Tt))
    x = xbc[:, :XS]
    bm = xbc[:, XS:XS + BC_S]
    x_s[...] = x
    b_s[...] = bm.astype(BF16)
    c_s[...] = xbc[:, XS + BC_S:].astype(BF16)
    wsrc = jnp.exp(csl - cs) * dt
    dec = jnp.exp(csl)
    cs_rows = _transpose_f32(cs)[0:NH_S, :]
    dt_rows = _transpose_f32(dt)[0:NH_S, :]
    for g in range(G_S):
        bt_s[g] = _transpose_f32(bm[:, g * N_S:(g + 1) * N_S]).astype(BF16)
    zg_s[...] = _silu(_dot(u, w_ref[0, :, 0:XS]))
    lane_t = lax.broadcasted_iota(jnp.int32, (Tt, LANES), 1)
    for pp in range(n_pairs):
        pc = slice(pp * LANES, (pp + 1) * LANES)
        e2_s[:, pc] = _pair_cols(cs, 2 * pp, lane_t)
        xw_s[:, pc] = (x[:, pc] * _pair_cols(wsrc, 2 * pp, lane_t)).astype(BF16)
    lane1 = lax.broadcasted_iota(jnp.int32, (1, LANES), 1)
    for c in range(nc):
        for pp in range(n_pairs):
            r0 = c * L
            dec_s[c * n_pairs + pp:c * n_pairs + pp + 1, :] = _pair_cols(dec[r0:r0 + 1, :], 2 * pp, lane1)

    rr = lax.broadcasted_iota(jnp.int32, (L, LANES), 0)
    lane_l = lax.broadcasted_iota(jnp.int32, (L, LANES), 1)
    causal2 = (lane_l % P_S) <= rr
    col = lax.broadcasted_iota(jnp.int32, (N_S, Tt), 1)
    zero_b = jnp.zeros((), BF16)
    for c in range(nc):
        rows = slice(c * L, (c + 1) * L)
        for g in range(G_S):
            gs = slice(g * N_S, (g + 1) * N_S)
            bg = b_s[rows, gs]
            cb2 = _dot_nt(c_s[rows, gs], jnp.concatenate([bg, bg], axis=0))
            btm = jnp.where((col // L) == c, bt_s[g], zero_b) if nc > 1 else bt_s[g]
            un_s[c * G_S + g] = _dot(btm, xw_s[:, g * 2 * LANES:(g + 1) * 2 * LANES])
            for p2 in range(2):
                pp = g * 2 + p2
                hd = 2 * pp
                pc = slice(pp * LANES, (pp + 1) * LANES)
                csr2 = jnp.concatenate([cs_rows[hd:hd + 1, rows], cs_rows[hd + 1:hd + 2, rows]], axis=1)
                dtr2 = jnp.concatenate([dt_rows[hd:hd + 1, rows], dt_rows[hd + 1:hd + 2, rows]], axis=1)
                seg2 = jnp.exp(jnp.where(causal2, e2_s[rows, pc] - csr2, NEG_INF))
                m2 = (cb2 * seg2 * dtr2).astype(BF16)
                xp = x_s[rows, pc].astype(BF16)
                xbd = jnp.concatenate([jnp.where(lane_l < P_S, xp, zero_b), jnp.where(lane_l >= P_S, xp, zero_b)], axis=0)
                y_s[rows, pc] = _dot(m2, xbd)

    for c in range(nc):
        rows = slice(c * L, (c + 1) * L)
        for g in range(G_S):
            gc = slice(g * 2 * LANES, (g + 1) * 2 * LANES)
            yi = _dot(c_s[rows, g * N_S:(g + 1) * N_S], ht_s[g][...].astype(BF16))
            y_s[rows, gc] += jnp.exp(e2_s[rows, gc]) * yi
        for g in range(G_S):
            for p2 in range(2):
                pp = g * 2 + p2
                hc = slice(p2 * LANES, (p2 + 1) * LANES)
                ht_s[g][:, hc] = (dec_s[c * n_pairs + pp:c * n_pairs + pp + 1, :] * ht_s[g][:, hc]
                                  + un_s[c * G_S + g][:, hc])

    y = (y_s[...] + d_ref[0] * x_s[...]) * zg_s[...]
    out_ref[0] = (_group_norm(y, G_S, False) * ng_ref[0]).astype(out_ref.dtype)

    @pl.when(t == pl.num_programs(1) - 1)
    def _():
        for g in range(G_S):
            h_out[0, g] = ht_s[g][...]
        cv_out[0] = conv_s[5:8, :]


def _ssd_call(l, x, ada, w, cw, cb, dtb, alog, dfull, ng, h0t, cv0, Tt):
    B, T, D = x.shape
    L = min(CHUNK, T)
    wcols = w.shape[2]
    bmap = lambda b, t: (b, 0, 0)
    lmap = lambda b, t: (l, 0, 0)
    hshape = (G_S, N_S, HPG_S * P_S)
    assert L == P_S and 2 * L == LANES and T % Tt == 0 and Tt % L == 0
    return pl.pallas_call(
        functools.partial(_ssd_kernel, Tt=Tt, L=L),
        grid=(B, T // Tt),
        in_specs=[pl.BlockSpec((1, Tt, D), lambda b, t: (b, t, 0)),
                  pl.BlockSpec((1, 6, D), bmap),
                  pl.BlockSpec((1, D, wcols), lmap),
                  pl.BlockSpec((1, CONV_W, CONV_S_DIM), lmap),
                  pl.BlockSpec((1, 1, CONV_S_DIM), lmap),
                  pl.BlockSpec((1, 1, LANES), lmap),
                  pl.BlockSpec((1, 1, LANES), lmap),
                  pl.BlockSpec((1, 1, XS), lmap),
                  pl.BlockSpec((1, 1, XS), lmap),
                  pl.BlockSpec((1,) + hshape, lambda b, t: (b, 0, 0, 0)),
                  pl.BlockSpec((1, CONV_W - 1, CONV_S_DIM), bmap)],
        out_specs=[pl.BlockSpec((1, Tt, XS), lambda b, t: (b, t, 0)),
                   pl.BlockSpec((1,) + hshape, lambda b, t: (b, 0, 0, 0)),
                   pl.BlockSpec((1, CONV_W - 1, CONV_S_DIM), bmap)],
        out_shape=[jax.ShapeDtypeStruct((B, T, XS), BF16),
                   jax.ShapeDtypeStruct((B,) + hshape, F32),
                   jax.ShapeDtypeStruct((B, CONV_W - 1, CONV_S_DIM), F32)],
        scratch_shapes=[pltpu.VMEM((8 + Tt, CONV_S_DIM), F32),
                        pltpu.VMEM((Tt, XS), F32),
                        pltpu.VMEM((Tt, XS), F32),
                        pltpu.VMEM((Tt, XS), BF16),
                        pltpu.VMEM((Tt, BC_S), BF16),
                        pltpu.VMEM((Tt, BC_S), BF16),
                        pltpu.VMEM((G_S, N_S, Tt), BF16),
                        pltpu.VMEM((Tt, XS), F32),
                        pltpu.VMEM((Tt // L * G_S, N_S, HPG_S * P_S), F32),
                        pltpu.VMEM((Tt // L * (NH_S // 2), LANES), F32),
                        pltpu.VMEM((Tt, XS), F32)] + [pltpu.VMEM(hshape[1:], F32)] * G_S,
        compiler_params=_cparams(2),
        name="ssd",
    )(x, ada, w, cw, cb, dtb, alog, dfull, ng, h0t, cv0)


def _gla_kernel(x_ref, ada_ref, w_ref, wlr_ref, blr_ref, ng_ref, s0_ref,
                out_ref, s_out,
                sg_s, qg_s, kg_s, v_s, kdt_s, egt_s, un_s, o_s, *st_s, Tt, L):
    t = pl.program_id(1)
    nc = Tt // L

    @pl.when(t == 0)
    def _():
        for h in range(NH_G):
            st_s[h][...] = s0_ref[0, h]

    u = _modulate(x_ref[0], ada_ref[0], 0, 1).astype(BF16)
    lr = _dot(u, w_ref[0, :, 2 * QK_G + 2 * VG:]).astype(BF16)
    lg = _log_sigmoid(_dot(lr, wlr_ref[0]) + blr_ref[0]) / TAU_G
    q = _dot(u, w_ref[0, :, 0:QK_G])
    k = _dot(u, w_ref[0, :, QK_G:2 * QK_G]) * DK_G ** -0.5

    tril = _block_tril(Tt, L)
    hi, mid, lo = _split3(lg)
    G = _dot(tril, lo) + _dot(tril, mid) + _dot(tril, hi)
    v_s[...] = _dot(u, w_ref[0, :, 2 * QK_G:2 * QK_G + VG]).astype(BF16)
    if nc > 1:
        Gl = _chunk_last(G, L)
    else:
        full = _block_full(Tt, L)
        Gl = _dot(full, lo) + _dot(full, mid) + _dot(full, hi)
    qg_s[...] = (q * jnp.exp(G)).astype(BF16)
    kg_s[...] = (k * jnp.exp(-G)).astype(BF16)
    kd = k * jnp.exp(Gl - G)
    eg = jnp.exp(Gl)
    sg_s[...] = _silu(_dot(u, w_ref[0, :, 2 * QK_G + VG:2 * QK_G + 2 * VG]))
    for h in range(NH_G):
        ks = slice(h * DK_G, (h + 1) * DK_G)
        kdt_s[h] = _transpose_f32(kd[:, ks]).astype(BF16)
        egt_s[h] = _transpose_f32(eg[:, ks])
    col = lax.broadcasted_iota(jnp.int32, (DK_G, Tt), 1)
    for c in range(nc):
        for h in range(NH_G):
            vs = slice(h * DV_G, (h + 1) * DV_G)
            kdm = jnp.where((col // L) == c, kdt_s[h], jnp.zeros((), BF16)) if nc > 1 else kdt_s[h]
            un_s[c * NH_G + h] = _dot(kdm, v_s[:, vs])

    rr = lax.broadcasted_iota(jnp.int32, (L, L), 0)
    cc = lax.broadcasted_iota(jnp.int32, (L, L), 1)
    causal = cc <= rr
    for c in range(nc):
        rows = slice(c * L, (c + 1) * L)
        att = []
        for h in range(NH_G):
            ks = slice(h * DK_G, (h + 1) * DK_G)
            att.append(_dot_nt(qg_s[rows, ks], kg_s[rows, ks]))
        for h in range(NH_G):
            ks = slice(h * DK_G, (h + 1) * DK_G)
            vs = slice(h * DV_G, (h + 1) * DV_G)
            o_s[rows, vs] = _dot(qg_s[rows, ks], st_s[h][...].astype(BF16))
        for h in range(NH_G):
            vs = slice(h * DV_G, (h + 1) * DV_G)
            o_s[rows, vs] += _dot(jnp.where(causal, att[h], 0.0).astype(BF16), v_s[rows, vs])
        for h in range(NH_G):
            st_s[h][...] = egt_s[h][:, c * L:c * L + 1] * st_s[h][...] + un_s[c * NH_G + h]

    out_ref[0] = (_group_norm(o_s[...], NH_G, False) * ng_ref[0] * sg_s[...]).astype(out_ref.dtype)

    @pl.when(t == pl.num_programs(1) - 1)
    def _():
        for h in range(NH_G):
            s_out[0, h] = st_s[h][...]


def _gla_call(l, x, ada, w, wlr, blr, ng, s0, Tt):
    B, T, D = x.shape
    L = min(CHUNK, T)
    wcols = w.shape[2]
    bmap = lambda b, t: (b, 0, 0)
    lmap = lambda b, t: (l, 0, 0)
    sshape = (NH_G, DK_G, DV_G)
    return pl.pallas_call(
        functools.partial(_gla_kernel, Tt=Tt, L=L),
        grid=(B, T // Tt),
        in_specs=[pl.BlockSpec((1, Tt, D), lambda b, t: (b, t, 0)),
                  pl.BlockSpec((1, 6, D), bmap),
                  pl.BlockSpec((1, D, wcols), lmap),
                  pl.BlockSpec((1, LANES, QK_G), lmap),
                  pl.BlockSpec((1, 1, QK_G), lmap),
                  pl.BlockSpec((1, 1, VG), lmap),
                  pl.BlockSpec((1,) + sshape, lambda b, t: (b, 0, 0, 0))],
        out_specs=[pl.BlockSpec((1, Tt, VG), lambda b, t: (b, t, 0)),
                   pl.BlockSpec((1,) + sshape, lambda b, t: (b, 0, 0, 0))],
        out_shape=[jax.ShapeDtypeStruct((B, T, VG), BF16),
                   jax.ShapeDtypeStruct((B,) + sshape, F32)],
        scratch_shapes=[pltpu.VMEM((Tt, VG), F32),
                        pltpu.VMEM((Tt, QK_G), BF16),
                        pltpu.VMEM((Tt, QK_G), BF16),
                        pltpu.VMEM((Tt, VG), BF16),
                        pltpu.VMEM((NH_G, DK_G, Tt), BF16),
                        pltpu.VMEM((NH_G, DK_G, Tt), F32),
                        pltpu.VMEM((Tt // L * NH_G, DK_G, DV_G), F32),
                        pltpu.VMEM((Tt, VG), F32)] + [pltpu.VMEM(sshape[1:], F32)] * NH_G,
        compiler_params=_cparams(2),
        name="gla",
    )(x, ada, w, wlr, blr, ng, s0)


def _layer_norm(x, g, b):
    mu = jnp.mean(x, axis=1, keepdims=True)
    xc = x - mu
    var = jnp.mean(xc * xc, axis=1, keepdims=True)
    return xc * lax.rsqrt(var + LN_EPS) * g + b


def _merge_kernel(x_ref, ada_ref, hm_ref, ys_ref, og_ref, wg_ref, bb_ref, wb_ref, wo_ref, lng_ref, lnb_ref,
                  wrt_ref, brt_ref,
                  x1_ref, u2_ref, eid_ref, wts_ref, cnt_ref, cnt_s, *, alpha):
    nb, tm_seq, D = x_ref.shape
    rows = nb * tm_seq
    ada = ada_ref[...]

    def per_row(i):
        return jnp.broadcast_to(ada[:, i:i + 1, :], (nb, tm_seq, D)).reshape(rows, D)

    x = x_ref[...].reshape(rows, D)
    u = (x * (1.0 + per_row(1)) + per_row(0)).astype(BF16)
    gate = jax.nn.sigmoid(_dot(u, wg_ref[0]) + bb_ref[0])
    merged = (gate[:, 0:D] * _dot(hm_ref[...].reshape(rows, D), wb_ref[0, 0])
              + gate[:, D:2 * D] * _dot(ys_ref[...].reshape(rows, D), wb_ref[0, 1])
              + gate[:, 2 * D:3 * D] * _dot(og_ref[...].reshape(rows, D), wb_ref[0, 2]))
    y = _dot(merged.astype(BF16), wo_ref[0])
    x1 = _layer_norm(alpha * x + per_row(2) * y, lng_ref[0, 0:1], lnb_ref[0, 0:1])
    x1_ref[...] = x1.reshape(nb, tm_seq, D)
    u2 = (x1 * (1.0 + per_row(4)) + per_row(3)).astype(BF16)
    bits = lax.bitcast_convert_type(u2.astype(F32), jnp.uint32)
    u2_ref[...] = ((bits[:, :D // 2] >> 16) | bits[:, D // 2:]).reshape(nb, tm_seq, D // 2)

    logits = _dot(u2, wrt_ref[0]) + brt_ref[0]
    lane = lax.broadcasted_iota(jnp.int32, logits.shape, 1)
    lane_f = lane.astype(F32)
    big = float(LANES)
    is_g = (lane >= N_EXPERTS) & (lane < N_EXPERTS + N_GROUPS)
    gmax = jnp.max(jnp.where(is_g, logits, NEG_INF), axis=1, keepdims=True)
    gsum = jnp.sum(jnp.where(is_g, jnp.exp(logits - gmax), 0.0), axis=1, keepdims=True)
    pg_top = 1.0 / gsum
    g_lane = jnp.min(jnp.where(is_g & (logits == gmax), lane_f, big), axis=1, keepdims=True)
    g_idx = g_lane.astype(jnp.int32) - N_EXPERTS
    in_grp = (lane < N_EXPERTS) & ((lane // EXP_PER_GROUP) == g_idx)
    el = jnp.where(in_grp, logits, NEG_INF)
    v1 = jnp.max(el, axis=1, keepdims=True)
    i1 = jnp.min(jnp.where(in_grp & (el == v1), lane_f, big), axis=1, keepdims=True)
    rest = in_grp & (lane_f != i1)
    el2 = jnp.where(rest, logits, NEG_INF)
    v2 = jnp.max(el2, axis=1, keepdims=True)
    i2 = jnp.min(jnp.where(rest & (el2 == v2), lane_f, big), axis=1, keepdims=True)
    e = jnp.exp(v2 - v1)
    w1 = pg_top / (1.0 + e)
    w2 = pg_top * e / (1.0 + e)
    @pl.when((pl.program_id(0) == 0) & (pl.program_id(1) == 0))
    def _():
        cnt_s[...] = jnp.zeros_like(cnt_s)

    tm = logits.shape[0]
    hit1 = lane_f == i1
    hit2 = lane_f == i2
    onehot = jnp.where(hit1 | hit2, 1.0, 0.0)
    rr = lax.broadcasted_iota(jnp.int32, (tm, tm), 0)
    cc = lax.broadcasted_iota(jnp.int32, (tm, tm), 1)
    before = _dot(jnp.where(cc < rr, 1.0, 0.0).astype(BF16), onehot.astype(BF16)) + cnt_s[...]
    r1 = jnp.sum(jnp.where(hit1, before, 0.0), axis=1, keepdims=True)
    r2 = jnp.sum(jnp.where(hit2, before, 0.0), axis=1, keepdims=True)
    cnt_s[...] = cnt_s[...] + jnp.sum(onehot, axis=0, keepdims=True)
    cnt_ref[0] = cnt_s[...]
    eid_ref[...] = jnp.where(lane == 0, i1, jnp.where(lane == 1, i2, jnp.where(lane == 2, r1, jnp.where(lane == 3, r2, 0.0)))).astype(jnp.int32).reshape(nb, tm_seq, LANES)
    wts_ref[...] = jnp.where(lane == 0, w1, jnp.where(lane == 1, w2, 0.0)).reshape(nb, tm_seq, LANES)


def _merge_call(l, x, ada, hm, ys, og, wg, bb, wb, wo, lng, lnb, wrt, brt, tm, alpha):
    B, T, D = x.shape
    nb = max(1, min(B, MERGE_ROWS // tm)) if tm == T else 1
    while B % nb:
        nb -= 1
    bmap = lambda b, t: (b, 0, 0)
    lmap = lambda b, t: (l, 0, 0)
    tmap = lambda b, t: (b, t, 0)
    return pl.pallas_call(
        functools.partial(_merge_kernel, alpha=alpha),
        grid=(B // nb, T // tm),
        in_specs=[pl.BlockSpec((nb, tm, D), tmap),
                  pl.BlockSpec((nb, 6, D), bmap),
                  pl.BlockSpec((nb, tm, D), tmap),
                  pl.BlockSpec((nb, tm, D), tmap),
                  pl.BlockSpec((nb, tm, D), tmap),
                  pl.BlockSpec((1, D, 3 * D), lmap),
                  pl.BlockSpec((1, 1, 3 * D), lmap),
                  pl.BlockSpec((1, 3, D, D), lambda b, t: (l, 0, 0, 0)),
                  pl.BlockSpec((1, D, D), lmap),
                  pl.BlockSpec((1, 2, D), lmap),
                  pl.BlockSpec((1, 2, D), lmap),
                  pl.BlockSpec((1, D, LANES), lmap),
                  pl.BlockSpec((1, 1, LANES), lmap)],
        out_specs=[pl.BlockSpec((nb, tm, D), tmap),
                   pl.BlockSpec((nb, tm, D // 2), tmap),
                   pl.BlockSpec((nb, tm, LANES), tmap),
                   pl.BlockSpec((nb, tm, LANES), tmap),
                   pl.BlockSpec((1, 1, LANES), lambda b, t: (0, 0, 0))],
        out_shape=[jax.ShapeDtypeStruct((B, T, D), F32),
                   jax.ShapeDtypeStruct((B, T, D // 2), jnp.uint32),
                   jax.ShapeDtypeStruct((B, T, LANES), jnp.int32),
                   jax.ShapeDtypeStruct((B, T, LANES), F32),
                   jax.ShapeDtypeStruct((1, 1, LANES), F32)],
        scratch_shapes=[pltpu.VMEM((1, LANES), F32)],
        compiler_params=_cparams(2),
        name="merge",
    )(x, ada, hm, ys, og, wg, bb, wb, wo, lng, lnb, wrt, brt)


def _moe_kernel(te_ref, xs_ref, wg_ref, wu_ref, wd_ref, o_ref, wg_b, wu_b, wd_b):
    i = pl.program_id(0)
    prev = te_ref[jnp.maximum(i - 1, 0)]

    @pl.when((i == 0) | (te_ref[i] != prev))
    def _():
        wg_b[...] = wg_ref[0, 0].astype(BF16)
        wu_b[...] = wu_ref[0, 0].astype(BF16)
        wd_b[...] = wd_ref[0, 0].astype(BF16)

    w = xs_ref[...]
    lo = lax.bitcast_convert_type(w << 16, F32)
    hi = lax.bitcast_convert_type(w & jnp.uint32(0xFFFF0000), F32)
    xs = jnp.concatenate([lo, hi], axis=1).astype(BF16)
    hg = _dot(xs, wg_b[...])
    hu = _dot(xs, wu_b[...])
    o_ref[...] = _dot((_silu(hg) * hu).astype(BF16), wd_b[...])


def _moe_call(l, tile_expert, xs, weg, weu, wed, tm):
    R = xs.shape[0]
    D, dexp = weg.shape[2], weg.shape[3]
    return pl.pallas_call(
        _moe_kernel,
        grid_spec=pltpu.PrefetchScalarGridSpec(
            num_scalar_prefetch=1,
            grid=(R // tm,),
            in_specs=[pl.BlockSpec((tm, D // 2), lambda i, te: (i, 0)),
                      pl.BlockSpec((1, 1, D, dexp), lambda i, te: (l, te[i], 0, 0)),
                      pl.BlockSpec((1, 1, D, dexp), lambda i, te: (l, te[i], 0, 0)),
                      pl.BlockSpec((1, 1, dexp, D), lambda i, te: (l, te[i], 0, 0))],
            out_specs=pl.BlockSpec((tm, D), lambda i, te: (i, 0)),
            scratch_shapes=[pltpu.VMEM((D, dexp), BF16), pltpu.VMEM((D, dexp), BF16), pltpu.VMEM((dexp, D), BF16)]),
        out_shape=jax.ShapeDtypeStruct((R, D), F32),
        compiler_params=_cparams(1),
        name="moe",
    )(tile_expert, xs, weg, weu, wed)


def _ln2_kernel(x1_ref, ada_ref, y0_ref, y1_ref, wts_ref, lng_ref, lnb_ref, o_ref, *, alpha):
    ada = ada_ref[0]
    wts = wts_ref[0]
    moe = y0_ref[0] * wts[:, 0:1] + y1_ref[0] * wts[:, 1:2]
    o_ref[0] = _layer_norm(alpha * x1_ref[0] + ada[5:6] * moe, lng_ref[0, 1:2], lnb_ref[0, 1:2])


def _ln2_call(l, x1, ada, y0, y1, wts, lng, lnb, tm, alpha):
    B, T, D = x1.shape
    bmap = lambda b, t: (b, 0, 0)
    lmap = lambda b, t: (l, 0, 0)
    tmap = lambda b, t: (b, t, 0)
    return pl.pallas_call(
        functools.partial(_ln2_kernel, alpha=alpha),
        grid=(B, T // tm),
        in_specs=[pl.BlockSpec((1, tm, D), tmap),
                  pl.BlockSpec((1, 6, D), bmap),
                  pl.BlockSpec((1, tm, D), tmap),
                  pl.BlockSpec((1, tm, D), tmap),
                  pl.BlockSpec((1, tm, LANES), tmap),
                  pl.BlockSpec((1, 2, D), lmap),
                  pl.BlockSpec((1, 2, D), lmap)],
        out_specs=pl.BlockSpec((1, tm, D), tmap),
        out_shape=jax.ShapeDtypeStruct((B, T, D), F32),
        compiler_params=_cparams(2),
        name="ln2",
    )(x1, ada, y0, y1, wts, lng, lnb)


def _take_rows(a, idx):
    return a.at[idx].get(mode="promise_in_bounds")


def _lookup(table, idx):
    sel = idx[:, None] == jnp.arange(table.shape[0], dtype=idx.dtype)[None, :]
    return jnp.sum(jnp.where(sel, table[None, :], 0), axis=1)


def _moe_tile(n_asg):
    return MOE_TILE if n_asg >= 4 * MOE_TILE * N_EXPERTS else MOE_TILE_SMALL


def _route(eid, rank, counts, n_tok, tm):
    flat = eid.reshape(-1)
    n_asg = flat.shape[0]
    n_rows = (-(-n_asg // tm) + N_EXPERTS) * tm
    sizes = counts.astype(jnp.int32)
    psz = ((sizes + tm - 1) // tm) * tm
    pend = jnp.cumsum(psz)
    poff = pend - psz
    dest = _lookup(poff, flat) + rank.reshape(-1)
    fill_end = jnp.cumsum(psz - sizes)
    fill = jnp.arange(n_rows - n_asg, dtype=jnp.int32)
    fill_key = jnp.sum((fill_end[None, :] <= fill[:, None]).astype(jnp.int32), axis=1)
    order = jnp.argsort(jnp.concatenate([flat, fill_key])).astype(jnp.int32)
    row_tok = jnp.where(order < n_asg, order // TOP_K, jnp.arange(n_rows, dtype=jnp.int32) % n_tok)
    tile_start = jnp.arange(n_rows // tm, dtype=jnp.int32) * tm
    tile_expert = jnp.minimum(jnp.sum((pend[None, :] <= tile_start[:, None]).astype(jnp.int32), axis=1),
                              N_EXPERTS - 1)
    return row_tok, dest.reshape(n_tok, TOP_K), tile_expert


def _pad_lanes(a, width=LANES):
    return jnp.pad(a, [(0, 0)] * (a.ndim - 1) + [(0, width - a.shape[-1])])


def _time_block(T):
    return min(T, 256)


MERGE_ROWS = 512


def _merge_rows(T):
    return min(T, MERGE_ROWS)


def _ln2_rows(T):
    return min(T, 1024)


def _trunk(x, ada_all, states, P, depth, alpha):
    B, T, D = x.shape
    Tt = _time_block(T)
    c_m, n_m, m_m, cv_m, h_s, cv_s, s_g = states
    new = [[] for _ in range(7)]
    for l in range(depth):
        ada = ada_all[l].reshape(B, 6, D)
        c0t = jnp.swapaxes(c_m[l], -1, -2)
        m0 = _pad_lanes(m_m[l])[:, None, :]
        h0t = (h_s[l].reshape(B, G_S, HPG_S, P_S, N_S).transpose(0, 1, 4, 2, 3)
               .reshape(B, G_S, N_S, HPG_S * P_S))
        hm, c_t, n_n, m_n, cvm_n = _mlstm_call(l, x, ada, P["w_mlstm"], P["mlstm_conv_w"], P["mlstm_conv_b"],
                                               P["mlstm_bif"], P["mlstm_norm_g"], c0t, n_m[l], m0, cv_m[l], Tt)
        ys, h_t, cvs_n = _ssd_call(l, x, ada, P["w_ssd"], P["ssd_conv_w"], P["ssd_conv_b"], P["ssd_dtb"],
                                   P["ssd_alog"], P["ssd_dfull"], P["ssd_norm_g"], h0t, cv_s[l], Tt)
        og, s_n = _gla_call(l, x, ada, P["w_gla"], P["gla_w_lr"], P["gla_b_lr"], P["gla_norm_g"], s_g[l], Tt)
        x1, u2, eid, wts, cnt = _merge_call(l, x, ada, hm, ys, og, P["w_gate"], P["b_branch"], P["w_branch"],
                                       P["w_out"], P["ln_g"], P["ln_b"], P["w_rt"], P["b_rt"], _merge_rows(T), alpha)
        moe_tm = _moe_tile(B * T * TOP_K)
        row_tok, dest, tile_expert = _route(eid[:, :, :TOP_K], eid[:, :, TOP_K:2 * TOP_K], cnt[0, 0, :N_EXPERTS], B * T, moe_tm)
        xs = _take_rows(u2.reshape(B * T, D // 2), row_tok)
        ye = _moe_call(l, tile_expert, xs, P["w_e_gate"], P["w_e_up"], P["w_e_down"], moe_tm)
        y0 = _take_rows(ye, dest[:, 0]).reshape(B, T, D)
        y1 = _take_rows(ye, dest[:, 1]).reshape(B, T, D)
        x = _ln2_call(l, x1, ada, y0, y1, wts, P["ln_g"], P["ln_b"], _ln2_rows(T), alpha)

        new[0].append(jnp.swapaxes(c_t, -1, -2))
        new[1].append(n_n)
        new[2].append(m_n[:, 0, :NH_M])
        new[3].append(cvm_n)
        new[4].append(h_t.reshape(B, G_S, N_S, HPG_S, P_S).transpose(0, 1, 3, 4, 2).reshape(B, NH_S, P_S, N_S))
        new[5].append(cvs_n)
        new[6].append(s_n)
    return x, tuple(jnp.stack(lst) for lst in new)


def kernel(x_prompt, x_sample, state_mlstm_c, state_mlstm_n, state_mlstm_m, state_mlstm_conv, state_ssd, state_ssd_conv, state_gla, c_prompt, c_sample, w_ada, b_ada, w_in, mlstm_b_i, mlstm_b_f, mlstm_conv_w, mlstm_conv_b, mlstm_norm_g, ssd_conv_w, ssd_conv_b, ssd_dt_bias, ssd_a_log, ssd_d, ssd_norm_g, gla_w_lr, gla_b_lr, gla_norm_g, b_branch, w_branch, w_out, ln_g, ln_b, w_grp, b_grp, w_router, b_router, w_e_gate, w_e_up, w_e_down):
    depth, D, _ = w_in.shape
    alpha = (2 * depth) ** 0.25
    nbp = x_prompt.shape[0]

    edges = np.concatenate([[0], np.cumsum(COL_SIZES)])
    col = {n: w_in[:, :, int(edges[i]):int(edges[i + 1])] for i, n in enumerate(COL_NAMES)}
    w_gate = w_in[:, :, int(edges[-1]):].astype(BF16)
    cat = lambda parts: jnp.concatenate(parts, axis=-1).astype(BF16)
    row = lambda a: a[:, None, :]
    P = {
        "w_mlstm": cat([col["qk_m"], col["v_m"], col["o_m"], _pad_lanes(jnp.concatenate([col["i_m"], col["f_m"]], -1))]),
        "w_ssd": cat([col["z_s"], col["xbc_s"], _pad_lanes(col["dt_s"])]),
        "w_gla": cat([col["q_g"], col["k_g"], col["v_g"], col["g_g"], _pad_lanes(col["lr_g"])]),
        "w_gate": w_gate,
        "mlstm_conv_w": mlstm_conv_w, "mlstm_conv_b": row(mlstm_conv_b),
        "mlstm_bif": row(_pad_lanes(jnp.concatenate([mlstm_b_i, mlstm_b_f], -1))),
        "mlstm_norm_g": row(mlstm_norm_g),
        "ssd_conv_w": ssd_conv_w, "ssd_conv_b": row(ssd_conv_b),
        "ssd_dtb": row(_pad_lanes(ssd_dt_bias)), "ssd_alog": row(_pad_lanes(ssd_a_log)),
        "ssd_dfull": row(jnp.repeat(ssd_d, P_S, axis=-1)), "ssd_norm_g": row(ssd_norm_g),
        "gla_w_lr": jnp.pad(gla_w_lr, ((0, 0), (0, LANES - R_G), (0, 0))).astype(BF16),
        "gla_b_lr": row(gla_b_lr), "gla_norm_g": row(gla_norm_g),
        "b_branch": row(b_branch), "w_branch": w_branch.astype(BF16), "w_out": w_out.astype(BF16),
        "ln_g": ln_g, "ln_b": ln_b,
        "w_rt": _pad_lanes(jnp.concatenate([w_router, w_grp], -1)).astype(BF16),
        "b_rt": row(_pad_lanes(jnp.concatenate([b_router, b_grp], -1))),
        "w_e_gate": w_e_gate, "w_e_up": w_e_up, "w_e_down": w_e_down,
    }

    ada_all = _ada_call(jnp.concatenate([c_prompt, c_sample], axis=0), w_ada, b_ada)

    sample_states = (state_mlstm_c, state_mlstm_n, state_mlstm_m, state_mlstm_conv,
                     state_ssd, state_ssd_conv, state_gla)
    prompt_states = tuple(jnp.zeros((s.shape[0], nbp) + s.shape[2:], x_prompt.dtype) for s in sample_states)

    y_prompt, new_p = _trunk(x_prompt, ada_all[:, :nbp], prompt_states, P, depth, alpha)
    y_sample, new_s = _trunk(x_sample, ada_all[:, nbp:], sample_states, P, depth, alpha)
    return (y_prompt, y_sample) + new_p + new_s
```
